```python
import math
import jax, jax.numpy as jnp
from jax import lax
import numpy as np

D_MODEL = 1024
BATCH = 4
SEQ = 8192
DEPTH = 2

GRID_W = 64
CTX_LEN = 256
HEAD_DIM = 64
A_HEADS = 4
A_KV_HEADS = 2
B_HEADS = 4
B_DK = 32
B_DV = 64
C_HEADS = 4
C_KV_HEADS = 2
WINDOW = 128
D_HEADS = 4
NA_ROWS_MAX = 8
NA_COLS = 16
Q_BLOCK = 128
N_BAND = 1 + 2 * (WINDOW // Q_BLOCK)
ROPE_THETA = 10000.0
N_BRANCH = 4
BRANCH_W = 256
N_EXPERTS = 32
TOP_K = 4
D_FF = D_MODEL
SWIGLU_LIMIT = 7.0
SWIGLU_ALPHA = 1.702
EXPERT_BLOCK = 256
N_MOD = 6
EPS = 1e-6
NEG_INF = -1e30
Q_SIZES = (A_HEADS * HEAD_DIM, B_HEADS * 2 * B_DK, C_HEADS * HEAD_DIM, D_HEADS * HEAD_DIM)
K_SIZES = (A_KV_HEADS * HEAD_DIM, B_HEADS * 2 * B_DK, C_KV_HEADS * HEAD_DIM, D_HEADS * HEAD_DIM)
V_SIZES = (A_KV_HEADS * HEAD_DIM, B_HEADS * B_DV, C_KV_HEADS * HEAD_DIM, D_HEADS * HEAD_DIM)
Q_TOT = sum(Q_SIZES)
KV_TOT = sum(K_SIZES) + sum(V_SIZES)
GATE_TOT = N_BRANCH * D_MODEL
W_IN_COLS = Q_TOT + KV_TOT + GATE_TOT

kernel_name = 'hybrid_diffusion_backbone'


def rms_norm(x, g):
    xf = x.astype(jnp.float32)
    y = xf * lax.rsqrt(jnp.mean(xf * xf, axis=-1, keepdims=True) + EPS)
    return (y * g.astype(jnp.float32)).astype(x.dtype)


def modulate(h, shift, scale):
    return h * (1 + scale) + shift


def heads(x, n):
    return x.reshape(x.shape[:-1] + (n, -1))


def split_cols(x, sizes):
    return jnp.split(x, np.cumsum(sizes)[:-1].tolist(), axis=-1)


def rope_2d(n_tok, dim):
    t = jnp.arange(n_tok, dtype=jnp.int32)
    row = (t // GRID_W).astype(jnp.float32)
    col = (t % GRID_W).astype(jnp.float32)
    n_freq = dim // 4
    inv = ROPE_THETA ** (-jnp.arange(n_freq, dtype=jnp.float32) / n_freq)
    ang = jnp.concatenate([row[:, None] * inv, col[:, None] * inv], axis=-1)
    return (jnp.cos(ang), jnp.sin(ang))


def apply_rope(x, cos, sin):
    half = x.shape[-1] // 2
    shape = (cos.shape[0],) + (1,) * (x.ndim - 3) + (half,)
    cos = cos.reshape(shape).astype(x.dtype)
    sin = sin.reshape(shape).astype(x.dtype)
    x1, x2 = x[..., :half], x[..., half:]
    return jnp.concatenate([x1 * cos - x2 * sin, x1 * sin + x2 * cos], axis=-1)


def softmax_sink(s, sink):
    m = jnp.maximum(jnp.max(s, axis=-1, keepdims=True), sink)
    e = jnp.exp(s - m)
    return e / (jnp.sum(e, axis=-1, keepdims=True) + jnp.exp(sink - m))


def sweep_query_blocks(attend, q):
    B, S = q.shape[:2]
    nb = S // Q_BLOCK
    qb = jnp.moveaxis(q.reshape((B, nb, Q_BLOCK) + q.shape[2:]), 1, 0)
    o = lax.map(attend, qb)
    return jnp.moveaxis(o, 0, 1).reshape((B, S) + o.shape[3:])


def gqa_attend(q, k, v, sink=None):
    B, Q, H, d = q.shape
    Hkv = k.shape[2]
    G = H // Hkv
    s = jnp.einsum('bqkgd,btkd->bkgqt', q.reshape(B, Q, Hkv, G, d), k).astype(jnp.float32) * (d ** -0.5)
    if sink is None:
        p = jax.nn.softmax(s, axis=-1)
    else:
        p = softmax_sink(s, sink.astype(jnp.float32).reshape(Hkv, G, 1, 1))
    o = jnp.einsum('bkgqt,btkd->bqkgd', p.astype(v.dtype), v)
    return o.reshape(B, Q, H, v.shape[-1])


def diff_lambda(lq1, lk1, lq2, lk2, lam_init):
    f = lambda a: a.astype(jnp.float32)
    return jnp.exp(jnp.sum(f(lq1) * f(lk1))) - jnp.exp(jnp.sum(f(lq2) * f(lk2))) + lam_init


def diff_attend(q, k, v, lam):
    s = jnp.einsum('bqhcd,bthcd->bchqt', q, k).astype(jnp.float32) * (q.shape[-1] ** -0.5)
    p = jax.nn.softmax(s, axis=-1)
    a = (p[:, 0] - lam * p[:, 1]).astype(v.dtype)
    return jnp.einsum('bhqt,bthd->bqhd', a, v)


def diff_out(o, subln_g, lam_init):
    return rms_norm(o, subln_g) * (1 - lam_init)


def window_attend(q, k, v, kc, vc, sink):
    B, S, H, d = q.shape
    Hkv = k.shape[2]
    G = H // Hkv
    nb = S // Q_BLOCK
    pad = ((0, 0), (WINDOW, WINDOW), (0, 0), (0, 0))
    idx = jnp.arange(nb)[:, None] + jnp.arange(N_BAND)[None, :]
    J = N_BAND * Q_BLOCK
    kb = jnp.pad(k, pad).reshape(B, nb + N_BAND - 1, Q_BLOCK, Hkv, d)[:, idx].reshape(B, nb, J, Hkv, d)
    vb = jnp.pad(v, pad).reshape(B, nb + N_BAND - 1, Q_BLOCK, Hkv, v.shape[-1])[:, idx].reshape(B, nb, J, Hkv, v.shape[-1])
    qb = q.reshape(B, nb, Q_BLOCK, Hkv, G, d)
    scale = d ** -0.5
    s_loc = jnp.einsum('bnqkgd,bnjkd->bnkgqj', qb, kb).astype(jnp.float32) * scale
    qpos = jnp.arange(nb)[:, None] * Q_BLOCK + jnp.arange(Q_BLOCK)[None, :]
    kpos = jnp.arange(nb)[:, None] * Q_BLOCK - WINDOW + jnp.arange(J)[None, :]
    valid = (jnp.abs(kpos[:, None, :] - qpos[:, :, None]) <= WINDOW) & (kpos[:, None, :] >= 0) & (kpos[:, None, :] < S)
    s_loc = jnp.where(valid[None, :, None, None], s_loc, NEG_INF)
    s_ctx = jnp.einsum('bnqkgd,bckd->bnkgqc', qb, kc).astype(jnp.float32) * scale
    p = softmax_sink(jnp.concatenate([s_loc, s_ctx], axis=-1), sink.astype(jnp.float32).reshape(1, 1, Hkv, G, 1, 1))
    p = p.astype(v.dtype)
    o = jnp.einsum('bnkgqj,bnjkd->bnqkgd', p[..., :J], vb) + jnp.einsum('bnkgqc,bckd->bnqkgd', p[..., J:], vc)
    return o.reshape(B, S, H, v.shape[-1])


def neighbourhood_attend(q, k, v, kc, vc, rpb):
    B, S, H, d = q.shape
    rows = S // GRID_W
    kh = min(NA_ROWS_MAX, rows)
    J = kh * GRID_W
    qg = q.reshape(B, rows, GRID_W, H, d)
    r = jnp.arange(rows)
    r0 = jnp.clip(r - kh // 2, 0, rows - kh)
    ridx = r0[:, None] + jnp.arange(kh)[None, :]
    kn = k.reshape(B, rows, GRID_W, H, d)[:, ridx].reshape(B, rows, J, H, d)
    vn = v.reshape(B, rows, GRID_W, H, v.shape[-1])[:, ridx].reshape(B, rows, J, H, v.shape[-1])
    scale = d ** -0.5
    s_loc = jnp.einsum('brqhd,brjhd->brhqj', qg, kn).astype(jnp.float32) * scale
    col = jnp.arange(GRID_W)
    c0 = jnp.clip(col - NA_COLS // 2, 0, GRID_W - NA_COLS)
    col_ok = (col[None, :] >= c0[:, None]) & (col[None, :] < c0[:, None] + NA_COLS)
    mask = jnp.broadcast_to(col_ok[:, None, :], (GRID_W, kh, GRID_W)).reshape(GRID_W, J)
    dr = ridx - r[:, None] + (NA_ROWS_MAX - 1)
    dc = jnp.clip(col[None, :] - col[:, None], -(NA_COLS - 1), NA_COLS - 1) + (NA_COLS - 1)
    bias = rpb[:, dr[:, None, :, None], dc[None, :, None, :]]
    bias = bias.transpose(1, 0, 2, 3, 4).reshape(rows, H, GRID_W, J).astype(jnp.float32)
    s_loc = jnp.where(mask, s_loc + bias, NEG_INF)
    s_ctx = jnp.einsum('brqhd,bchd->brhqc', qg, kc).astype(jnp.float32) * scale
    p = jax.nn.softmax(jnp.concatenate([s_loc, s_ctx], axis=-1), axis=-1).astype(v.dtype)
    o = jnp.einsum('brhqj,brjhd->brqhd', p[..., :J], vn) + jnp.einsum('brhqc,bchd->brqhd', p[..., J:], vc)
    return o.reshape(B, S, H, v.shape[-1])


def merge_branches(outs, g, b_gate, w_branch, w_out):
    lead = outs[0].shape[:2]
    br = jnp.stack([o.reshape(lead + (BRANCH_W,)) for o in outs], axis=2)
    proj = jnp.einsum('btnc,ncd->btnd', br, w_branch)
    gate = jax.nn.sigmoid(g + b_gate).reshape(lead + (N_BRANCH, D_MODEL))
    return jnp.sum(gate * proj, axis=2) @ w_out


def token_mixer(h, hc, need_ctx, lam_init, rope, w_in, b_gate, a_qn, a_kn, b_qn, b_kn,
                lam_q1, lam_k1, lam_q2, lam_k2, subln_g, c_qn, c_kn, sink, d_qn, d_kn,
                rpb, w_branch, w_out):
    cos64, sin64, cos32, sin32 = rope
    B, S, _ = h.shape
    C = hc.shape[1]
    q, kv, g = split_cols(h @ w_in, (Q_TOT, KV_TOT, GATE_TOT))
    q_g, q_d, q_w, q_n = split_cols(q, Q_SIZES)
    k_g, k_d, k_w, k_n, v_g, v_d, v_w, v_n = split_cols(kv, K_SIZES + V_SIZES)
    ck_g, ck_d, ck_w, ck_n, cv_g, cv_d, cv_w, cv_n = split_cols(hc @ w_in[:, Q_TOT:Q_TOT + KV_TOT], K_SIZES + V_SIZES)

    q_g = apply_rope(rms_norm(heads(q_g, A_HEADS), a_qn), cos64, sin64)
    k_g = apply_rope(rms_norm(heads(k_g, A_KV_HEADS), a_kn), cos64, sin64)
    ck_g = rms_norm(heads(ck_g, A_KV_HEADS), a_kn)
    cv_g = heads(cv_g, A_KV_HEADS)
    kk_g = jnp.concatenate([k_g, ck_g], axis=1)
    vv_g = jnp.concatenate([heads(v_g, A_KV_HEADS), cv_g], axis=1)
    o_g = sweep_query_blocks(lambda qb: gqa_attend(qb, kk_g, vv_g), q_g)

    lam = diff_lambda(lam_q1, lam_k1, lam_q2, lam_k2, lam_init)
    q_d = apply_rope(rms_norm(q_d.reshape(B, S, B_HEADS, 2, B_DK), b_qn), cos32, sin32)
    k_d = apply_rope(rms_norm(k_d.reshape(B, S, B_HEADS, 2, B_DK), b_kn), cos32, sin32)
    ck_d = rms_norm(ck_d.reshape(B, C, B_HEADS, 2, B_DK), b_kn)
    cv_d = heads(cv_d, B_HEADS)
    kk_d = jnp.concatenate([k_d, ck_d], axis=1)
    vv_d = jnp.concatenate([heads(v_d, B_HEADS), cv_d], axis=1)
    o_d = diff_out(sweep_query_blocks(lambda qb: diff_attend(qb, kk_d, vv_d, lam), q_d), subln_g, lam_init)

    q_w = apply_rope(rms_norm(heads(q_w, C_HEADS), c_qn), cos64, sin64)
    k_w = apply_rope(rms_norm(heads(k_w, C_KV_HEADS), c_kn), cos64, sin64)
    ck_w = rms_norm(heads(ck_w, C_KV_HEADS), c_kn)
    cv_w = heads(cv_w, C_KV_HEADS)
    o_w = window_attend(q_w, k_w, heads(v_w, C_KV_HEADS), ck_w, cv_w, sink)

    q_n = rms_norm(heads(q_n, D_HEADS), d_qn)
    k_n = rms_norm(heads(k_n, D_HEADS), d_kn)
    ck_n = rms_norm(heads(ck_n, D_HEADS), d_kn)
    cv_n = heads(cv_n, D_HEADS)
    o_n = neighbourhood_attend(q_n, k_n, heads(v_n, D_HEADS), ck_n, cv_n, rpb)

    out = merge_branches((o_g, o_d, o_w, o_n), g, b_gate, w_branch, w_out)
    if not need_ctx:
        return out, None
    cq_g, cq_d, cq_w, cq_n = split_cols(hc @ w_in[:, :Q_TOT], Q_SIZES)
    co_g = gqa_attend(rms_norm(heads(cq_g, A_HEADS), a_qn), ck_g, cv_g)
    co_d = diff_out(diff_attend(rms_norm(cq_d.reshape(B, C, B_HEADS, 2, B_DK), b_qn), ck_d, cv_d, lam), subln_g, lam_init)
    co_w = gqa_attend(rms_norm(heads(cq_w, C_HEADS), c_qn), ck_w, cv_w, sink)
    co_n = gqa_attend(rms_norm(heads(cq_n, D_HEADS), d_qn), ck_n, cv_n)
    out_c = merge_branches((co_g, co_d, co_w, co_n), hc @ w_in[:, Q_TOT + KV_TOT:], b_gate, w_branch, w_out)
    return out, out_c


def moe_ffn(h, router_w, router_b, w_gate_up, b_gate_up, w_down, b_down):
    T, D = h.shape
    logits = (h @ router_w).astype(jnp.float32) + router_b.astype(jnp.float32)
    top_v, top_e = lax.top_k(logits, TOP_K)
    top_w = jax.nn.softmax(top_v, axis=-1).astype(h.dtype)
    n = T * TOP_K
    flat_e = top_e.reshape(-1)
    order = jnp.argsort(flat_e, stable=True)
    e_sorted = flat_e[order]
    tok = (order // TOP_K).astype(jnp.int32)
    counts = jnp.bincount(flat_e, length=N_EXPERTS)
    padded = (counts + EXPERT_BLOCK - 1) // EXPERT_BLOCK * EXPERT_BLOCK
    pad_end = jnp.cumsum(padded)
    pad_start = pad_end - padded
    start = jnp.cumsum(counts) - counts
    dest = pad_start[e_sorted] + jnp.arange(n) - start[e_sorted]
    P = n + N_EXPERTS * EXPERT_BLOCK
    n_blk = P // EXPERT_BLOCK
    slot_tok = jnp.full((P,), T, jnp.int32).at[dest].set(tok)
    hp = jnp.concatenate([h, jnp.zeros((1, D), h.dtype)], axis=0)
    xbuf = hp[slot_tok].reshape(n_blk, EXPERT_BLOCK, D)
    blk_e = jnp.minimum(jnp.searchsorted(pad_end, jnp.arange(n_blk) * EXPERT_BLOCK, side='right'), N_EXPERTS - 1)

    def expert_block(args):
        xb, e = args
        gu = xb @ w_gate_up[e] + b_gate_up[e]
        gate = jnp.minimum(gu[:, :D_FF], SWIGLU_LIMIT)
        up = jnp.clip(gu[:, D_FF:], -SWIGLU_LIMIT, SWIGLU_LIMIT)
        return (gate * jax.nn.sigmoid(SWIGLU_ALPHA * gate) * (up + 1)) @ w_down[e] + b_down[e]

    y = lax.map(expert_block, (xbuf, blk_e)).reshape(P, D)
    contrib = y[dest] * top_w.reshape(-1)[order][:, None]
    return jax.ops.segment_sum(contrib, tok, num_segments=T)


def _normal(k, shape, scale):
    return jax.random.normal(k, shape, jnp.float32) * scale


def _gain(k, shape):
    return 1.0 + 0.05 * jax.random.normal(k, shape, jnp.float32)


def setup_inputs(seed: int = 0) -> dict:
    key = jax.random.key(seed)
    ks = jax.random.split(key, 33)
    L, D, E = DEPTH, D_MODEL, N_EXPERTS
    return {
        'x': _normal(ks[0], (BATCH, SEQ, D), 1.0),
        'c': _normal(ks[1], (BATCH, D), 1.0),
        'ctx': _normal(ks[2], (BATCH, CTX_LEN, D), 1.0),
        'c_ctx': _normal(ks[3], (D,), 1.0),
        'norm1_g': _gain(ks[4], (L, D)),
        'norm2_g': _gain(ks[5], (L, D)),
        'w_ada': _normal(ks[6], (L, D, N_MOD * D), 0.5 * D ** -0.5),
        'b_ada': _normal(ks[7], (L, N_MOD * D), 0.1),
        'w_in': _normal(ks[8], (L, D, W_IN_COLS), D ** -0.5),
        'b_gate': _normal(ks[9], (L, GATE_TOT), 0.1),
        'a_qn': _gain(ks[10], (L, HEAD_DIM)),
        'a_kn': _gain(ks[11], (L, HEAD_DIM)),
        'b_qn': _gain(ks[12], (L, B_DK)),
        'b_kn': _gain(ks[13], (L, B_DK)),
        'lam_q1': _normal(ks[14], (L, B_DK), 0.1),
        'lam_k1': _normal(ks[15], (L, B_DK), 0.1),
        'lam_q2': _normal(ks[16], (L, B_DK), 0.1),
        'lam_k2': _normal(ks[17], (L, B_DK), 0.1),
        'subln_g': _gain(ks[18], (L, B_DV)),
        'c_qn': _gain(ks[19], (L, HEAD_DIM)),
        'c_kn': _gain(ks[20], (L, HEAD_DIM)),
        'sink': _normal(ks[21], (L, C_HEADS), 0.5),
        'd_qn': _gain(ks[22], (L, HEAD_DIM)),
        'd_kn': _gain(ks[23], (L, HEAD_DIM)),
        'rpb': _normal(ks[24], (L, D_HEADS, 2 * NA_ROWS_MAX - 1, 2 * NA_COLS - 1), 0.1),
        'w_branch': _normal(ks[25], (L, N_BRANCH, BRANCH_W, D), BRANCH_W ** -0.5),
        'w_out': _normal(ks[26], (L, D, D), D ** -0.5),
        'router_w': _normal(ks[27], (L, D, E), D ** -0.5),
        'router_b': _normal(ks[28], (L, E), 0.01),
        'w_gate_up': _normal(ks[29], (L, E, D, 2 * D_FF), D ** -0.5),
        'b_gate_up': _normal(ks[30], (L, E, 2 * D_FF), 0.02),
        'w_down': _normal(ks[31], (L, E, D_FF, D), D_FF ** -0.5),
        'b_down': _normal(ks[32], (L, E, D), 0.02),
    }


def reference(x, c, ctx, c_ctx, norm1_g, norm2_g, w_ada, b_ada, w_in, b_gate,
              a_qn, a_kn, b_qn, b_kn, lam_q1, lam_k1, lam_q2, lam_k2, subln_g,
              c_qn, c_kn, sink, d_qn, d_kn, rpb, w_branch, w_out,
              router_w, router_b, w_gate_up, b_gate_up, w_down, b_down):
    B, S, D = x.shape
    rope = rope_2d(S, HEAD_DIM) + rope_2d(S, B_DK)
    for l in range(DEPTH):
        last = l == DEPTH - 1
        lam_init = 0.8 - 0.6 * math.exp(-0.3 * l)
        mod = (jax.nn.silu(c) @ w_ada[l] + b_ada[l])[:, None, :]
        sh1, sc1, g1, sh2, sc2, g2 = jnp.split(mod, N_MOD, axis=-1)
        cmod = jax.nn.silu(c_ctx) @ w_ada[l] + b_ada[l]
        csh1, csc1, cg1, csh2, csc2, cg2 = jnp.split(cmod, N_MOD, axis=-1)
        h = modulate(rms_norm(x, norm1_g[l]), sh1, sc1)
        hc = modulate(rms_norm(ctx, norm1_g[l]), csh1, csc1)
        mix, mix_c = token_mixer(h, hc, not last, lam_init, rope, w_in[l], b_gate[l], a_qn[l], a_kn[l],
                                 b_qn[l], b_kn[l], lam_q1[l], lam_k1[l], lam_q2[l], lam_k2[l], subln_g[l],
                                 c_qn[l], c_kn[l], sink[l], d_qn[l], d_kn[l], rpb[l], w_branch[l], w_out[l])
        x = x + g1 * mix
        h = modulate(rms_norm(x, norm2_g[l]), sh2, sc2)
        expert_args = (router_w[l], router_b[l], w_gate_up[l], b_gate_up[l], w_down[l], b_down[l])
        if last:
            x = x + g2 * moe_ffn(h.reshape(-1, D), *expert_args).reshape(B, S, D)
        else:
            ctx = ctx + cg1 * mix_c
            hc = modulate(rms_norm(ctx, norm2_g[l]), csh2, csc2)
            y = moe_ffn(jnp.concatenate([h.reshape(-1, D), hc.reshape(-1, D)], axis=0), *expert_args)
            x = x + g2 * y[:B * S].reshape(B, S, D)
            ctx = ctx + cg2 * y[B * S:].reshape(ctx.shape)
    return x
```

```python
import functools
import math

import numpy as np
import jax
import jax.numpy as jnp
from jax import lax
from jax.experimental import pallas as pl
from jax.experimental.pallas import tpu as pltpu

F32 = jnp.float32
BF16 = jnp.bfloat16

D_MODEL = 1024
GRID_W = 64
HEAD_DIM = 64
B_DK = 32
WINDOW = 128
NA_ROWS = 8
NA_COLS = 16
ROPE_THETA = 10000.0
N_EXPERTS = 32
TOP_K = 4
D_FF = D_MODEL
SWIGLU_LIMIT = 7.0
SWIGLU_ALPHA = 1.702
N_MOD = 6
EPS = 1e-6
NEG_INF = -1e30

Q_TOT = 1024
KV_TOT = 1536
QKV_TOT = Q_TOT + KV_TOT
GATE_TOT = 4 * D_MODEL

ROW_TILE = 256
LANES = 128
EXPERT_ROWS = 256
NBR_WIN_ROWS = 10
NBR_TABLE_ROWS = NBR_WIN_ROWS + 8
VMEM_LIMIT = 52 * 1024 * 1024


def _cparams(sem):
    return pltpu.CompilerParams(dimension_semantics=sem, vmem_limit_bytes=VMEM_LIMIT)


def _dot(a, b):
    return jnp.dot(a, b, preferred_element_type=F32)


def _ada_kernel(c_ref, w_ref, b_ref, o_ref):
    c = c_ref[...]
    s = c * jax.nn.sigmoid(c)
    o_ref[0] = jnp.dot(s, w_ref[0], precision=lax.Precision.HIGHEST,
                       preferred_element_type=F32) + b_ref[0]


def _ada(cvec, w_ada, b_ada):
    L = w_ada.shape[0]
    n_out = w_ada.shape[2]
    tn = 1536
    return pl.pallas_call(
        _ada_kernel,
        grid=(L, n_out // tn),
        in_specs=[
            pl.BlockSpec((8, D_MODEL), lambda l, j: (0, 0)),
            pl.BlockSpec((1, D_MODEL, tn), lambda l, j: (l, 0, j)),
            pl.BlockSpec((1, 1, tn), lambda l, j: (l, 0, j)),
        ],
        out_specs=pl.BlockSpec((1, 8, tn), lambda l, j: (l, 0, j)),
        out_shape=jax.ShapeDtypeStruct((L, 8, n_out), F32),
        compiler_params=_cparams(("arbitrary", "arbitrary")),
        name="ada_mod",
    )(cvec, w_ada, b_ada.reshape(L, 1, n_out))


def _in_kernel(x_ref, g_ref, mod_ref, w_ref, bg_ref, qkv_ref, gate_ref, *, tn):
    x = x_ref[...]
    ms = jnp.mean(x * x, axis=-1, keepdims=True)
    y = x * lax.rsqrt(ms + EPS) * g_ref[...]
    mod = mod_ref[0]
    sh = mod[:, 0:D_MODEL]
    sc = mod[:, D_MODEL:2 * D_MODEL]
    h = (y * (1 + sc) + sh).astype(BF16)
    for j in range(QKV_TOT // tn):
        qkv_ref[:, j * tn:(j + 1) * tn] = _dot(h, w_ref[:, j * tn:(j + 1) * tn]).astype(BF16)
    for j in range(GATE_TOT // tn):
        g = _dot(h, w_ref[:, QKV_TOT + j * tn:QKV_TOT + (j + 1) * tn]) + bg_ref[:, j * tn:(j + 1) * tn]
        gate_ref[:, j * tn:(j + 1) * tn] = jax.nn.sigmoid(g).astype(BF16)


def _in_proj(X, norm_g, mod, w_in_bf, b_gate, row_batch):
    R = X.shape[0]
    nt = R // ROW_TILE
    return pl.pallas_call(
        functools.partial(_in_kernel, tn=512),
        grid=(nt,),
        in_specs=[
            pl.BlockSpec((ROW_TILE, D_MODEL), lambda t: (t, 0)),
            pl.BlockSpec((1, D_MODEL), lambda t: (0, 0)),
            pl.BlockSpec((1, 1, N_MOD * D_MODEL), lambda t: (row_batch(t), 0, 0)),
            pl.BlockSpec((D_MODEL, QKV_TOT + GATE_TOT), lambda t: (0, 0), pipeline_mode=pl.Buffered(1)),
            pl.BlockSpec((1, GATE_TOT), lambda t: (0, 0)),
        ],
        out_specs=[
            pl.BlockSpec((ROW_TILE, QKV_TOT), lambda t: (t, 0)),
            pl.BlockSpec((ROW_TILE, GATE_TOT), lambda t: (t, 0)),
        ],
        out_shape=[
            jax.ShapeDtypeStruct((R, QKV_TOT), BF16),
            jax.ShapeDtypeStruct((R, GATE_TOT), BF16),
        ],
        compiler_params=_cparams(("arbitrary",)),
        name="in_proj",
    )(X, norm_g.reshape(1, D_MODEL), mod, w_in_bf, b_gate.reshape(1, GATE_TOT))


def _seg_rms(x, ones, seg, g):
    sq = x * x
    hi = sq.astype(BF16)
    lo = (sq - hi.astype(F32)).astype(BF16)
    ss = _dot(hi, ones) + _dot(lo, ones)
    return x * lax.rsqrt(ss * (1.0 / seg) + EPS) * g


def _rot_half(y, half):
    lane = lax.broadcasted_iota(jnp.int32, y.shape, 1)
    fwd = pltpu.roll(y, LANES - half, axis=1)
    bwd = pltpu.roll(y, half, axis=1)
    return jnp.where((lane % (2 * half)) < half, fwd, bwd)


def _qkv_post_kernel(qkv_ref, c64_ref, s64_ref, c32_ref, s32_ref, ones64_ref, ones32_ref, gains_ref,
                     qa_ref, qb_ref, qc_ref, qd_ref, ka_ref, kb_ref, kc_ref, kd_ref,
                     va_ref, vb_ref, vc_ref, vd_ref):
    gains = gains_ref[...]
    rope = {64: (c64_ref[...], s64_ref[...]), 32: (c32_ref[...], s32_ref[...])}
    ones = {64: ones64_ref[...], 32: ones32_ref[...]}

    def chunk(col):
        return qkv_ref[:, col:col + LANES].astype(F32)

    def normed(col, seg, gain_row, use_rope, scale):
        y = _seg_rms(chunk(col), ones[seg], seg, gains[gain_row:gain_row + 1, :])
        if use_rope:
            cos, sin = rope[seg]
            y = y * cos + _rot_half(y, seg // 2) * sin
        return y * scale if scale != 1.0 else y

    q_refs = (qa_ref, qb_ref, qc_ref, qd_ref)
    q_seg = (64, 32, 64, 64)
    q_rope = (True, True, True, False)
    for m in range(4):
        scale = float(q_seg[m]) ** -0.5
        for c in range(2):
            y = normed(m * 256 + c * LANES, q_seg[m], 2 * m, q_rope[m], scale)
            q_refs[m][0, c * LANES:(c + 1) * LANES, :] = y.T.astype(BF16)

    k_refs = (ka_ref, kb_ref, kc_ref, kd_ref)
    v_refs = (va_ref, vb_ref, vc_ref, vd_ref)
    widths = (128, 256, 128, 256)
    kcol = Q_TOT
    vcol = Q_TOT + sum(widths)
    for m in range(4):
        for c in range(widths[m] // LANES):
            y = normed(kcol, q_seg[m], 2 * m + 1, q_rope[m], 1.0)
            k_refs[m][0, :, c * LANES:(c + 1) * LANES] = y.astype(BF16)
            v_refs[m][0, c * LANES:(c + 1) * LANES, :] = chunk(vcol).T.astype(BF16)
            kcol += LANES
            vcol += LANES


def _qkv_post(qkv, tables, ones64, ones32, gains, B, S, C):
    R = qkv.shape[0]
    nt = R // ROW_TILE
    per = S // ROW_TILE
    nlat = B * per
    T = S + C

    def bidx(t):
        return jnp.where(t < nlat, t // per, t - nlat)

    def pidx(t):
        return jnp.where(t < nlat, t % per, per)

    tab_spec = pl.BlockSpec((ROW_TILE, LANES), lambda t: (pidx(t), 0))
    const_spec = pl.BlockSpec((LANES, LANES), lambda t: (0, 0))
    widths = (128, 256, 128, 256)
    q_specs = [pl.BlockSpec((1, 256, ROW_TILE), lambda t: (bidx(t), 0, pidx(t))) for _ in range(4)]
    k_specs = [pl.BlockSpec((1, ROW_TILE, w), lambda t: (bidx(t), pidx(t), 0)) for w in widths]
    v_specs = [pl.BlockSpec((1, w, ROW_TILE), lambda t: (bidx(t), 0, pidx(t))) for w in widths]
    q_shapes = [jax.ShapeDtypeStruct((B, 256, T), BF16) for _ in range(4)]
    k_shapes = [jax.ShapeDtypeStruct((B, T, w), BF16) for w in widths]
    v_shapes = [jax.ShapeDtypeStruct((B, w, T), BF16) for w in widths]
    outs = pl.pallas_call(
        _qkv_post_kernel,
        grid=(nt,),
        in_specs=[pl.BlockSpec((ROW_TILE, QKV_TOT), lambda t: (t, 0)),
                  tab_spec, tab_spec, tab_spec, tab_spec, const_spec, const_spec,
                  pl.BlockSpec((8, LANES), lambda t: (0, 0))],
        out_specs=q_specs + k_specs + v_specs,
        out_shape=q_shapes + k_shapes + v_shapes,
        compiler_params=_cparams(("arbitrary",)),
        name="qkv_post",
    )(qkv, *tables, ones64, ones32, gains)
    return outs[0:4], outs[4:8], outs[8:12]


def _pad_queries(q, pieces, kw, unit):
    tq = q.shape[1]
    r = lax.broadcasted_iota(jnp.int32, (kw, tq), 0)
    blocks = []
    for row0, size, extra in pieces:
        tiled = jnp.concatenate([q[row0:row0 + size, :]] * (kw // size), axis=0)
        off = unit * HEAD_DIM + extra
        blocks.append(jnp.where((r >= off) & (r < off + size), tiled, 0.0))
    out = blocks[0] if len(blocks) == 1 else jnp.concatenate(blocks, axis=1)
    return out.astype(BF16)


_PIECES = {
    "gqa": ((0, 64, 0), (64, 64, 0)),
    "diff": ((0, 32, 0), (32, 32, 32)),
    "mha": ((0, 64, 0),),
}


def _flash_kernel(*refs, mode, kw, tq, tk, kv_start, n_chunks, use_sink, lam_init, aliased):
    refs = list(refs)
    qT_ref, k_ref, vT_ref = refs[0:3]
    pos = 3
    sink_ref = None
    if use_sink:
        sink_ref = refs[pos]
        pos += 1
    if mode == "diff":
        lamvec_ref, subg_ref = refs[pos:pos + 2]
        pos += 2
    if aliased:
        pos += 1
    o_ref = refs[pos]

    unit = pl.program_id(1)
    pieces = _PIECES[mode]
    ng = len(pieces)
    n = ng * tq
    qpad = _pad_queries(qT_ref[0].astype(F32), pieces, kw, unit)

    if use_sink:
        m0 = jnp.concatenate([jnp.full((1, tq), sink_ref[ng * unit + g], F32) for g in range(ng)], axis=1)
        l0 = jnp.ones((1, n), F32)
    else:
        m0 = jnp.full((1, n), NEG_INF, F32)
        l0 = jnp.zeros((1, n), F32)
    acc0 = jnp.zeros((HEAD_DIM, n), F32)

    def body(j, carry):
        m, l, acc = carry
        start = pl.multiple_of(kv_start + j * tk, LANES)
        s = _dot(k_ref[0, pl.ds(start, tk), :], qpad)
        m_new = jnp.maximum(m, jnp.max(s, axis=0, keepdims=True))
        alpha = jnp.exp(m - m_new)
        p = jnp.exp(s - m_new)
        l_new = alpha * l + jnp.sum(p, axis=0, keepdims=True)
        acc_new = alpha * acc + _dot(vT_ref[0, :, pl.ds(start, tk)], p.astype(BF16))
        return m_new, l_new, acc_new

    if n_chunks == 1:
        m, l, acc = body(0, (m0, l0, acc0))
    else:
        m, l, acc = lax.fori_loop(0, n_chunks, body, (m0, l0, acc0))
    o = acc / l

    if mode == "gqa":
        for g in range(ng):
            o_ref[0, g * HEAD_DIM:(g + 1) * HEAD_DIM, :] = o[:, g * tq:(g + 1) * tq].astype(o_ref.dtype)
    elif mode == "mha":
        o_ref[0] = o.astype(o_ref.dtype)
    else:
        lv = lamvec_ref[...]
        lam = (jnp.exp(jnp.sum(lv[0:1] * lv[1:2], axis=1, keepdims=True))
               - jnp.exp(jnp.sum(lv[2:3] * lv[3:4], axis=1, keepdims=True)) + lam_init)
        d = o[:, 0:tq] - lam * o[:, tq:2 * tq]
        ms = jnp.mean(d * d, axis=0, keepdims=True)
        o_ref[0] = (d * lax.rsqrt(ms + EPS) * subg_ref[...] * (1.0 - lam_init)).astype(o_ref.dtype)


def _flash(qT, k, vT, *, mode, S, C, ctx_only, sink=None, lamvec=None, subg=None, lam_init=0.0, out_prev=None):
    B, _, T = qT.shape
    kw = k.shape[2]
    q_rows = 128 if mode == "gqa" else 64
    units = 256 // q_rows
    if ctx_only:
        tq, nq, q_blk0 = C, 1, S // C
        tk, kv_start, n_chunks = C, S, 1
    else:
        tq, nq, q_blk0 = 256, S // 256, 0
        tk = 768 if T % 768 == 0 else 256
        kv_start, n_chunks = 0, T // tk
    in_specs = [
        pl.BlockSpec((1, q_rows, tq), lambda b, u, i: (b, u, i + q_blk0)),
        pl.BlockSpec((1, T, kw), lambda b, u, i: (b, 0, 0)),
        pl.BlockSpec((1, HEAD_DIM, T), lambda b, u, i: (b, u, 0)),
    ]
    args = [qT, k, vT]
    if sink is not None:
        in_specs.append(pl.BlockSpec(memory_space=pltpu.SMEM))
        args.append(sink)
    if mode == "diff":
        in_specs += [pl.BlockSpec((8, LANES), lambda b, u, i: (0, 0)),
                     pl.BlockSpec((HEAD_DIM, 1), lambda b, u, i: (0, 0))]
        args += [lamvec, subg]
    aliases = {}
    if out_prev is not None:
        in_specs.append(pl.BlockSpec(memory_space=pl.ANY))
        aliases = {len(args): 0}
        args.append(out_prev)
    kern = functools.partial(_flash_kernel, mode=mode, kw=kw, tq=tq, tk=tk, kv_start=kv_start,
                             n_chunks=n_chunks, use_sink=sink is not None, lam_init=lam_init,
                             aliased=out_prev is not None)
    return pl.pallas_call(
        kern,
        grid=(B, units, nq),
        in_specs=in_specs,
        out_specs=pl.BlockSpec((1, q_rows, tq), lambda b, u, i: (b, u, i + q_blk0)),
        out_shape=jax.ShapeDtypeStruct((B, 256, T), BF16),
        input_output_aliases=aliases,
        compiler_params=_cparams(("arbitrary", "arbitrary", "arbitrary")),
        name="flash_" + mode + ("_ctx" if ctx_only else ""),
    )(*args)


def _window_kernel(qT_ref, k_ref, vT_ref, sink_ref, o_ref, *, S, C):
    unit = pl.program_id(1)
    i = pl.program_id(2)
    qb = WINDOW
    span = 3 * qb
    n = 2 * qb
    qpad = _pad_queries(qT_ref[0].astype(F32), _PIECES["gqa"], LANES, unit)
    start = pl.multiple_of(jnp.clip((i - 1) * qb, 0, S - span), LANES)
    s_loc = _dot(k_ref[0, pl.ds(start, span), :], qpad)
    s_ctx = _dot(k_ref[0, S:S + C, :], qpad)
    kpos = start + lax.broadcasted_iota(jnp.int32, (span, n), 0)
    qpos = i * qb + lax.broadcasted_iota(jnp.int32, (span, n), 1) % qb
    s_loc = jnp.where(jnp.abs(kpos - qpos) <= WINDOW, s_loc, NEG_INF)
    sink = jnp.concatenate([jnp.full((1, qb), sink_ref[2 * unit + g], F32) for g in range(2)], axis=1)
    m = jnp.maximum(jnp.maximum(jnp.max(s_loc, axis=0, keepdims=True),
                                jnp.max(s_ctx, axis=0, keepdims=True)), sink)
    e_loc = jnp.exp(s_loc - m)
    e_ctx = jnp.exp(s_ctx - m)
    den = (jnp.sum(e_loc, axis=0, keepdims=True) + jnp.sum(e_ctx, axis=0, keepdims=True)
           + jnp.exp(sink - m))
    o = (_dot(vT_ref[0, :, pl.ds(start, span)], e_loc.astype(BF16))
         + _dot(vT_ref[0, :, S:S + C], e_ctx.astype(BF16))) / den
    for g in range(2):
        o_ref[0, g * HEAD_DIM:(g + 1) * HEAD_DIM, :] = o[:, g * qb:(g + 1) * qb].astype(o_ref.dtype)


def _window(qT, k, vT, sink, S, C):
    B, _, T = qT.shape
    return pl.pallas_call(
        functools.partial(_window_kernel, S=S, C=C),
        grid=(B, 2, S // WINDOW),
        in_specs=[
            pl.BlockSpec((1, 128, WINDOW), lambda b, u, i: (b, u, i)),
            pl.BlockSpec((1, T, LANES), lambda b, u, i: (b, 0, 0)),
            pl.BlockSpec((1, HEAD_DIM, T), lambda b, u, i: (b, u, 0)),
            pl.BlockSpec(memory_space=pltpu.SMEM),
        ],
        out_specs=pl.BlockSpec((1, 128, WINDOW), lambda b, u, i: (b, u, i)),
        out_shape=jax.ShapeDtypeStruct((B, 256, T), BF16),
        compiler_params=_cparams(("arbitrary", "arbitrary", "arbitrary")),
        name="window_attn",
    )(qT, k, vT, sink)


def _nbr_kernel(qT_ref, k_ref, vT_ref, tab_ref, o_ref, *, S, C):
    unit = pl.program_id(1)
    j = pl.program_id(2)
    rows = S // GRID_W
    n = 2 * GRID_W
    span = NBR_WIN_ROWS * GRID_W
    qpad = _pad_queries(qT_ref[0].astype(F32), _PIECES["mha"], 256, unit)
    w0 = jnp.clip(2 * j - NA_ROWS // 2, 0, rows - NBR_WIN_ROWS)
    shift = w0 - 2 * j + NA_ROWS // 2 + 4
    start = pl.multiple_of(w0 * GRID_W, LANES)
    s_loc = _dot(k_ref[0, pl.ds(start, span), :], qpad)
    s_ctx = _dot(k_ref[0, S:S + C, :], qpad)
    bias = tab_ref[0, pl.ds(pl.multiple_of(shift * GRID_W, GRID_W), span), :]
    kr = w0 + lax.broadcasted_iota(jnp.int32, (span, n), 0) // GRID_W
    qr = 2 * j + lax.broadcasted_iota(jnp.int32, (span, n), 1) // GRID_W
    r0 = jnp.clip(qr - NA_ROWS // 2, 0, rows - NA_ROWS)
    s_loc = jnp.where((kr >= r0) & (kr < r0 + NA_ROWS), s_loc + bias, NEG_INF)
    m = jnp.maximum(jnp.max(s_loc, axis=0, keepdims=True), jnp.max(s_ctx, axis=0, keepdims=True))
    e_loc = jnp.exp(s_loc - m)
    e_ctx = jnp.exp(s_ctx - m)
    den = jnp.sum(e_loc, axis=0, keepdims=True) + jnp.sum(e_ctx, axis=0, keepdims=True)
    o = (_dot(vT_ref[0, :, pl.ds(start, span)], e_loc.astype(BF16))
         + _dot(vT_ref[0, :, S:S + C], e_ctx.astype(BF16))) / den
    o_ref[0] = o.astype(o_ref.dtype)


def _nbr_bias_table(rpb):
    u = np.arange(NBR_TABLE_ROWS)[:, None, None, None]
    kc = np.arange(GRID_W)[None, :, None, None]
    e = np.arange(2)[None, None, :, None]
    qc = np.arange(GRID_W)[None, None, None, :]
    dr = u - e - 1
    row_ok = (dr >= 0) & (dr < 2 * NA_ROWS - 1)
    dc = np.clip(kc - qc, -(NA_COLS - 1), NA_COLS - 1) + (NA_COLS - 1)
    c0 = np.clip(qc - NA_COLS // 2, 0, GRID_W - NA_COLS)
    col_ok = (kc >= c0) & (kc < c0 + NA_COLS)
    shape = (NBR_TABLE_ROWS, GRID_W, 2, GRID_W)
    dr_b = np.broadcast_to(np.clip(dr, 0, 2 * NA_ROWS - 2), shape)
    dc_b = np.broadcast_to(dc, shape)
    vals = rpb[:, dr_b, dc_b].astype(F32)
    vals = jnp.where(np.broadcast_to(row_ok, shape), vals, 0.0)
    vals = jnp.where(np.broadcast_to(col_ok, shape), vals, NEG_INF)
    return vals.reshape(rpb.shape[0], NBR_TABLE_ROWS * GRID_W, 2 * GRID_W)


def _nbr(qT, k, vT, table, S, C):
    B, _, T = qT.shape
    return pl.pallas_call(
        functools.partial(_nbr_kernel, S=S, C=C),
        grid=(B, 4, S // (2 * GRID_W)),
        in_specs=[
            pl.BlockSpec((1, HEAD_DIM, 2 * GRID_W), lambda b, u, j: (b, u, j)),
            pl.BlockSpec((1, T, 256), lambda b, u, j: (b, 0, 0)),
            pl.BlockSpec((1, HEAD_DIM, T), lambda b, u, j: (b, u, 0)),
            pl.BlockSpec((1, NBR_TABLE_ROWS * GRID_W, 2 * GRID_W), lambda b, u, j: (u, 0, 0)),
        ],
        out_specs=pl.BlockSpec((1, HEAD_DIM, 2 * GRID_W), lambda b, u, j: (b, u, j)),
        out_shape=jax.ShapeDtypeStruct((B, 256, T), BF16),
        compiler_params=_cparams(("arbitrary", "arbitrary", "arbitrary")),
        name="nbr_attn",
    )(qT, k, vT, table)


def _merge_kernel(oa_ref, ob_ref, oc_ref, od_ref, gate_ref, x_ref, mod_ref, wb_ref, wo_ref, n2_ref,
                  rw_ref, rb_ref, xo_ref, h_ref, lg_ref):
    acc = None
    for nbr, o_ref in enumerate((oa_ref, ob_ref, oc_ref, od_ref)):
        proj = lax.dot_general(o_ref[0], wb_ref[nbr], (((0,), (0,)), ((), ())),
                               preferred_element_type=F32)
        term = gate_ref[:, nbr * D_MODEL:(nbr + 1) * D_MODEL].astype(F32) * proj
        acc = term if acc is None else acc + term
    mix = _dot(acc.astype(BF16), wo_ref[...])
    mod = mod_ref[0]
    g1 = mod[:, 2 * D_MODEL:3 * D_MODEL]
    sh2 = mod[:, 3 * D_MODEL:4 * D_MODEL]
    sc2 = mod[:, 4 * D_MODEL:5 * D_MODEL]
    xn = x_ref[...] + g1 * mix
    xo_ref[...] = xn
    ms = jnp.mean(xn * xn, axis=-1, keepdims=True)
    h = (xn * lax.rsqrt(ms + EPS) * n2_ref[...]) * (1 + sc2) + sh2
    h_ref[...] = h.astype(BF16)
    lg_ref[...] = jnp.dot(h, rw_ref[...], precision=lax.Precision.HIGHEST,
                          preferred_element_type=F32) + rb_ref[...]


def _merge(oTs, gate, X, mod, wb_bf, wo_bf, norm2_g, rw_pad, rb_pad, n_rows, B, S, C, row_batch):
    nt = n_rows // ROW_TILE
    per = S // ROW_TILE
    nlat = B * per

    def bidx(t):
        return jnp.where(t < nlat, t // per, t - nlat)

    def pidx(t):
        return jnp.where(t < nlat, t % per, per)

    o_spec = pl.BlockSpec((1, 256, ROW_TILE), lambda t: (bidx(t), 0, pidx(t)))
    return pl.pallas_call(
        _merge_kernel,
        grid=(nt,),
        in_specs=[o_spec, o_spec, o_spec, o_spec,
                  pl.BlockSpec((ROW_TILE, GATE_TOT), lambda t: (t, 0)),
                  pl.BlockSpec((ROW_TILE, D_MODEL), lambda t: (t, 0)),
                  pl.BlockSpec((1, 1, N_MOD * D_MODEL), lambda t: (row_batch(t), 0, 0)),
                  pl.BlockSpec((4, 256, D_MODEL), lambda t: (0, 0, 0)),
                  pl.BlockSpec((D_MODEL, D_MODEL), lambda t: (0, 0)),
                  pl.BlockSpec((1, D_MODEL), lambda t: (0, 0)),
                  pl.BlockSpec((D_MODEL, LANES), lambda t: (0, 0)),
                  pl.BlockSpec((1, LANES), lambda t: (0, 0))],
        out_specs=[pl.BlockSpec((ROW_TILE, D_MODEL), lambda t: (t, 0)),
                   pl.BlockSpec((ROW_TILE, D_MODEL), lambda t: (t, 0)),
                   pl.BlockSpec((ROW_TILE, LANES), lambda t: (t, 0))],
        out_shape=[jax.ShapeDtypeStruct((n_rows, D_MODEL), F32),
                   jax.ShapeDtypeStruct((n_rows, D_MODEL), BF16),
                   jax.ShapeDtypeStruct((n_rows, LANES), F32)],
        compiler_params=_cparams(("arbitrary",)),
        name="merge",
    )(*oTs, gate, X, mod, wb_bf, wo_bf, norm2_g.reshape(1, D_MODEL), rw_pad, rb_pad)


def _expert_kernel(be_ref, nu_ref, x_ref, wgu_ref, bgu_ref, wd_ref, bd_ref, y_ref, wgu_s, wd_s):
    i = pl.program_id(0)

    @pl.when(i < nu_ref[0])
    def _():
        prev = be_ref[jnp.maximum(i - 1, 0)]

        @pl.when((i == 0) | (be_ref[i] != prev))
        def _():
            wgu_s[...] = wgu_ref[0].astype(BF16)
            wd_s[...] = wd_ref[0].astype(BF16)

        gu = _dot(x_ref[...], wgu_s[...]) + bgu_ref[0]
        gate = jnp.minimum(gu[:, :D_FF], SWIGLU_LIMIT)
        up = jnp.clip(gu[:, D_FF:], -SWIGLU_LIMIT, SWIGLU_LIMIT)
        a = gate * jax.nn.sigmoid(SWIGLU_ALPHA * gate) * (up + 1)
        y_ref[...] = (_dot(a.astype(BF16), wd_s[...]) + bd_ref[0]).astype(y_ref.dtype)


def _experts(xbuf, blk_e, n_used, w_gate_up, b_gate_up, w_down, b_down):
    P = xbuf.shape[0]
    n_blk = P // EXPERT_ROWS
    E = w_gate_up.shape[0]

    def row_map(i, be, nu):
        return (jnp.minimum(i, nu[0] - 1), 0)

    def w_map(i, be, nu):
        return (be[i], 0, 0)

    grid_spec = pltpu.PrefetchScalarGridSpec(
        num_scalar_prefetch=2,
        grid=(n_blk,),
        in_specs=[
            pl.BlockSpec((EXPERT_ROWS, D_MODEL), row_map),
            pl.BlockSpec((1, D_MODEL, 2 * D_FF), w_map),
            pl.BlockSpec((1, 1, 2 * D_FF), w_map),
            pl.BlockSpec((1, D_FF, D_MODEL), w_map),
            pl.BlockSpec((1, 1, D_MODEL), w_map),
        ],
        out_specs=pl.BlockSpec((EXPERT_ROWS, D_MODEL), row_map),
        scratch_shapes=[pltpu.VMEM((D_MODEL, 2 * D_FF), BF16), pltpu.VMEM((D_FF, D_MODEL), BF16)],
    )
    return pl.pallas_call(
        _expert_kernel,
        grid_spec=grid_spec,
        out_shape=jax.ShapeDtypeStruct((P, D_MODEL), BF16),
        compiler_params=_cparams(("arbitrary",)),
        name="experts",
    )(blk_e, n_used, xbuf, w_gate_up, b_gate_up.reshape(E, 1, 2 * D_FF), w_down, b_down.reshape(E, 1, D_MODEL))


def _combine_kernel(yg_ref, tw_ref, x_ref, mod_ref, o_ref):
    tw = tw_ref[...]
    y = None
    for k in range(TOP_K):
        term = tw[:, k:k + 1] * yg_ref[:, k * D_MODEL:(k + 1) * D_MODEL].astype(F32)
        y = term if y is None else y + term
    g2 = mod_ref[0][:, 5 * D_MODEL:6 * D_MODEL]
    o_ref[...] = x_ref[...] + g2 * y


def _combine(yg, tw, X, mod, row_batch):
    n_rows = X.shape[0]
    return pl.pallas_call(
        _combine_kernel,
        grid=(n_rows // ROW_TILE,),
        in_specs=[pl.BlockSpec((ROW_TILE, TOP_K * D_MODEL), lambda t: (t, 0)),
                  pl.BlockSpec((ROW_TILE, TOP_K), lambda t: (t, 0)),
                  pl.BlockSpec((ROW_TILE, D_MODEL), lambda t: (t, 0)),
                  pl.BlockSpec((1, 1, N_MOD * D_MODEL), lambda t: (row_batch(t), 0, 0))],
        out_specs=pl.BlockSpec((ROW_TILE, D_MODEL), lambda t: (t, 0)),
        out_shape=jax.ShapeDtypeStruct((n_rows, D_MODEL), F32),
        compiler_params=_cparams(("arbitrary",)),
        name="moe_combine",
    )(yg, tw, X, mod)


def _moe(h_bf, logits, X, mod, row_batch, w_gate_up, b_gate_up, w_down, b_down):
    Tn = h_bf.shape[0]
    top_v, top_e = lax.top_k(logits[:, :N_EXPERTS], TOP_K)
    top_w = jax.nn.softmax(top_v, axis=-1)
    n = Tn * TOP_K
    flat_e = top_e.reshape(-1)
    onehot = (flat_e[:, None] == jnp.arange(N_EXPERTS)[None, :]).astype(jnp.int32)
    rank = jnp.take_along_axis(jnp.cumsum(onehot, axis=0), flat_e[:, None], axis=1)[:, 0] - 1
    counts = jnp.sum(onehot, axis=0)
    padded = (counts + EXPERT_ROWS - 1) // EXPERT_ROWS * EXPERT_ROWS
    pad_end = jnp.cumsum(padded)
    pad_start = pad_end - padded
    dest = (pad_start[flat_e] + rank).astype(jnp.int32)
    P = n + N_EXPERTS * EXPERT_ROWS
    n_blk = P // EXPERT_ROWS
    slot_tok = jnp.full((P,), Tn, jnp.int32).at[dest].set(jnp.arange(n, dtype=jnp.int32) // TOP_K)
    n_used = (pad_end[-1] // EXPERT_ROWS).astype(jnp.int32)
    blk = jnp.arange(n_blk, dtype=jnp.int32)
    blk_e = jnp.searchsorted(pad_end, jnp.minimum(blk, n_used - 1) * EXPERT_ROWS, side="right")
    blk_e = jnp.minimum(blk_e, N_EXPERTS - 1).astype(jnp.int32)
    hp = jnp.concatenate([h_bf, jnp.zeros((1, D_MODEL), h_bf.dtype)], axis=0)
    xbuf = jnp.take(hp, slot_tok, axis=0)
    y = _experts(xbuf, blk_e, n_used.reshape(1), w_gate_up, b_gate_up, w_down, b_down)
    yg = jnp.take(y, dest, axis=0).reshape(Tn, TOP_K * D_MODEL)
    return _combine(yg, top_w.astype(F32), X, mod, row_batch)


def _rope_tables(S, dim):
    t = jnp.arange(S, dtype=jnp.int32)
    row = (t // GRID_W).astype(F32)
    col = (t % GRID_W).astype(F32)
    n_freq = dim // 4
    inv = ROPE_THETA ** (-jnp.arange(n_freq, dtype=F32) / n_freq)
    ang = jnp.concatenate([row[:, None] * inv, col[:, None] * inv], axis=-1)
    cos, sin = jnp.cos(ang), jnp.sin(ang)
    reps = LANES // dim
    cos_t = jnp.tile(jnp.concatenate([cos, cos], axis=-1), (1, reps))
    sin_t = jnp.tile(jnp.concatenate([-sin, sin], axis=-1), (1, reps))
    cos_t = jnp.concatenate([cos_t, jnp.ones((ROW_TILE, LANES), F32)], axis=0)
    sin_t = jnp.concatenate([sin_t, jnp.zeros((ROW_TILE, LANES), F32)], axis=0)
    return cos_t, sin_t


def _block_ones(seg):
    i = np.arange(LANES)
    return jnp.asarray((i[:, None] // seg == i[None, :] // seg).astype(np.float32), dtype=BF16)


def _lane_tile(v):
    return jnp.tile(v.astype(F32), LANES // v.shape[0])


def kernel(x, c, ctx, c_ctx, norm1_g, norm2_g, w_ada, b_ada, w_in, b_gate, a_qn, a_kn, b_qn, b_kn, lam_q1, lam_k1, lam_q2, lam_k2, subln_g, c_qn, c_kn, sink, d_qn, d_kn, rpb, w_branch, w_out, router_w, router_b, w_gate_up, b_gate_up, w_down, b_down):
    B, S, D = x.shape
    C = ctx.shape[1]
    L = w_in.shape[0]
    assert D == D_MODEL and C == ROW_TILE and S % ROW_TILE == 0 and B + 1 <= 8
    n_lat = B * S
    per = S // ROW_TILE
    nlat_tiles = B * per

    def row_batch(t):
        return jnp.where(t < nlat_tiles, t // per, B)

    cvec = jnp.zeros((8, D), F32).at[:B].set(c).at[B].set(c_ctx)
    mod_all = _ada(cvec, w_ada, b_ada)

    tables = _rope_tables(S, HEAD_DIM) + _rope_tables(S, B_DK)
    ones64, ones32 = _block_ones(HEAD_DIM), _block_ones(B_DK)

    X = jnp.concatenate([x.reshape(n_lat, D), ctx.reshape(B * C, D)], axis=0)
    for l in range(L):
        last = l == L - 1
        lam_init = 0.8 - 0.6 * math.exp(-0.3 * l)
        mod = mod_all[l].reshape(8, 1, N_MOD * D)
        qkv, gate = _in_proj(X, norm1_g[l], mod, w_in[l].astype(BF16), b_gate[l], row_batch)
        gains = jnp.stack([_lane_tile(g[l]) for g in (a_qn, a_kn, b_qn, b_kn, c_qn, c_kn, d_qn, d_kn)])
        qTs, ks, vTs = _qkv_post(qkv, tables, ones64, ones32, gains, B, S, C)

        lamvec = jnp.zeros((8, LANES), F32)
        for r, v in enumerate((lam_q1, lam_k1, lam_q2, lam_k2)):
            lamvec = lamvec.at[r, :B_DK].set(v[l])
        subg = subln_g[l].reshape(HEAD_DIM, 1)
        sink_l = sink[l].astype(F32)
        nbr_table = _nbr_bias_table(rpb[l])

        o_a = _flash(qTs[0], ks[0], vTs[0], mode="gqa", S=S, C=C, ctx_only=False)
        o_b = _flash(qTs[1], ks[1], vTs[1], mode="diff", S=S, C=C, ctx_only=False,
                     lamvec=lamvec, subg=subg, lam_init=lam_init)
        o_c = _window(qTs[2], ks[2], vTs[2], sink_l, S, C)
        o_d = _nbr(qTs[3], ks[3], vTs[3], nbr_table, S, C)
        if not last:
            o_a = _flash(qTs[0], ks[0], vTs[0], mode="gqa", S=S, C=C, ctx_only=True, out_prev=o_a)
            o_b = _flash(qTs[1], ks[1], vTs[1], mode="diff", S=S, C=C, ctx_only=True,
                         lamvec=lamvec, subg=subg, lam_init=lam_init, out_prev=o_b)
            o_c = _flash(qTs[2], ks[2], vTs[2], mode="gqa", S=S, C=C, ctx_only=True, sink=sink_l, out_prev=o_c)
            o_d = _flash(qTs[3], ks[3], vTs[3], mode="mha", S=S, C=C, ctx_only=True, out_prev=o_d)

        n_rows = n_lat if last else X.shape[0]
        rw_pad = jnp.zeros((D, LANES), F32).at[:, :N_EXPERTS].set(router_w[l])
        rb_pad = jnp.zeros((1, LANES), F32).at[0, :N_EXPERTS].set(router_b[l])
        Xm, h2, logits = _merge((o_a, o_b, o_c, o_d), gate, X, mod, w_branch[l].astype(BF16),
                                w_out[l].astype(BF16), norm2_g[l], rw_pad, rb_pad, n_rows, B, S, C, row_batch)
        X = _moe(h2, logits, Xm, mod, row_batch, w_gate_up[l], b_gate_up[l], w_down[l], b_down[l])
    return X[:n_lat].reshape(B, S, D)
```

```python
import functools
import math

import numpy as np
import jax
import jax.numpy as jnp
from jax import lax
from jax.experimental import pallas as pl
from jax.experimental.pallas import tpu as pltpu

F32 = jnp.float32
BF16 = jnp.bfloat16

D_MODEL = 1024
GRID_W = 64
HEAD_DIM = 64
B_DK = 32
WINDOW = 128
NA_ROWS = 8
NA_COLS = 16
ROPE_THETA = 10000.0
N_EXPERTS = 32
TOP_K = 4
D_FF = D_MODEL
SWIGLU_LIMIT = 7.0
SWIGLU_ALPHA = 1.702
N_MOD = 6
EPS = 1e-6
NEG_INF = -1e30
LOG2E = 1.4426950408889634

Q_TOT = 1024
KV_TOT = 1536
QKV_TOT = Q_TOT + KV_TOT
GATE_TOT = 4 * D_MODEL

ROW_TILE = 256
LANES = 128
EXPERT_ROWS = 256
NBR_WIN_ROWS = 10
NBR_TABLE_ROWS = NBR_WIN_ROWS + 8
SUM_ROWS = 16
VMEM_LIMIT = 52 * 1024 * 1024


def _cparams(sem):
    return pltpu.CompilerParams(dimension_semantics=sem, vmem_limit_bytes=VMEM_LIMIT)


def _dot(a, b):
    return jnp.dot(a, b, preferred_element_type=F32)


def _ada_kernel(c_ref, w_ref, b_ref, o_ref):
    c = c_ref[...]
    s = c * jax.nn.sigmoid(c)
    o_ref[0] = jnp.dot(s, w_ref[0], precision=lax.Precision.HIGHEST,
                       preferred_element_type=F32) + b_ref[0]


def _ada(cvec, w_ada, b_ada):
    L = w_ada.shape[0]
    n_out = w_ada.shape[2]
    tn = 1536
    return pl.pallas_call(
        _ada_kernel,
        grid=(L, n_out // tn),
        in_specs=[
            pl.BlockSpec((8, D_MODEL), lambda l, j: (0, 0)),
            pl.BlockSpec((1, D_MODEL, tn), lambda l, j: (l, 0, j)),
            pl.BlockSpec((1, 1, tn), lambda l, j: (l, 0, j)),
        ],
        out_specs=pl.BlockSpec((1, 8, tn), lambda l, j: (l, 0, j)),
        out_shape=jax.ShapeDtypeStruct((L, 8, n_out), F32),
        compiler_params=_cparams(("arbitrary", "arbitrary")),
        name="ada_mod",
    )(cvec, w_ada, b_ada.reshape(L, 1, n_out))


def _in_kernel(x_ref, g_ref, mod_ref, w_ref, bg_ref, qkv_ref, gate_ref, *, tn):
    x = x_ref[...]
    ms = jnp.mean(x * x, axis=-1, keepdims=True)
    y = x * lax.rsqrt(ms + EPS) * g_ref[...]
    mod = mod_ref[0]
    sh = mod[:, 0:D_MODEL]
    sc = mod[:, D_MODEL:2 * D_MODEL]
    h = (y * (1 + sc) + sh).astype(BF16)
    for j in range(QKV_TOT // tn):
        qkv_ref[:, j * tn:(j + 1) * tn] = _dot(h, w_ref[:, j * tn:(j + 1) * tn]).astype(BF16)
    for j in range(GATE_TOT // tn):
        g = _dot(h, w_ref[:, QKV_TOT + j * tn:QKV_TOT + (j + 1) * tn]) + bg_ref[:, j * tn:(j + 1) * tn]
        gate_ref[:, j * tn:(j + 1) * tn] = jax.nn.sigmoid(g).astype(BF16)


def _in_proj(X, norm_g, mod, w_in_bf, b_gate, row_batch):
    R = X.shape[0]
    nt = R // ROW_TILE
    return pl.pallas_call(
        functools.partial(_in_kernel, tn=512),
        grid=(nt,),
        in_specs=[
            pl.BlockSpec((ROW_TILE, D_MODEL), lambda t: (t, 0)),
            pl.BlockSpec((1, D_MODEL), lambda t: (0, 0)),
            pl.BlockSpec((1, 1, N_MOD * D_MODEL), lambda t: (row_batch(t), 0, 0)),
            pl.BlockSpec((D_MODEL, QKV_TOT + GATE_TOT), lambda t: (0, 0), pipeline_mode=pl.Buffered(1)),
            pl.BlockSpec((1, GATE_TOT), lambda t: (0, 0)),
        ],
        out_specs=[
            pl.BlockSpec((ROW_TILE, QKV_TOT), lambda t: (t, 0)),
            pl.BlockSpec((ROW_TILE, GATE_TOT), lambda t: (t, 0)),
        ],
        out_shape=[
            jax.ShapeDtypeStruct((R, QKV_TOT), BF16),
            jax.ShapeDtypeStruct((R, GATE_TOT), BF16),
        ],
        compiler_params=_cparams(("arbitrary",)),
        name="in_proj",
    )(X, norm_g.reshape(1, D_MODEL), mod, w_in_bf, b_gate.reshape(1, GATE_TOT))


def _seg_rms(x, ones, seg, g):
    sq = x * x
    hi = sq.astype(BF16)
    lo = (sq - hi.astype(F32)).astype(BF16)
    ss = _dot(hi, ones) + _dot(lo, ones)
    return x * lax.rsqrt(ss * (1.0 / seg) + EPS) * g


def _rot_half(y, half):
    lane = lax.broadcasted_iota(jnp.int32, y.shape, 1)
    fwd = pltpu.roll(y, LANES - half, axis=1)
    bwd = pltpu.roll(y, half, axis=1)
    return jnp.where((lane % (2 * half)) < half, fwd, bwd)


def _qkv_post_kernel(qkv_ref, c64_ref, s64_ref, c32_ref, s32_ref, ones64_ref, ones32_ref, gains_ref,
                     qa_ref, qb_ref, qc_ref, qd_ref, ka_ref, kb_ref, kc_ref, kd_ref,
                     va_ref, vb_ref, vc_ref, vd_ref):
    gains = gains_ref[...]
    rope = {64: (c64_ref[...], s64_ref[...]), 32: (c32_ref[...], s32_ref[...])}
    ones = {64: ones64_ref[...], 32: ones32_ref[...]}

    def chunk(col):
        return qkv_ref[:, col:col + LANES].astype(F32)

    def normed(col, seg, gain_row, use_rope, scale):
        y = _seg_rms(chunk(col), ones[seg], seg, gains[gain_row:gain_row + 1, :])
        if use_rope:
            cos, sin = rope[seg]
            y = y * cos + _rot_half(y, seg // 2) * sin
        return y * scale if scale != 1.0 else y

    q_refs = (qa_ref, qb_ref, qc_ref, qd_ref)
    q_seg = (64, 32, 64, 64)
    q_rope = (True, True, True, False)
    for m in range(4):
        scale = float(q_seg[m]) ** -0.5 * LOG2E
        for c in range(2):
            y = normed(m * 256 + c * LANES, q_seg[m], 2 * m, q_rope[m], scale)
            q_refs[m][0, c * LANES:(c + 1) * LANES, :] = y.T.astype(BF16)

    k_refs = (ka_ref, kb_ref, kc_ref, kd_ref)
    v_refs = (va_ref, vb_ref, vc_ref, vd_ref)
    widths = (128, 256, 128, 256)
    kcol = Q_TOT
    vcol = Q_TOT + sum(widths)
    for m in range(4):
        for c in range(widths[m] // LANES):
            y = normed(kcol, q_seg[m], 2 * m + 1, q_rope[m], 1.0)
            k_refs[m][0, :, c * LANES:(c + 1) * LANES] = y.astype(BF16)
            v_refs[m][0, c * LANES:(c + 1) * LANES, :] = chunk(vcol).T.astype(BF16)
            kcol += LANES
            vcol += LANES


def _qkv_post(qkv, tables, ones64, ones32, gains, B, S, C):
    R = qkv.shape[0]
    nt = R // ROW_TILE
    per = S // ROW_TILE
    nlat = B * per
    T = S + C

    def bidx(t):
        return jnp.where(t < nlat, t // per, t - nlat)

    def pidx(t):
        return jnp.where(t < nlat, t % per, per)

    tab_spec = pl.BlockSpec((ROW_TILE, LANES), lambda t: (pidx(t), 0))
    const_spec = pl.BlockSpec((LANES, LANES), lambda t: (0, 0))
    widths = (128, 256, 128, 256)
    q_specs = [pl.BlockSpec((1, 256, ROW_TILE), lambda t: (bidx(t), 0, pidx(t))) for _ in range(4)]
    k_specs = [pl.BlockSpec((1, ROW_TILE, w), lambda t: (bidx(t), pidx(t), 0)) for w in widths]
    v_specs = [pl.BlockSpec((1, w, ROW_TILE), lambda t: (bidx(t), 0, pidx(t))) for w in widths]
    q_shapes = [jax.ShapeDtypeStruct((B, 256, T), BF16) for _ in range(4)]
    k_shapes = [jax.ShapeDtypeStruct((B, T, w), BF16) for w in widths]
    v_shapes = [jax.ShapeDtypeStruct((B, w, T), BF16) for w in widths]
    outs = pl.pallas_call(
        _qkv_post_kernel,
        grid=(nt,),
        in_specs=[pl.BlockSpec((ROW_TILE, QKV_TOT), lambda t: (t, 0)),
                  tab_spec, tab_spec, tab_spec, tab_spec, const_spec, const_spec,
                  pl.BlockSpec((8, LANES), lambda t: (0, 0))],
        out_specs=q_specs + k_specs + v_specs,
        out_shape=q_shapes + k_shapes + v_shapes,
        compiler_params=_cparams(("arbitrary",)),
        name="qkv_post",
    )(qkv, *tables, ones64, ones32, gains)
    return outs[0:4], outs[4:8], outs[8:12]


def _pad_queries(q, pieces, kw, unit):
    tq = q.shape[1]
    r = lax.broadcasted_iota(jnp.int32, (kw, tq), 0)
    blocks = []
    for row0, size, extra in pieces:
        tiled = jnp.concatenate([q[row0:row0 + size, :]] * (kw // size), axis=0)
        off = unit * HEAD_DIM + extra
        blocks.append(jnp.where((r >= off) & (r < off + size), tiled, 0.0))
    out = blocks[0] if len(blocks) == 1 else jnp.concatenate(blocks, axis=1)
    return out.astype(BF16)


_PIECES = {
    "gqa": ((0, 64, 0), (64, 64, 0)),
    "diff": ((0, 32, 0), (32, 32, 32)),
    "mha": ((0, 64, 0),),
}


def _flash_kernel(*refs, mode, kw, tq, tk, kv_start, n_chunks, use_sink, lam_init, aliased):
    refs = list(refs)
    qT_ref, k_ref, vT_ref = refs[0:3]
    pos = 3
    sink_ref = None
    if use_sink:
        sink_ref = refs[pos]
        pos += 1
    if mode == "diff":
        lamvec_ref, subg_ref = refs[pos:pos + 2]
        pos += 2
    if aliased:
        pos += 1
    o_ref = refs[pos]
    s_a, s_b = refs[pos + 1:pos + 3]

    unit = pl.program_id(1)
    pieces = _PIECES[mode]
    ng = len(pieces)
    n = ng * tq
    qpad = _pad_queries(qT_ref[0].astype(F32), pieces, kw, unit)

    if use_sink:
        m0 = jnp.concatenate([jnp.full((1, tq), sink_ref[ng * unit + g] * LOG2E, F32) for g in range(ng)],
                             axis=1)
        l0 = jnp.ones((1, n), F32)
    else:
        m0 = jnp.full((1, n), NEG_INF, F32)
        l0 = jnp.zeros((1, n), F32)
    acc0 = jnp.zeros((HEAD_DIM, n), F32)
    ones_rows = jnp.ones((SUM_ROWS, tk), BF16)

    def chunk_start(j):
        return pl.multiple_of(kv_start + j * tk, LANES)

    def scores(j, s_ref):
        s = _dot(k_ref[0, pl.ds(chunk_start(j), tk), :], qpad)
        s_ref[...] = s
        return jnp.max(s, axis=0, keepdims=True)

    def absorb(j, s_ref, mc, state):
        m, l, acc = state
        m_new = jnp.maximum(m, mc)
        alpha = jnp.exp2(m - m_new)
        p = jnp.exp2(s_ref[...] - m_new)
        v_aug = jnp.concatenate([vT_ref[0, :, pl.ds(chunk_start(j), tk)], ones_rows], axis=0)
        r = _dot(v_aug, p.astype(BF16))
        return m_new, alpha * l + r[HEAD_DIM:HEAD_DIM + 1], alpha * acc + r[0:HEAD_DIM]

    mc_a = scores(0, s_a)
    n_pairs = (n_chunks - 1) // 2

    def pair(i, carry):
        mc_a, state = carry[0], carry[1:]
        mc_b = scores(2 * i + 1, s_b)
        state = absorb(2 * i, s_a, mc_a, state)
        mc_a = scores(2 * i + 2, s_a)
        state = absorb(2 * i + 1, s_b, mc_b, state)
        return (mc_a,) + tuple(state)

    carry = (mc_a, m0, l0, acc0)
    if n_pairs > 0:
        carry = lax.fori_loop(0, n_pairs, pair, carry)
    mc_a, state = carry[0], carry[1:]
    if n_chunks % 2 == 0:
        mc_b = scores(n_chunks - 1, s_b)
        state = absorb(n_chunks - 2, s_a, mc_a, state)
        state = absorb(n_chunks - 1, s_b, mc_b, state)
    else:
        state = absorb(n_chunks - 1, s_a, mc_a, state)
    m, l, acc = state
    o = acc / l

    if mode == "gqa":
        for g in range(ng):
            o_ref[0, g * HEAD_DIM:(g + 1) * HEAD_DIM, :] = o[:, g * tq:(g + 1) * tq].astype(o_ref.dtype)
    elif mode == "mha":
        o_ref[0] = o.astype(o_ref.dtype)
    else:
        lv = lamvec_ref[...]
        lam = (jnp.exp(jnp.sum(lv[0:1] * lv[1:2], axis=1, keepdims=True))
               - jnp.exp(jnp.sum(lv[2:3] * lv[3:4], axis=1, keepdims=True)) + lam_init)
        d = o[:, 0:tq] - lam * o[:, tq:2 * tq]
        ms = jnp.mean(d * d, axis=0, keepdims=True)
        o_ref[0] = (d * lax.rsqrt(ms + EPS) * subg_ref[...] * (1.0 - lam_init)).astype(o_ref.dtype)


def _flash(qT, k, vT, *, mode, S, C, ctx_only, sink=None, lamvec=None, subg=None, lam_init=0.0, out_prev=None):
    B, _, T = qT.shape
    kw = k.shape[2]
    q_rows = 128 if mode == "gqa" else 64
    units = 256 // q_rows
    if ctx_only:
        tq, nq, q_blk0 = C, 1, S // C
        tk, kv_start, n_chunks = C, S, 1
    else:
        tq, nq, q_blk0 = 256, S // 256, 0
        tk = 768 if T % 768 == 0 else 256
        kv_start, n_chunks = 0, T // tk
    in_specs = [
        pl.BlockSpec((1, q_rows, tq), lambda b, u, i: (b, u, i + q_blk0)),
        pl.BlockSpec((1, T, kw), lambda b, u, i: (b, 0, 0)),
        pl.BlockSpec((1, HEAD_DIM, T), lambda b, u, i: (b, u, 0)),
    ]
    args = [qT, k, vT]
    if sink is not None:
        in_specs.append(pl.BlockSpec(memory_space=pltpu.SMEM))
        args.append(sink)
    if mode == "diff":
        in_specs += [pl.BlockSpec((8, LANES), lambda b, u, i: (0, 0)),
                     pl.BlockSpec((HEAD_DIM, 1), lambda b, u, i: (0, 0))]
        args += [lamvec, subg]
    aliases = {}
    if out_prev is not None:
        in_specs.append(pl.BlockSpec(memory_space=pl.ANY))
        aliases = {len(args): 0}
        args.append(out_prev)
    kern = functools.partial(_flash_kernel, mode=mode, kw=kw, tq=tq, tk=tk, kv_start=kv_start,
                             n_chunks=n_chunks, use_sink=sink is not None, lam_init=lam_init,
                             aliased=out_prev is not None)
    return pl.pallas_call(
        kern,
        grid=(B, units, nq),
        in_specs=in_specs,
        out_specs=pl.BlockSpec((1, q_rows, tq), lambda b, u, i: (b, u, i + q_blk0)),
        out_shape=jax.ShapeDtypeStruct((B, 256, T), BF16),
        input_output_aliases=aliases,
        scratch_shapes=[pltpu.VMEM((tk, len(_PIECES[mode]) * tq), F32)] * 2,
        compiler_params=_cparams(("arbitrary", "arbitrary", "arbitrary")),
        name="flash_" + mode + ("_ctx" if ctx_only else ""),
    )(*args)


def _window_kernel(qT_ref, k_ref, vT_ref, sink_ref, o_ref, *, S, C):
    unit = pl.program_id(1)
    i = pl.program_id(2)
    qb = WINDOW
    span = 3 * qb
    n = 2 * qb
    qpad = _pad_queries(qT_ref[0].astype(F32), _PIECES["gqa"], LANES, unit)
    start = pl.multiple_of(jnp.clip((i - 1) * qb, 0, S - span), LANES)
    s_loc = _dot(k_ref[0, pl.ds(start, span), :], qpad)
    s_ctx = _dot(k_ref[0, S:S + C, :], qpad)
    kpos = start + lax.broadcasted_iota(jnp.int32, (span, n), 0)
    qpos = i * qb + lax.broadcasted_iota(jnp.int32, (span, n), 1) % qb
    s_loc = jnp.where(jnp.abs(kpos - qpos) <= WINDOW, s_loc, NEG_INF)
    sink = jnp.concatenate([jnp.full((1, qb), sink_ref[2 * unit + g] * LOG2E, F32) for g in range(2)],
                           axis=1)
    m = jnp.maximum(jnp.maximum(jnp.max(s_loc, axis=0, keepdims=True),
                                jnp.max(s_ctx, axis=0, keepdims=True)), sink)
    e_loc = jnp.exp2(s_loc - m)
    e_ctx = jnp.exp2(s_ctx - m)
    den = (jnp.sum(e_loc, axis=0, keepdims=True) + jnp.sum(e_ctx, axis=0, keepdims=True)
           + jnp.exp2(sink - m))
    o = (_dot(vT_ref[0, :, pl.ds(start, span)], e_loc.astype(BF16))
         + _dot(vT_ref[0, :, S:S + C], e_ctx.astype(BF16))) / den
    for g in range(2):
        o_ref[0, g * HEAD_DIM:(g + 1) * HEAD_DIM, :] = o[:, g * qb:(g + 1) * qb].astype(o_ref.dtype)


def _window(qT, k, vT, sink, S, C):
    B, _, T = qT.shape
    return pl.pallas_call(
        functools.partial(_window_kernel, S=S, C=C),
        grid=(B, 2, S // WINDOW),
        in_specs=[
            pl.BlockSpec((1, 128, WINDOW), lambda b, u, i: (b, u, i)),
            pl.BlockSpec((1, T, LANES), lambda b, u, i: (b, 0, 0)),
            pl.BlockSpec((1, HEAD_DIM, T), lambda b, u, i: (b, u, 0)),
            pl.BlockSpec(memory_space=pltpu.SMEM),
        ],
        out_specs=pl.BlockSpec((1, 128, WINDOW), lambda b, u, i: (b, u, i)),
        out_shape=jax.ShapeDtypeStruct((B, 256, T), BF16),
        compiler_params=_cparams(("arbitrary", "arbitrary", "arbitrary")),
        name="window_attn",
    )(qT, k, vT, sink)


def _nbr_kernel(qT_ref, k_ref, vT_ref, tab_ref, o_ref, *, S, C):
    unit = pl.program_id(1)
    j = pl.program_id(2)
    rows = S // GRID_W
    n = 2 * GRID_W
    span = NBR_WIN_ROWS * GRID_W
    qpad = _pad_queries(qT_ref[0].astype(F32), _PIECES["mha"], 256, unit)
    w0 = jnp.clip(2 * j - NA_ROWS // 2, 0, rows - NBR_WIN_ROWS)
    shift = w0 - 2 * j + NA_ROWS // 2 + 4
    start = pl.multiple_of(w0 * GRID_W, LANES)
    s_loc = _dot(k_ref[0, pl.ds(start, span), :], qpad)
    s_ctx = _dot(k_ref[0, S:S + C, :], qpad)
    bias = tab_ref[0, pl.ds(pl.multiple_of(shift * GRID_W, GRID_W), span), :]
    kr = w0 + lax.broadcasted_iota(jnp.int32, (span, n), 0) // GRID_W
    qr = 2 * j + lax.broadcasted_iota(jnp.int32, (span, n), 1) // GRID_W
    r0 = jnp.clip(qr - NA_ROWS // 2, 0, rows - NA_ROWS)
    s_loc = jnp.where((kr >= r0) & (kr < r0 + NA_ROWS), s_loc + bias, NEG_INF)
    m = jnp.maximum(jnp.max(s_loc, axis=0, keepdims=True), jnp.max(s_ctx, axis=0, keepdims=True))
    e_loc = jnp.exp2(s_loc - m)
    e_ctx = jnp.exp2(s_ctx - m)
    den = jnp.sum(e_loc, axis=0, keepdims=True) + jnp.sum(e_ctx, axis=0, keepdims=True)
    o = (_dot(vT_ref[0, :, pl.ds(start, span)], e_loc.astype(BF16))
         + _dot(vT_ref[0, :, S:S + C], e_ctx.astype(BF16))) / den
    o_ref[0] = o.astype(o_ref.dtype)


def _nbr_bias_table(rpb):
    u = np.arange(NBR_TABLE_ROWS)[:, None, None, None]
    kc = np.arange(GRID_W)[None, :, None, None]
    e = np.arange(2)[None, None, :, None]
    qc = np.arange(GRID_W)[None, None, None, :]
    dr = u - e - 1
    row_ok = (dr >= 0) & (dr < 2 * NA_ROWS - 1)
    dc = np.clip(kc - qc, -(NA_COLS - 1), NA_COLS - 1) + (NA_COLS - 1)
    c0 = np.clip(qc - NA_COLS // 2, 0, GRID_W - NA_COLS)
    col_ok = (kc >= c0) & (kc < c0 + NA_COLS)
    shape = (NBR_TABLE_ROWS, GRID_W, 2, GRID_W)
    n_dr, n_dc = 2 * NA_ROWS - 1, 2 * NA_COLS - 1
    sel_r = ((dr[:, 0, :, 0, None] == np.arange(n_dr)) & row_ok[:, 0, :, 0, None]).astype(np.float32)
    sel_c = (dc[0, :, 0, :, None] == np.arange(n_dc)).astype(np.float32)
    vals = jnp.einsum("uer,hrc,kqc->hukeq", sel_r, rpb.astype(F32) * LOG2E, sel_c,
                      precision=lax.Precision.HIGHEST)
    vals = jnp.where(np.broadcast_to(col_ok, shape), vals, NEG_INF)
    return vals.reshape(rpb.shape[0], NBR_TABLE_ROWS * GRID_W, 2 * GRID_W)


def _nbr(qT, k, vT, table, S, C):
    B, _, T = qT.shape
    return pl.pallas_call(
        functools.partial(_nbr_kernel, S=S, C=C),
        grid=(B, 4, S // (2 * GRID_W)),
        in_specs=[
            pl.BlockSpec((1, HEAD_DIM, 2 * GRID_W), lambda b, u, j: (b, u, j)),
            pl.BlockSpec((1, T, 256), lambda b, u, j: (b, 0, 0)),
            pl.BlockSpec((1, HEAD_DIM, T), lambda b, u, j: (b, u, 0)),
            pl.BlockSpec((1, NBR_TABLE_ROWS * GRID_W, 2 * GRID_W), lambda b, u, j: (u, 0, 0)),
        ],
        out_specs=pl.BlockSpec((1, HEAD_DIM, 2 * GRID_W), lambda b, u, j: (b, u, j)),
        out_shape=jax.ShapeDtypeStruct((B, 256, T), BF16),
        compiler_params=_cparams(("arbitrary", "arbitrary", "arbitrary")),
        name="nbr_attn",
    )(qT, k, vT, table)


ROUTE_W, ROUTE_E, ROUTE_RANK = 0, 4, 8


def _merge_kernel(oa_ref, ob_ref, oc_ref, od_ref, gate_ref, x_ref, mod_ref, wb_ref, wo_ref, n2_ref,
                  rw_ref, rb_ref, tri_ref, xo_ref, h_ref, route_ref, cnt_ref, base_ref):
    @pl.when(pl.program_id(0) == 0)
    def _():
        base_ref[...] = jnp.zeros_like(base_ref)

    acc = None
    for nbr, o_ref in enumerate((oa_ref, ob_ref, oc_ref, od_ref)):
        proj = lax.dot_general(o_ref[0], wb_ref[nbr], (((0,), (0,)), ((), ())),
                               preferred_element_type=F32)
        term = gate_ref[:, nbr * D_MODEL:(nbr + 1) * D_MODEL].astype(F32) * proj
        acc = term if acc is None else acc + term
    mix = _dot(acc.astype(BF16), wo_ref[...])
    mod = mod_ref[0]
    g1 = mod[:, 2 * D_MODEL:3 * D_MODEL]
    sh2 = mod[:, 3 * D_MODEL:4 * D_MODEL]
    sc2 = mod[:, 4 * D_MODEL:5 * D_MODEL]
    xn = x_ref[...] + g1 * mix
    xo_ref[...] = xn
    ms = jnp.mean(xn * xn, axis=-1, keepdims=True)
    h = (xn * lax.rsqrt(ms + EPS) * n2_ref[...]) * (1 + sc2) + sh2
    h_ref[...] = h.astype(BF16)
    logits = jnp.dot(h, rw_ref[...], precision=lax.Precision.HIGHEST,
                     preferred_element_type=F32) + rb_ref[...]
    lane = lax.broadcasted_iota(jnp.int32, logits.shape, 1)
    lane_f = lane.astype(F32)
    work = logits
    picks = []
    for _ in range(TOP_K):
        top = jnp.max(work, axis=1, keepdims=True)
        idx = jnp.min(jnp.where(work == top, lane_f, float(LANES)), axis=1, keepdims=True)
        hit = lane_f == idx
        picks.append((top, idx, hit))
        work = jnp.where(hit, -jnp.inf, work)
    ex = [jnp.exp(top - picks[0][0]) for top, _, _ in picks]
    den = ex[0] + ex[1] + ex[2] + ex[3]
    chosen = jnp.zeros(logits.shape, F32)
    for _, _, hit in picks:
        chosen = chosen + hit.astype(F32)
    before = _dot(tri_ref[...], chosen.astype(BF16)) + base_ref[...]
    route = jnp.zeros(logits.shape, F32)
    for k, (_, idx, hit) in enumerate(picks):
        rank = jnp.sum(jnp.where(hit, before, 0.0), axis=1, keepdims=True)
        route = jnp.where(lane == ROUTE_W + k, ex[k] / den, route)
        route = jnp.where(lane == ROUTE_E + k, idx, route)
        route = jnp.where(lane == ROUTE_RANK + k, rank, route)
    route_ref[...] = route
    base_ref[...] = base_ref[...] + jnp.sum(chosen, axis=0, keepdims=True)
    cnt_ref[...] = jnp.broadcast_to(base_ref[...], cnt_ref.shape)


def _merge(oTs, gate, X, mod, wb_bf, wo_bf, norm2_g, rw_pad, rb_pad, n_rows, B, S, C, row_batch):
    nt = n_rows // ROW_TILE
    per = S // ROW_TILE
    nlat = B * per

    def bidx(t):
        return jnp.where(t < nlat, t // per, t - nlat)

    def pidx(t):
        return jnp.where(t < nlat, t % per, per)

    o_spec = pl.BlockSpec((1, 256, ROW_TILE), lambda t: (bidx(t), 0, pidx(t)))
    r = np.arange(ROW_TILE)
    strict_lower = jnp.asarray((r[:, None] > r[None, :]).astype(np.float32), dtype=BF16)
    return pl.pallas_call(
        _merge_kernel,
        grid=(nt,),
        in_specs=[o_spec, o_spec, o_spec, o_spec,
                  pl.BlockSpec((ROW_TILE, GATE_TOT), lambda t: (t, 0)),
                  pl.BlockSpec((ROW_TILE, D_MODEL), lambda t: (t, 0)),
                  pl.BlockSpec((1, 1, N_MOD * D_MODEL), lambda t: (row_batch(t), 0, 0)),
                  pl.BlockSpec((4, 256, D_MODEL), lambda t: (0, 0, 0)),
                  pl.BlockSpec((D_MODEL, D_MODEL), lambda t: (0, 0)),
                  pl.BlockSpec((1, D_MODEL), lambda t: (0, 0)),
                  pl.BlockSpec((D_MODEL, LANES), lambda t: (0, 0)),
                  pl.BlockSpec((1, LANES), lambda t: (0, 0)),
                  pl.BlockSpec((ROW_TILE, ROW_TILE), lambda t: (0, 0))],
        out_specs=[pl.BlockSpec((ROW_TILE, D_MODEL), lambda t: (t, 0)),
                   pl.BlockSpec((ROW_TILE, D_MODEL), lambda t: (t, 0)),
                   pl.BlockSpec((ROW_TILE, LANES), lambda t: (t, 0)),
                   pl.BlockSpec((8, LANES), lambda t: (0, 0))],
        out_shape=[jax.ShapeDtypeStruct((n_rows, D_MODEL), F32),
                   jax.ShapeDtypeStruct((n_rows, D_MODEL), BF16),
                   jax.ShapeDtypeStruct((n_rows, LANES), F32),
                   jax.ShapeDtypeStruct((8, LANES), F32)],
        scratch_shapes=[pltpu.VMEM((1, LANES), F32)],
        compiler_params=_cparams(("arbitrary",)),
        name="merge",
    )(*oTs, gate, X, mod, wb_bf, wo_bf, norm2_g.reshape(1, D_MODEL), rw_pad, rb_pad, strict_lower)


def _expert_kernel(be_ref, nu_ref, x_ref, wgu_ref, bgu_ref, wd_ref, bd_ref, y_ref, wgu_s, wd_s):
    i = pl.program_id(0)

    @pl.when(i < nu_ref[0])
    def _():
        prev = be_ref[jnp.maximum(i - 1, 0)]

        @pl.when((i == 0) | (be_ref[i] != prev))
        def _():
            wgu_s[...] = wgu_ref[0].astype(BF16)
            wd_s[...] = wd_ref[0].astype(BF16)

        gu = _dot(x_ref[...], wgu_s[...]) + bgu_ref[0]
        gate = jnp.minimum(gu[:, :D_FF], SWIGLU_LIMIT)
        up = jnp.clip(gu[:, D_FF:], -SWIGLU_LIMIT, SWIGLU_LIMIT)
        a = gate * jax.nn.sigmoid(SWIGLU_ALPHA * gate) * (up + 1)
        y_ref[...] = (_dot(a.astype(BF16), wd_s[...]) + bd_ref[0]).astype(y_ref.dtype)


def _experts(xbuf, blk_e, n_used, w_gate_up, b_gate_up, w_down, b_down):
    P = xbuf.shape[0]
    n_blk = P // EXPERT_ROWS
    E = w_gate_up.shape[0]

    def row_map(i, be, nu):
        return (jnp.minimum(i, nu[0] - 1), 0)

    def w_map(i, be, nu):
        return (be[i], 0, 0)

    grid_spec = pltpu.PrefetchScalarGridSpec(
        num_scalar_prefetch=2,
        grid=(n_blk,),
        in_specs=[
            pl.BlockSpec((EXPERT_ROWS, D_MODEL), row_map),
            pl.BlockSpec((1, D_MODEL, 2 * D_FF), w_map),
            pl.BlockSpec((1, 1, 2 * D_FF), w_map),
            pl.BlockSpec((1, D_FF, D_MODEL), w_map),
            pl.BlockSpec((1, 1, D_MODEL), w_map),
        ],
        out_specs=pl.BlockSpec((EXPERT_ROWS, D_MODEL), row_map),
        scratch_shapes=[pltpu.VMEM((D_MODEL, 2 * D_FF), BF16), pltpu.VMEM((D_FF, D_MODEL), BF16)],
    )
    return pl.pallas_call(
        _expert_kernel,
        grid_spec=grid_spec,
        out_shape=jax.ShapeDtypeStruct((P, D_MODEL), BF16),
        compiler_params=_cparams(("arbitrary",)),
        name="experts",
    )(blk_e, n_used, xbuf, w_gate_up, b_gate_up.reshape(E, 1, 2 * D_FF), w_down, b_down.reshape(E, 1, D_MODEL))


def _combine_kernel(yg_ref, route_ref, x_ref, mod_ref, o_ref):
    route = route_ref[...]
    y = None
    for k in range(TOP_K):
        term = route[:, ROUTE_W + k:ROUTE_W + k + 1] * yg_ref[k].astype(F32)
        y = term if y is None else y + term
    g2 = mod_ref[0][:, 5 * D_MODEL:6 * D_MODEL]
    o_ref[...] = x_ref[...] + g2 * y


def _combine(yg, route, X, mod, row_batch):
    n_rows = X.shape[0]
    return pl.pallas_call(
        _combine_kernel,
        grid=(n_rows // ROW_TILE,),
        in_specs=[pl.BlockSpec((TOP_K, ROW_TILE, D_MODEL), lambda t: (0, t, 0)),
                  pl.BlockSpec((ROW_TILE, LANES), lambda t: (t, 0)),
                  pl.BlockSpec((ROW_TILE, D_MODEL), lambda t: (t, 0)),
                  pl.BlockSpec((1, 1, N_MOD * D_MODEL), lambda t: (row_batch(t), 0, 0))],
        out_specs=pl.BlockSpec((ROW_TILE, D_MODEL), lambda t: (t, 0)),
        out_shape=jax.ShapeDtypeStruct((n_rows, D_MODEL), F32),
        compiler_params=_cparams(("arbitrary",)),
        name="moe_combine",
    )(yg, route, X, mod)


def _moe(h_bf, route, counts_f, X, mod, row_batch, layer, w_gate_up, b_gate_up, w_down, b_down):
    Tn = h_bf.shape[0]
    n = Tn * TOP_K
    top_e = route[:, ROUTE_E:ROUTE_E + TOP_K].astype(jnp.int32)
    rank = route[:, ROUTE_RANK:ROUTE_RANK + TOP_K].astype(jnp.int32)
    counts = counts_f[0, :N_EXPERTS].astype(jnp.int32)
    padded = (counts + EXPERT_ROWS - 1) // EXPERT_ROWS * EXPERT_ROWS
    pad_end = jnp.cumsum(padded)
    pad_start = pad_end - padded
    experts = jnp.arange(N_EXPERTS, dtype=jnp.int32)
    start_of = jnp.sum(jnp.where(top_e[:, :, None] == experts, pad_start, 0), axis=-1)
    dest = start_of + rank
    P = n + N_EXPERTS * EXPERT_ROWS
    n_blk = P // EXPERT_ROWS
    tok = jnp.broadcast_to(jnp.arange(Tn, dtype=jnp.int32)[:, None], (Tn, TOP_K))
    slot_tok = jnp.full((P,), Tn, jnp.int32).at[dest.reshape(-1)].set(
        tok.reshape(-1), unique_indices=True, mode="promise_in_bounds")
    n_used = (pad_end[-1] // EXPERT_ROWS).astype(jnp.int32)
    blk_row = jnp.minimum(jnp.arange(n_blk, dtype=jnp.int32), n_used - 1) * EXPERT_ROWS
    blk_e = jnp.sum((pad_end[None, :] <= blk_row[:, None]).astype(jnp.int32), axis=1)
    blk_e = jnp.minimum(blk_e, N_EXPERTS - 1) + layer * N_EXPERTS
    hp = jnp.concatenate([h_bf, jnp.zeros((8, D_MODEL), h_bf.dtype)], axis=0)
    xbuf = hp.at[slot_tok].get(mode="promise_in_bounds")
    y = _experts(xbuf, blk_e, n_used.reshape(1), w_gate_up, b_gate_up, w_down, b_down)
    yg = y.at[dest.T.reshape(-1)].get(mode="promise_in_bounds").reshape(TOP_K, Tn, D_MODEL)
    return _combine(yg, route, X, mod, row_batch)


def _rope_tables(S, dim):
    t = jnp.arange(S, dtype=jnp.int32)
    row = (t // GRID_W).astype(F32)
    col = (t % GRID_W).astype(F32)
    n_freq = dim // 4
    inv = ROPE_THETA ** (-jnp.arange(n_freq, dtype=F32) / n_freq)
    ang = jnp.concatenate([row[:, None] * inv, col[:, None] * inv], axis=-1)
    cos, sin = jnp.cos(ang), jnp.sin(ang)
    reps = LANES // dim
    cos_t = jnp.tile(jnp.concatenate([cos, cos], axis=-1), (1, reps))
    sin_t = jnp.tile(jnp.concatenate([-sin, sin], axis=-1), (1, reps))
    cos_t = jnp.concatenate([cos_t, jnp.ones((ROW_TILE, LANES), F32)], axis=0)
    sin_t = jnp.concatenate([sin_t, jnp.zeros((ROW_TILE, LANES), F32)], axis=0)
    return cos_t, sin_t


def _block_ones(seg):
    i = np.arange(LANES)
    return jnp.asarray((i[:, None] // seg == i[None, :] // seg).astype(np.float32), dtype=BF16)


def _lane_tile(v):
    return jnp.tile(v.astype(F32), LANES // v.shape[0])


def kernel(x, c, ctx, c_ctx, norm1_g, norm2_g, w_ada, b_ada, w_in, b_gate, a_qn, a_kn, b_qn, b_kn, lam_q1, lam_k1, lam_q2, lam_k2, subln_g, c_qn, c_kn, sink, d_qn, d_kn, rpb, w_branch, w_out, router_w, router_b, w_gate_up, b_gate_up, w_down, b_down):
    B, S, D = x.shape
    C = ctx.shape[1]
    L = w_in.shape[0]
    assert D == D_MODEL and C == ROW_TILE and S % ROW_TILE == 0 and B + 1 <= 8
    n_lat = B * S
    per = S // ROW_TILE
    nlat_tiles = B * per

    def row_batch(t):
        return jnp.where(t < nlat_tiles, t // per, B)

    cvec = jnp.zeros((8, D), F32).at[:B].set(c).at[B].set(c_ctx)
    mod_all = _ada(cvec, w_ada, b_ada)

    tables = _rope_tables(S, HEAD_DIM) + _rope_tables(S, B_DK)
    ones64, ones32 = _block_ones(HEAD_DIM), _block_ones(B_DK)

    E = w_gate_up.shape[1]
    w_gu_all = w_gate_up.reshape(L * E, D, 2 * D_FF)
    w_dn_all = w_down.reshape(L * E, D_FF, D)

    X = jnp.concatenate([x.reshape(n_lat, D), ctx.reshape(B * C, D)], axis=0)
    for l in range(L):
        last = l == L - 1
        lam_init = 0.8 - 0.6 * math.exp(-0.3 * l)
        mod = mod_all[l].reshape(8, 1, N_MOD * D)
        qkv, gate = _in_proj(X, norm1_g[l], mod, w_in[l].astype(BF16), b_gate[l], row_batch)
        gains = jnp.stack([_lane_tile(g[l]) for g in (a_qn, a_kn, b_qn, b_kn, c_qn, c_kn, d_qn, d_kn)])
        qTs, ks, vTs = _qkv_post(qkv, tables, ones64, ones32, gains, B, S, C)

        lamvec = jnp.zeros((8, LANES), F32)
        for r, v in enumerate((lam_q1, lam_k1, lam_q2, lam_k2)):
            lamvec = lamvec.at[r, :B_DK].set(v[l])
        subg = subln_g[l].reshape(HEAD_DIM, 1)
        sink_l = sink[l].astype(F32)
        nbr_table = _nbr_bias_table(rpb[l])

        o_a = _flash(qTs[0], ks[0], vTs[0], mode="gqa", S=S, C=C, ctx_only=False)
        o_b = _flash(qTs[1], ks[1], vTs[1], mode="diff", S=S, C=C, ctx_only=False,
                     lamvec=lamvec, subg=subg, lam_init=lam_init)
        o_c = _window(qTs[2], ks[2], vTs[2], sink_l, S, C)
        o_d = _nbr(qTs[3], ks[3], vTs[3], nbr_table, S, C)
        if not last:
            o_a = _flash(qTs[0], ks[0], vTs[0], mode="gqa", S=S, C=C, ctx_only=True, out_prev=o_a)
            o_b = _flash(qTs[1], ks[1], vTs[1], mode="diff", S=S, C=C, ctx_only=True,
                         lamvec=lamvec, subg=subg, lam_init=lam_init, out_prev=o_b)
            o_c = _flash(qTs[2], ks[2], vTs[2], mode="gqa", S=S, C=C, ctx_only=True, sink=sink_l, out_prev=o_c)
            o_d = _flash(qTs[3], ks[3], vTs[3], mode="mha", S=S, C=C, ctx_only=True, out_prev=o_d)

        n_rows = n_lat if last else X.shape[0]
        rw_pad = jnp.zeros((D, LANES), F32).at[:, :N_EXPERTS].set(router_w[l])
        rb_pad = jnp.full((1, LANES), NEG_INF, F32).at[0, :N_EXPERTS].set(router_b[l])
        Xm, h2, route, counts = _merge((o_a, o_b, o_c, o_d), gate, X, mod, w_branch[l].astype(BF16),
                                       w_out[l].astype(BF16), norm2_g[l], rw_pad, rb_pad, n_rows, B, S, C,
                                       row_batch)
        X = _moe(h2, route, counts, Xm, mod, row_batch, l, w_gu_all, b_gate_up.reshape(L * E, -1),
                 w_dn_all, b_down.reshape(L * E, -1))
    return X[:n_lat].reshape(B, S, D)
```

```python
import functools
import math

import numpy as np
import jax
import jax.numpy as jnp
from jax import lax
from jax.experimental import pallas as pl
from jax.experimental.pallas import tpu as pltpu

F32 = jnp.float32
BF16 = jnp.bfloat16

D_MODEL = 1024
GRID_W = 64
HEAD_DIM = 64
B_DK = 32
WINDOW = 128
NA_ROWS = 8
NA_COLS = 16
ROPE_THETA = 10000.0
N_EXPERTS = 32
TOP_K = 4
D_FF = D_MODEL
SWIGLU_LIMIT = 7.0
SWIGLU_ALPHA = 1.702
N_MOD = 6
EPS = 1e-6
NEG_INF = -1e30
LOG2E = 1.4426950408889634

Q_TOT = 1024
KV_TOT = 1536
QKV_TOT = Q_TOT + KV_TOT
GATE_TOT = 4 * D_MODEL

ROW_TILE = 256
LANES = 128
EXPERT_ROWS = 256
NBR_WIN_ROWS = 10
NBR_TABLE_ROWS = NBR_WIN_ROWS + 8
FLASH_KEYS = 1408
SUM_ROWS = 16
VMEM_LIMIT = 52 * 1024 * 1024


def _cparams(sem):
    return pltpu.CompilerParams(dimension_semantics=sem, vmem_limit_bytes=VMEM_LIMIT)


def _dot(a, b):
    return jnp.dot(a, b, preferred_element_type=F32)


def _ada_kernel(c_ref, w_ref, b_ref, o_ref):
    c = c_ref[...]
    s = c * jax.nn.sigmoid(c)
    o_ref[0] = jnp.dot(s, w_ref[0], precision=lax.Precision.HIGHEST,
                       preferred_element_type=F32) + b_ref[0]


def _ada(cvec, w_ada, b_ada):
    L = w_ada.shape[0]
    n_out = w_ada.shape[2]
    tn = 1536
    return pl.pallas_call(
        _ada_kernel,
        grid=(L, n_out // tn),
        in_specs=[
            pl.BlockSpec((8, D_MODEL), lambda l, j: (0, 0)),
            pl.BlockSpec((1, D_MODEL, tn), lambda l, j: (l, 0, j)),
            pl.BlockSpec((1, 1, tn), lambda l, j: (l, 0, j)),
        ],
        out_specs=pl.BlockSpec((1, 8, tn), lambda l, j: (l, 0, j)),
        out_shape=jax.ShapeDtypeStruct((L, 8, n_out), F32),
        compiler_params=_cparams(("arbitrary", "arbitrary")),
        name="ada_mod",
    )(cvec, w_ada, b_ada.reshape(L, 1, n_out))


def _in_kernel(x_ref, g_ref, mod_ref, w_ref, bg_ref, qkv_ref, gate_ref, *, tn):
    x = x_ref[...]
    ms = jnp.mean(x * x, axis=-1, keepdims=True)
    y = x * lax.rsqrt(ms + EPS) * g_ref[...]
    mod = mod_ref[0]
    sh = mod[:, 0:D_MODEL]
    sc = mod[:, D_MODEL:2 * D_MODEL]
    h = (y * (1 + sc) + sh).astype(BF16)
    for j in range(QKV_TOT // tn):
        qkv_ref[:, j * tn:(j + 1) * tn] = _dot(h, w_ref[:, j * tn:(j + 1) * tn]).astype(BF16)
    for j in range(GATE_TOT // tn):
        g = _dot(h, w_ref[:, QKV_TOT + j * tn:QKV_TOT + (j + 1) * tn]) + bg_ref[:, j * tn:(j + 1) * tn]
        gate_ref[:, j * tn:(j + 1) * tn] = jax.nn.sigmoid(g).astype(BF16)


def _in_proj(X, norm_g, mod, w_in_bf, b_gate, row_batch):
    R = X.shape[0]
    nt = R // ROW_TILE
    return pl.pallas_call(
        functools.partial(_in_kernel, tn=512),
        grid=(nt,),
        in_specs=[
            pl.BlockSpec((ROW_TILE, D_MODEL), lambda t: (t, 0)),
            pl.BlockSpec((1, D_MODEL), lambda t: (0, 0)),
            pl.BlockSpec((1, 1, N_MOD * D_MODEL), lambda t: (row_batch(t), 0, 0)),
            pl.BlockSpec((D_MODEL, QKV_TOT + GATE_TOT), lambda t: (0, 0), pipeline_mode=pl.Buffered(1)),
            pl.BlockSpec((1, GATE_TOT), lambda t: (0, 0)),
        ],
        out_specs=[
            pl.BlockSpec((ROW_TILE, QKV_TOT), lambda t: (t, 0)),
            pl.BlockSpec((ROW_TILE, GATE_TOT), lambda t: (t, 0)),
        ],
        out_shape=[
            jax.ShapeDtypeStruct((R, QKV_TOT), BF16),
            jax.ShapeDtypeStruct((R, GATE_TOT), BF16),
        ],
        compiler_params=_cparams(("arbitrary",)),
        name="in_proj",
    )(X, norm_g.reshape(1, D_MODEL), mod, w_in_bf, b_gate.reshape(1, GATE_TOT))


def _seg_rms(x, ones, seg, g):
    sq = x * x
    hi = sq.astype(BF16)
    lo = (sq - hi.astype(F32)).astype(BF16)
    ss = _dot(hi, ones) + _dot(lo, ones)
    return x * lax.rsqrt(ss * (1.0 / seg) + EPS) * g


def _rot_half(y, half):
    lane = lax.broadcasted_iota(jnp.int32, y.shape, 1)
    fwd = pltpu.roll(y, LANES - half, axis=1)
    bwd = pltpu.roll(y, half, axis=1)
    return jnp.where((lane % (2 * half)) < half, fwd, bwd)


def _qkv_post_kernel(qkv_ref, c64_ref, s64_ref, c32_ref, s32_ref, ones64_ref, ones32_ref, gains_ref,
                     qa_ref, qb_ref, qc_ref, qd_ref, ka_ref, kb_ref, kc_ref, kd_ref,
                     va_ref, vb_ref, vc_ref, vd_ref):
    gains = gains_ref[...]
    rope = {64: (c64_ref[...], s64_ref[...]), 32: (c32_ref[...], s32_ref[...])}
    ones = {64: ones64_ref[...], 32: ones32_ref[...]}

    def chunk(col):
        return qkv_ref[:, col:col + LANES].astype(F32)

    def normed(col, seg, gain_row, use_rope, scale):
        y = _seg_rms(chunk(col), ones[seg], seg, gains[gain_row:gain_row + 1, :])
        if use_rope:
            cos, sin = rope[seg]
            y = y * cos + _rot_half(y, seg // 2) * sin
        return y * scale if scale != 1.0 else y

    q_refs = (qa_ref, qb_ref, qc_ref, qd_ref)
    q_seg = (64, 32, 64, 64)
    q_rope = (True, True, True, False)
    for m in range(4):
        scale = float(q_seg[m]) ** -0.5 * LOG2E
        for c in range(2):
            y = normed(m * 256 + c * LANES, q_seg[m], 2 * m, q_rope[m], scale)
            q_refs[m][0, c * LANES:(c + 1) * LANES, :] = y.T.astype(BF16)

    k_refs = (ka_ref, kb_ref, kc_ref, kd_ref)
    v_refs = (va_ref, vb_ref, vc_ref, vd_ref)
    widths = (128, 256, 128, 256)
    kcol = Q_TOT
    vcol = Q_TOT + sum(widths)
    for m in range(4):
        for c in range(widths[m] // LANES):
            y = normed(kcol, q_seg[m], 2 * m + 1, q_rope[m], 1.0)
            k_refs[m][0, :, c * LANES:(c + 1) * LANES] = y.astype(BF16)
            v_refs[m][0, c * LANES:(c + 1) * LANES, :] = chunk(vcol).T.astype(BF16)
            kcol += LANES
            vcol += LANES


def _qkv_post(qkv, tables, ones64, ones32, gains, B, S, C):
    R = qkv.shape[0]
    nt = R // ROW_TILE
    per = S // ROW_TILE
    nlat = B * per
    T = S + C

    def bidx(t):
        return jnp.where(t < nlat, t // per, t - nlat)

    def pidx(t):
        return jnp.where(t < nlat, t % per, per)

    tab_spec = pl.BlockSpec((ROW_TILE, LANES), lambda t: (pidx(t), 0))
    const_spec = pl.BlockSpec((LANES, LANES), lambda t: (0, 0))
    widths = (128, 256, 128, 256)
    q_specs = [pl.BlockSpec((1, 256, ROW_TILE), lambda t: (bidx(t), 0, pidx(t))) for _ in range(4)]
    k_specs = [pl.BlockSpec((1, ROW_TILE, w), lambda t: (bidx(t), pidx(t), 0)) for w in widths]
    v_specs = [pl.BlockSpec((1, w, ROW_TILE), lambda t: (bidx(t), 0, pidx(t))) for w in widths]
    q_shapes = [jax.ShapeDtypeStruct((B, 256, T), BF16) for _ in range(4)]
    k_shapes = [jax.ShapeDtypeStruct((B, T, w), BF16) for w in widths]
    v_shapes = [jax.ShapeDtypeStruct((B, w, T), BF16) for w in widths]
    outs = pl.pallas_call(
        _qkv_post_kernel,
        grid=(nt,),
        in_specs=[pl.BlockSpec((ROW_TILE, QKV_TOT), lambda t: (t, 0)),
                  tab_spec, tab_spec, tab_spec, tab_spec, const_spec, const_spec,
                  pl.BlockSpec((8, LANES), lambda t: (0, 0))],
        out_specs=q_specs + k_specs + v_specs,
        out_shape=q_shapes + k_shapes + v_shapes,
        compiler_params=_cparams(("arbitrary",)),
        name="qkv_post",
    )(qkv, *tables, ones64, ones32, gains)
    return outs[0:4], outs[4:8], outs[8:12]


def _pad_queries(q, pieces, kw, unit):
    tq = q.shape[1]
    r = lax.broadcasted_iota(jnp.int32, (kw, tq), 0)
    blocks = []
    for row0, size, extra in pieces:
        tiled = jnp.concatenate([q[row0:row0 + size, :]] * (kw // size), axis=0)
        off = unit * HEAD_DIM + extra
        blocks.append(jnp.where((r >= off) & (r < off + size), tiled, 0.0))
    out = blocks[0] if len(blocks) == 1 else jnp.concatenate(blocks, axis=1)
    return out.astype(BF16)


_PIECES = {
    "gqa": ((0, 64, 0), (64, 64, 0)),
    "diff": ((0, 32, 0), (32, 32, 32)),
    "mha": ((0, 64, 0),),
}


def _flash_kernel(*refs, mode, kw, tq, tk, kv_start, n_chunks, use_sink, lam_init, aliased):
    refs = list(refs)
    qT_ref, k_ref, vT_ref = refs[0:3]
    pos = 3
    sink_ref = None
    if use_sink:
        sink_ref = refs[pos]
        pos += 1
    if mode == "diff":
        lamvec_ref, subg_ref = refs[pos:pos + 2]
        pos += 2
    if aliased:
        pos += 1
    o_ref = refs[pos]
    s_a, s_b = refs[pos + 1:pos + 3]

    unit = pl.program_id(1)
    pieces = _PIECES[mode]
    ng = len(pieces)
    n = ng * tq
    qpad = _pad_queries(qT_ref[0].astype(F32), pieces, kw, unit)

    if use_sink:
        m0 = jnp.concatenate([jnp.full((1, tq), sink_ref[ng * unit + g] * LOG2E, F32) for g in range(ng)],
                             axis=1)
        l0 = jnp.ones((1, n), F32)
    else:
        m0 = jnp.full((1, n), NEG_INF, F32)
        l0 = jnp.zeros((1, n), F32)
    acc0 = jnp.zeros((HEAD_DIM, n), F32)
    ones_rows = jnp.ones((SUM_ROWS, tk), BF16)

    def chunk_start(j):
        return pl.multiple_of(kv_start + j * tk, LANES)

    def scores(j, s_ref):
        s = _dot(k_ref[0, pl.ds(chunk_start(j), tk), :], qpad)
        s_ref[...] = s
        return jnp.max(s, axis=0, keepdims=True)

    def absorb(j, s_ref, mc, state):
        m, l, acc = state
        m_new = jnp.maximum(m, mc)
        alpha = jnp.exp2(m - m_new)
        p = jnp.exp2(s_ref[...] - m_new)
        v_aug = jnp.concatenate([vT_ref[0, :, pl.ds(chunk_start(j), tk)], ones_rows], axis=0)
        r = _dot(v_aug, p.astype(BF16))
        return m_new, alpha * l + r[HEAD_DIM:HEAD_DIM + 1], alpha * acc + r[0:HEAD_DIM]

    mc_a = scores(0, s_a)
    n_pairs = (n_chunks - 1) // 2

    def pair(i, carry):
        mc_a, state = carry[0], carry[1:]
        mc_b = scores(2 * i + 1, s_b)
        state = absorb(2 * i, s_a, mc_a, state)
        mc_a = scores(2 * i + 2, s_a)
        state = absorb(2 * i + 1, s_b, mc_b, state)
        return (mc_a,) + tuple(state)

    carry = (mc_a, m0, l0, acc0)
    if n_pairs > 0:
        carry = lax.fori_loop(0, n_pairs, pair, carry)
    mc_a, state = carry[0], carry[1:]
    if n_chunks % 2 == 0:
        mc_b = scores(n_chunks - 1, s_b)
        state = absorb(n_chunks - 2, s_a, mc_a, state)
        state = absorb(n_chunks - 1, s_b, mc_b, state)
    else:
        state = absorb(n_chunks - 1, s_a, mc_a, state)
    m, l, acc = state
    o = acc / l

    if mode == "gqa":
        for g in range(ng):
            o_ref[0, g * HEAD_DIM:(g + 1) * HEAD_DIM, :] = o[:, g * tq:(g + 1) * tq].astype(o_ref.dtype)
    elif mode == "mha":
        o_ref[0] = o.astype(o_ref.dtype)
    else:
        lv = lamvec_ref[...]
        lam = (jnp.exp(jnp.sum(lv[0:1] * lv[1:2], axis=1, keepdims=True))
               - jnp.exp(jnp.sum(lv[2:3] * lv[3:4], axis=1, keepdims=True)) + lam_init)
        d = o[:, 0:tq] - lam * o[:, tq:2 * tq]
        ms = jnp.mean(d * d, axis=0, keepdims=True)
        o_ref[0] = (d * lax.rsqrt(ms + EPS) * subg_ref[...] * (1.0 - lam_init)).astype(o_ref.dtype)


def _flash(qT, k, vT, *, mode, S, C, ctx_only, sink=None, lamvec=None, subg=None, lam_init=0.0, out_prev=None):
    B, _, T = qT.shape
    kw = k.shape[2]
    q_rows = 128 if mode == "gqa" else 64
    units = 256 // q_rows
    if ctx_only:
        tq, nq, q_blk0 = C, 1, S // C
        tk, kv_start, n_chunks = C, S, 1
    else:
        tq, nq, q_blk0 = 256, S // 256, 0
        tk = FLASH_KEYS if T % FLASH_KEYS == 0 else 256
        kv_start, n_chunks = 0, T // tk
    in_specs = [
        pl.BlockSpec((1, q_rows, tq), lambda b, u, i: (b, u, i + q_blk0)),
        pl.BlockSpec((1, T, kw), lambda b, u, i: (b, 0, 0)),
        pl.BlockSpec((1, HEAD_DIM, T), lambda b, u, i: (b, u, 0)),
    ]
    args = [qT, k, vT]
    if sink is not None:
        in_specs.append(pl.BlockSpec(memory_space=pltpu.SMEM))
        args.append(sink)
    if mode == "diff":
        in_specs += [pl.BlockSpec((8, LANES), lambda b, u, i: (0, 0)),
                     pl.BlockSpec((HEAD_DIM, 1), lambda b, u, i: (0, 0))]
        args += [lamvec, subg]
    aliases = {}
    if out_prev is not None:
        in_specs.append(pl.BlockSpec(memory_space=pl.ANY))
        aliases = {len(args): 0}
        args.append(out_prev)
    kern = functools.partial(_flash_kernel, mode=mode, kw=kw, tq=tq, tk=tk, kv_start=kv_start,
                             n_chunks=n_chunks, use_sink=sink is not None, lam_init=lam_init,
                             aliased=out_prev is not None)
    return pl.pallas_call(
        kern,
        grid=(B, units, nq),
        in_specs=in_specs,
        out_specs=pl.BlockSpec((1, q_rows, tq), lambda b, u, i: (b, u, i + q_blk0)),
        out_shape=jax.ShapeDtypeStruct((B, 256, T), BF16),
        input_output_aliases=aliases,
        scratch_shapes=[pltpu.VMEM((tk, len(_PIECES[mode]) * tq), F32)] * 2,
        compiler_params=_cparams(("arbitrary", "arbitrary", "arbitrary")),
        name="flash_" + mode + ("_ctx" if ctx_only else ""),
    )(*args)


WINDOW_BLOCKS = 2


def _window_kernel(qT_ref, k_ref, vT_ref, sink_ref, o_ref, *, S, C):
    step = pl.program_id(1)
    qb = WINDOW
    span = 3 * qb
    n = 2 * qb
    k_ctx = k_ref[0, S:S + C, :]
    for bi in range(WINDOW_BLOCKS):
        i = step * WINDOW_BLOCKS + bi
        cols = slice(bi * qb, (bi + 1) * qb)
        start = pl.multiple_of(jnp.clip((i - 1) * qb, 0, S - span), LANES)
        k_loc = k_ref[0, pl.ds(start, span), :]
        kpos = start + lax.broadcasted_iota(jnp.int32, (span, n), 0)
        qpos = i * qb + lax.broadcasted_iota(jnp.int32, (span, n), 1) % qb
        in_window = jnp.abs(kpos - qpos) <= WINDOW
        for unit in range(2):
            q = qT_ref[0, unit * 128:(unit + 1) * 128, cols].astype(F32)
            qpad = _pad_queries(q, _PIECES["gqa"], LANES, unit)
            s_loc = jnp.where(in_window, _dot(k_loc, qpad), NEG_INF)
            s_ctx = _dot(k_ctx, qpad)
            sink = jnp.concatenate(
                [jnp.full((1, qb), sink_ref[2 * unit + g] * LOG2E, F32) for g in range(2)], axis=1)
            m = jnp.maximum(jnp.maximum(jnp.max(s_loc, axis=0, keepdims=True),
                                        jnp.max(s_ctx, axis=0, keepdims=True)), sink)
            e_loc = jnp.exp2(s_loc - m)
            e_ctx = jnp.exp2(s_ctx - m)
            den = (jnp.sum(e_loc, axis=0, keepdims=True) + jnp.sum(e_ctx, axis=0, keepdims=True)
                   + jnp.exp2(sink - m))
            v_rows = slice(unit * HEAD_DIM, (unit + 1) * HEAD_DIM)
            o = (_dot(vT_ref[0, v_rows, pl.ds(start, span)], e_loc.astype(BF16))
                 + _dot(vT_ref[0, v_rows, S:S + C], e_ctx.astype(BF16))) / den
            for g in range(2):
                head = 2 * unit + g
                o_ref[0, head * HEAD_DIM:(head + 1) * HEAD_DIM, cols] = (
                    o[:, g * qb:(g + 1) * qb].astype(o_ref.dtype))


def _window(qT, k, vT, sink, S, C):
    B, _, T = qT.shape
    tq = WINDOW * WINDOW_BLOCKS
    return pl.pallas_call(
        functools.partial(_window_kernel, S=S, C=C),
        grid=(B, S // tq),
        in_specs=[
            pl.BlockSpec((1, 256, tq), lambda b, i: (b, 0, i)),
            pl.BlockSpec((1, T, LANES), lambda b, i: (b, 0, 0)),
            pl.BlockSpec((1, 2 * HEAD_DIM, T), lambda b, i: (b, 0, 0)),
            pl.BlockSpec(memory_space=pltpu.SMEM),
        ],
        out_specs=pl.BlockSpec((1, 256, tq), lambda b, i: (b, 0, i)),
        out_shape=jax.ShapeDtypeStruct((B, 256, T), BF16),
        compiler_params=_cparams(("arbitrary", "arbitrary")),
        name="window_attn",
    )(qT, k, vT, sink)


def _nbr_kernel(qT_ref, k_ref, vT_ref, tab_ref, o_ref, *, S, C):
    j = pl.program_id(1)
    rows = S // GRID_W
    n = 2 * GRID_W
    span = NBR_WIN_ROWS * GRID_W
    w0 = jnp.clip(2 * j - NA_ROWS // 2, 0, rows - NBR_WIN_ROWS)
    shift = w0 - 2 * j + NA_ROWS // 2 + 4
    start = pl.multiple_of(w0 * GRID_W, LANES)
    k_loc = k_ref[0, pl.ds(start, span), :]
    k_ctx = k_ref[0, S:S + C, :]
    kr = w0 + lax.broadcasted_iota(jnp.int32, (span, n), 0) // GRID_W
    qr = 2 * j + lax.broadcasted_iota(jnp.int32, (span, n), 1) // GRID_W
    r0 = jnp.clip(qr - NA_ROWS // 2, 0, rows - NA_ROWS)
    in_rows = (kr >= r0) & (kr < r0 + NA_ROWS)
    for head in range(4):
        h_rows = slice(head * HEAD_DIM, (head + 1) * HEAD_DIM)
        qpad = _pad_queries(qT_ref[0, h_rows, :].astype(F32), _PIECES["mha"], 256, head)
        bias = tab_ref[head, pl.ds(pl.multiple_of(shift * GRID_W, GRID_W), span), :]
        s_loc = jnp.where(in_rows, _dot(k_loc, qpad) + bias, NEG_INF)
        s_ctx = _dot(k_ctx, qpad)
        m = jnp.maximum(jnp.max(s_loc, axis=0, keepdims=True), jnp.max(s_ctx, axis=0, keepdims=True))
        e_loc = jnp.exp2(s_loc - m)
        e_ctx = jnp.exp2(s_ctx - m)
        den = jnp.sum(e_loc, axis=0, keepdims=True) + jnp.sum(e_ctx, axis=0, keepdims=True)
        o = (_dot(vT_ref[0, h_rows, pl.ds(start, span)], e_loc.astype(BF16))
             + _dot(vT_ref[0, h_rows, S:S + C], e_ctx.astype(BF16))) / den
        o_ref[0, h_rows, :] = o.astype(o_ref.dtype)


def _nbr_bias_table(rpb):
    u = np.arange(NBR_TABLE_ROWS)[:, None, None, None]
    kc = np.arange(GRID_W)[None, :, None, None]
    e = np.arange(2)[None, None, :, None]
    qc = np.arange(GRID_W)[None, None, None, :]
    dr = u - e - 1
    row_ok = (dr >= 0) & (dr < 2 * NA_ROWS - 1)
    dc = np.clip(kc - qc, -(NA_COLS - 1), NA_COLS - 1) + (NA_COLS - 1)
    c0 = np.clip(qc - NA_COLS // 2, 0, GRID_W - NA_COLS)
    col_ok = (kc >= c0) & (kc < c0 + NA_COLS)
    shape = (NBR_TABLE_ROWS, GRID_W, 2, GRID_W)
    n_dr, n_dc = 2 * NA_ROWS - 1, 2 * NA_COLS - 1
    sel_r = ((dr[:, 0, :, 0, None] == np.arange(n_dr)) & row_ok[:, 0, :, 0, None]).astype(np.float32)
    sel_c = (dc[0, :, 0, :, None] == np.arange(n_dc)).astype(np.float32)
    vals = jnp.einsum("uer,hrc,kqc->hukeq", sel_r, rpb.astype(F32) * LOG2E, sel_c,
                      precision=lax.Precision.HIGHEST)
    vals = jnp.where(np.broadcast_to(col_ok, shape), vals, NEG_INF)
    return vals.reshape(rpb.shape[0], NBR_TABLE_ROWS * GRID_W, 2 * GRID_W)


def _nbr(qT, k, vT, table, S, C):
    B, _, T = qT.shape
    return pl.pallas_call(
        functools.partial(_nbr_kernel, S=S, C=C),
        grid=(B, S // (2 * GRID_W)),
        in_specs=[
            pl.BlockSpec((1, 256, 2 * GRID_W), lambda b, j: (b, 0, j)),
            pl.BlockSpec((1, T, 256), lambda b, j: (b, 0, 0)),
            pl.BlockSpec((1, 256, T), lambda b, j: (b, 0, 0)),
            pl.BlockSpec((4, NBR_TABLE_ROWS * GRID_W, 2 * GRID_W), lambda b, j: (0, 0, 0)),
        ],
        out_specs=pl.BlockSpec((1, 256, 2 * GRID_W), lambda b, j: (b, 0, j)),
        out_shape=jax.ShapeDtypeStruct((B, 256, T), BF16),
        compiler_params=_cparams(("arbitrary", "arbitrary")),
        name="nbr_attn",
    )(qT, k, vT, table)


ROUTE_W, ROUTE_E, ROUTE_RANK = 0, 4, 8
ROUTE_ROWS = 16


def _merge_kernel(oa_ref, ob_ref, oc_ref, od_ref, gate_ref, x_ref, mod_ref, wb_ref, wo_ref, n2_ref,
                  rw_ref, rb_ref, tri_ref, xo_ref, h_ref, route_ref, cnt_ref, base_ref):
    @pl.when(pl.program_id(0) == 0)
    def _():
        base_ref[...] = jnp.zeros_like(base_ref)

    acc = None
    for nbr, o_ref in enumerate((oa_ref, ob_ref, oc_ref, od_ref)):
        proj = lax.dot_general(o_ref[0], wb_ref[nbr], (((0,), (0,)), ((), ())),
                               preferred_element_type=F32)
        term = gate_ref[:, nbr * D_MODEL:(nbr + 1) * D_MODEL].astype(F32) * proj
        acc = term if acc is None else acc + term
    mix = _dot(acc.astype(BF16), wo_ref[...])
    mod = mod_ref[0]
    g1 = mod[:, 2 * D_MODEL:3 * D_MODEL]
    sh2 = mod[:, 3 * D_MODEL:4 * D_MODEL]
    sc2 = mod[:, 4 * D_MODEL:5 * D_MODEL]
    xn = x_ref[...] + g1 * mix
    xo_ref[...] = xn
    ms = jnp.mean(xn * xn, axis=-1, keepdims=True)
    h = (xn * lax.rsqrt(ms + EPS) * n2_ref[...]) * (1 + sc2) + sh2
    h_ref[...] = h.astype(BF16)
    logits = lax.dot_general(rw_ref[...], h, (((1,), (1,)), ((), ())), precision=lax.Precision.HIGHEST,
                             preferred_element_type=F32) + rb_ref[...]
    row_f = lax.broadcasted_iota(jnp.int32, logits.shape, 0).astype(F32)
    work = logits
    picks = []
    for _ in range(TOP_K):
        top = jnp.max(work, axis=0, keepdims=True)
        idx = jnp.min(jnp.where(work == top, row_f, float(N_EXPERTS)), axis=0, keepdims=True)
        hit = row_f == idx
        picks.append((top, idx, hit))
        work = jnp.where(hit, -jnp.inf, work)
    ex = [jnp.exp(top - picks[0][0]) for top, _, _ in picks]
    den = ex[0] + ex[1] + ex[2] + ex[3]
    chosen = jnp.zeros(logits.shape, F32)
    for _, _, hit in picks:
        chosen = chosen + hit.astype(F32)
    base = base_ref[...][:, 0:1]
    before = _dot(chosen.astype(BF16), tri_ref[...]) + base
    fields = ([ex[k] / den for k in range(TOP_K)] + [idx for _, idx, _ in picks]
              + [jnp.sum(jnp.where(hit, before, 0.0), axis=0, keepdims=True) for _, _, hit in picks])
    fields.append(jnp.zeros((ROUTE_ROWS - len(fields), logits.shape[1]), F32))
    route_ref[...] = jnp.concatenate(fields, axis=0)
    base_ref[...] = base_ref[...] + jnp.sum(chosen, axis=1, keepdims=True)
    cnt_ref[...] = base_ref[...]


def _merge(oTs, gate, X, mod, wb_bf, wo_bf, norm2_g, rw_t, rb_col, n_rows, B, S, C, row_batch):
    nt = n_rows // ROW_TILE
    per = S // ROW_TILE
    nlat = B * per

    def bidx(t):
        return jnp.where(t < nlat, t // per, t - nlat)

    def pidx(t):
        return jnp.where(t < nlat, t % per, per)

    o_spec = pl.BlockSpec((1, 256, ROW_TILE), lambda t: (bidx(t), 0, pidx(t)))
    r = np.arange(ROW_TILE)
    strict_upper = jnp.asarray((r[:, None] < r[None, :]).astype(np.float32), dtype=BF16)
    return pl.pallas_call(
        _merge_kernel,
        grid=(nt,),
        in_specs=[o_spec, o_spec, o_spec, o_spec,
                  pl.BlockSpec((ROW_TILE, GATE_TOT), lambda t: (t, 0)),
                  pl.BlockSpec((ROW_TILE, D_MODEL), lambda t: (t, 0)),
                  pl.BlockSpec((1, 1, N_MOD * D_MODEL), lambda t: (row_batch(t), 0, 0)),
                  pl.BlockSpec((4, 256, D_MODEL), lambda t: (0, 0, 0)),
                  pl.BlockSpec((D_MODEL, D_MODEL), lambda t: (0, 0)),
                  pl.BlockSpec((1, D_MODEL), lambda t: (0, 0)),
                  pl.BlockSpec((N_EXPERTS, D_MODEL), lambda t: (0, 0)),
                  pl.BlockSpec((N_EXPERTS, 1), lambda t: (0, 0)),
                  pl.BlockSpec((ROW_TILE, ROW_TILE), lambda t: (0, 0))],
        out_specs=[pl.BlockSpec((ROW_TILE, D_MODEL), lambda t: (t, 0)),
                   pl.BlockSpec((ROW_TILE, D_MODEL), lambda t: (t, 0)),
                   pl.BlockSpec((ROUTE_ROWS, ROW_TILE), lambda t: (0, t)),
                   pl.BlockSpec((N_EXPERTS, LANES), lambda t: (0, 0))],
        out_shape=[jax.ShapeDtypeStruct((n_rows, D_MODEL), F32),
                   jax.ShapeDtypeStruct((n_rows, D_MODEL), BF16),
                   jax.ShapeDtypeStruct((ROUTE_ROWS, n_rows), F32),
                   jax.ShapeDtypeStruct((N_EXPERTS, LANES), F32)],
        scratch_shapes=[pltpu.VMEM((N_EXPERTS, LANES), F32)],
        compiler_params=_cparams(("arbitrary",)),
        name="merge",
    )(*oTs, gate, X, mod, wb_bf, wo_bf, norm2_g.reshape(1, D_MODEL), rw_t, rb_col, strict_upper)


def _expert_kernel(be_ref, nu_ref, x_ref, wgu_ref, bgu_ref, wd_ref, bd_ref, y_ref, wgu_s, wd_s):
    i = pl.program_id(0)

    @pl.when(i < nu_ref[0])
    def _():
        prev = be_ref[jnp.maximum(i - 1, 0)]

        @pl.when((i == 0) | (be_ref[i] != prev))
        def _():
            wgu_s[...] = wgu_ref[0].astype(BF16)
            wd_s[...] = wd_ref[0].astype(BF16)

        gu = _dot(x_ref[...], wgu_s[...]) + bgu_ref[0]
        gate = jnp.minimum(gu[:, :D_FF], SWIGLU_LIMIT)
        up = jnp.clip(gu[:, D_FF:], -SWIGLU_LIMIT, SWIGLU_LIMIT)
        a = gate * jax.nn.sigmoid(SWIGLU_ALPHA * gate) * (up + 1)
        y_ref[...] = (_dot(a.astype(BF16), wd_s[...]) + bd_ref[0]).astype(y_ref.dtype)


def _experts(xbuf, blk_e, n_used, w_gate_up, b_gate_up, w_down, b_down):
    P = xbuf.shape[0]
    n_blk = P // EXPERT_ROWS
    E = w_gate_up.shape[0]

    def row_map(i, be, nu):
        return (jnp.maximum(jnp.minimum(i, nu[0] - 1), 0), 0)

    def w_map(i, be, nu):
        return (be[i], 0, 0)

    grid_spec = pltpu.PrefetchScalarGridSpec(
        num_scalar_prefetch=2,
        grid=(n_blk,),
        in_specs=[
            pl.BlockSpec((EXPERT_ROWS, D_MODEL), row_map),
            pl.BlockSpec((1, D_MODEL, 2 * D_FF), w_map),
            pl.BlockSpec((1, 1, 2 * D_FF), w_map),
            pl.BlockSpec((1, D_FF, D_MODEL), w_map),
            pl.BlockSpec((1, 1, D_MODEL), w_map),
        ],
        out_specs=pl.BlockSpec((EXPERT_ROWS, D_MODEL), row_map),
        scratch_shapes=[pltpu.VMEM((D_MODEL, 2 * D_FF), BF16), pltpu.VMEM((D_FF, D_MODEL), BF16)],
    )
    return pl.pallas_call(
        _expert_kernel,
        grid_spec=grid_spec,
        out_shape=jax.ShapeDtypeStruct((P, D_MODEL), BF16),
        compiler_params=_cparams(("arbitrary",)),
        name="experts",
    )(blk_e, n_used, xbuf, w_gate_up, b_gate_up.reshape(E, 1, 2 * D_FF), w_down, b_down.reshape(E, 1, D_MODEL))


def _combine_kernel(yg_ref, route_ref, x_ref, mod_ref, o_ref):
    route = route_ref[...]
    y = None
    for k in range(TOP_K):
        term = route[:, ROUTE_W + k:ROUTE_W + k + 1] * yg_ref[k].astype(F32)
        y = term if y is None else y + term
    g2 = mod_ref[0][:, 5 * D_MODEL:6 * D_MODEL]
    o_ref[...] = x_ref[...] + g2 * y


def _combine(yg, route, X, mod, row_batch):
    n_rows = X.shape[0]
    return pl.pallas_call(
        _combine_kernel,
        grid=(n_rows // ROW_TILE,),
        in_specs=[pl.BlockSpec((TOP_K, ROW_TILE, D_MODEL), lambda t: (0, t, 0)),
                  pl.BlockSpec((ROW_TILE, ROUTE_ROWS), lambda t: (t, 0)),
                  pl.BlockSpec((ROW_TILE, D_MODEL), lambda t: (t, 0)),
                  pl.BlockSpec((1, 1, N_MOD * D_MODEL), lambda t: (row_batch(t), 0, 0))],
        out_specs=pl.BlockSpec((ROW_TILE, D_MODEL), lambda t: (t, 0)),
        out_shape=jax.ShapeDtypeStruct((n_rows, D_MODEL), F32),
        compiler_params=_cparams(("arbitrary",)),
        name="moe_combine",
    )(yg, route, X, mod)


def _moe(h_bf, route, counts_f, X, mod, row_batch, layer, w_gate_up, b_gate_up, w_down, b_down):
    Tn = h_bf.shape[0]
    n = Tn * TOP_K
    top_e = route[:, ROUTE_E:ROUTE_E + TOP_K].astype(jnp.int32)
    rank = route[:, ROUTE_RANK:ROUTE_RANK + TOP_K].astype(jnp.int32)
    counts = counts_f[:, 0].astype(jnp.int32)
    padded = (counts + EXPERT_ROWS - 1) // EXPERT_ROWS * EXPERT_ROWS
    pad_end = jnp.cumsum(padded)
    pad_start = pad_end - padded
    experts = jnp.arange(N_EXPERTS, dtype=jnp.int32)
    start_of = jnp.sum(jnp.where(top_e[:, :, None] == experts, pad_start, 0), axis=-1)
    dest = start_of + rank
    P = n + N_EXPERTS * EXPERT_ROWS
    n_blk = P // EXPERT_ROWS
    tok = jnp.broadcast_to(jnp.arange(Tn, dtype=jnp.int32)[:, None], (Tn, TOP_K))
    slot_tok = jnp.full((P,), Tn, jnp.int32).at[dest.reshape(-1)].set(
        tok.reshape(-1), unique_indices=True, mode="promise_in_bounds")
    n_used = (pad_end[-1] // EXPERT_ROWS).astype(jnp.int32)
    blk_row = jnp.minimum(jnp.arange(n_blk, dtype=jnp.int32), n_used - 1) * EXPERT_ROWS
    blk_e = jnp.sum((pad_end[None, :] <= blk_row[:, None]).astype(jnp.int32), axis=1)
    blk_e = jnp.minimum(blk_e, N_EXPERTS - 1) + layer * N_EXPERTS
    hp = jnp.concatenate([h_bf, jnp.zeros((8, D_MODEL), h_bf.dtype)], axis=0)
    xbuf = hp.at[slot_tok].get(mode="promise_in_bounds")
    y = _experts(xbuf, blk_e, n_used.reshape(1), w_gate_up, b_gate_up, w_down, b_down)
    yg = y.at[dest.T.reshape(-1)].get(mode="promise_in_bounds").reshape(TOP_K, Tn, D_MODEL)
    return _combine(yg, route, X, mod, row_batch)


def _rope_tables(S, dim):
    t = jnp.arange(S, dtype=jnp.int32)
    row = (t // GRID_W).astype(F32)
    col = (t % GRID_W).astype(F32)
    n_freq = dim // 4
    inv = ROPE_THETA ** (-jnp.arange(n_freq, dtype=F32) / n_freq)
    ang = jnp.concatenate([row[:, None] * inv, col[:, None] * inv], axis=-1)
    cos, sin = jnp.cos(ang), jnp.sin(ang)
    reps = LANES // dim
    cos_t = jnp.tile(jnp.concatenate([cos, cos], axis=-1), (1, reps))
    sin_t = jnp.tile(jnp.concatenate([-sin, sin], axis=-1), (1, reps))
    cos_t = jnp.concatenate([cos_t, jnp.ones((ROW_TILE, LANES), F32)], axis=0)
    sin_t = jnp.concatenate([sin_t, jnp.zeros((ROW_TILE, LANES), F32)], axis=0)
    return cos_t, sin_t


def _block_ones(seg):
    i = np.arange(LANES)
    return jnp.asarray((i[:, None] // seg == i[None, :] // seg).astype(np.float32), dtype=BF16)


def _lane_tile(v):
    return jnp.tile(v.astype(F32), LANES // v.shape[0])


def kernel(x, c, ctx, c_ctx, norm1_g, norm2_g, w_ada, b_ada, w_in, b_gate, a_qn, a_kn, b_qn, b_kn, lam_q1, lam_k1, lam_q2, lam_k2, subln_g, c_qn, c_kn, sink, d_qn, d_kn, rpb, w_branch, w_out, router_w, router_b, w_gate_up, b_gate_up, w_down, b_down):
    B, S, D = x.shape
    C = ctx.shape[1]
    L = w_in.shape[0]
    assert D == D_MODEL and C == ROW_TILE and S % ROW_TILE == 0 and B + 1 <= 8
    n_lat = B * S
    per = S // ROW_TILE
    nlat_tiles = B * per

    def row_batch(t):
        return jnp.where(t < nlat_tiles, t // per, B)

    cvec = jnp.zeros((8, D), F32).at[:B].set(c).at[B].set(c_ctx)
    mod_all = _ada(cvec, w_ada, b_ada)

    tables = _rope_tables(S, HEAD_DIM) + _rope_tables(S, B_DK)
    ones64, ones32 = _block_ones(HEAD_DIM), _block_ones(B_DK)

    E = w_gate_up.shape[1]
    w_gu_all = w_gate_up.reshape(L * E, D, 2 * D_FF)
    w_dn_all = w_down.reshape(L * E, D_FF, D)

    X = jnp.concatenate([x.reshape(n_lat, D), ctx.reshape(B * C, D)], axis=0)
    for l in range(L):
        last = l == L - 1
        lam_init = 0.8 - 0.6 * math.exp(-0.3 * l)
        mod = mod_all[l].reshape(8, 1, N_MOD * D)
        qkv, gate = _in_proj(X, norm1_g[l], mod, w_in[l].astype(BF16), b_gate[l], row_batch)
        gains = jnp.stack([_lane_tile(g[l]) for g in (a_qn, a_kn, b_qn, b_kn, c_qn, c_kn, d_qn, d_kn)])
        qTs, ks, vTs = _qkv_post(qkv, tables, ones64, ones32, gains, B, S, C)

        lamvec = jnp.zeros((8, LANES), F32)
        for r, v in enumerate((lam_q1, lam_k1, lam_q2, lam_k2)):
            lamvec = lamvec.at[r, :B_DK].set(v[l])
        subg = subln_g[l].reshape(HEAD_DIM, 1)
        sink_l = sink[l].astype(F32)
        nbr_table = _nbr_bias_table(rpb[l])

        o_a = _flash(qTs[0], ks[0], vTs[0], mode="gqa", S=S, C=C, ctx_only=False)
        o_b = _flash(qTs[1], ks[1], vTs[1], mode="diff", S=S, C=C, ctx_only=False,
                     lamvec=lamvec, subg=subg, lam_init=lam_init)
        o_c = _window(qTs[2], ks[2], vTs[2], sink_l, S, C)
        o_d = _nbr(qTs[3], ks[3], vTs[3], nbr_table, S, C)
        if not last:
            o_a = _flash(qTs[0], ks[0], vTs[0], mode="gqa", S=S, C=C, ctx_only=True, out_prev=o_a)
            o_b = _flash(qTs[1], ks[1], vTs[1], mode="diff", S=S, C=C, ctx_only=True,
                         lamvec=lamvec, subg=subg, lam_init=lam_init, out_prev=o_b)
            o_c = _flash(qTs[2], ks[2], vTs[2], mode="gqa", S=S, C=C, ctx_only=True, sink=sink_l, out_prev=o_c)
            o_d = _flash(qTs[3], ks[3], vTs[3], mode="mha", S=S, C=C, ctx_only=True, out_prev=o_d)

        n_rows = n_lat if last else X.shape[0]
        Xm, h2, route_t, counts = _merge((o_a, o_b, o_c, o_d), gate, X, mod, w_branch[l].astype(BF16),
                                         w_out[l].astype(BF16), norm2_g[l], router_w[l].T.astype(F32),
                                         router_b[l].astype(F32).reshape(N_EXPERTS, 1), n_rows, B, S, C,
                                         row_batch)
        X = _moe(h2, route_t.T, counts, Xm, mod, row_batch, l, w_gu_all, b_gate_up.reshape(L * E, -1),
                 w_dn_all, b_down.reshape(L * E, -1))
    return X[:n_lat].reshape(B, S, D)
```

```python
import functools
import math

import numpy as np
import jax
import jax.numpy as jnp
from jax import lax
from jax.experimental import pallas as pl
from jax.experimental.pallas import tpu as pltpu

F32 = jnp.float32
BF16 = jnp.bfloat16

D_MODEL = 1024
GRID_W = 64
HEAD_DIM = 64
B_DK = 32
WINDOW = 128
NA_ROWS = 8
NA_COLS = 16
ROPE_THETA = 10000.0
N_EXPERTS = 32
TOP_K = 4
D_FF = D_MODEL
SWIGLU_LIMIT = 7.0
SWIGLU_ALPHA = 1.702
N_MOD = 6
EPS = 1e-6
NEG_INF = -1e30
LOG2E = 1.4426950408889634

Q_TOT = 1024
KV_TOT = 1536
QKV_TOT = Q_TOT + KV_TOT
GATE_TOT = 4 * D_MODEL

ROW_TILE = 256
LANES = 128
EXPERT_ROWS = 256
NBR_WIN_ROWS = 10
NBR_TABLE_ROWS = NBR_WIN_ROWS + 8
FLASH_KEYS = 1408
SUM_ROWS = 16
VMEM_LIMIT = 52 * 1024 * 1024


def _cparams(sem):
    return pltpu.CompilerParams(dimension_semantics=sem, vmem_limit_bytes=VMEM_LIMIT)


def _dot(a, b):
    return jnp.dot(a, b, preferred_element_type=F32)


def _ada_kernel(c_ref, w_ref, b_ref, o_ref):
    c = c_ref[...]
    s = c * jax.nn.sigmoid(c)
    o_ref[0] = jnp.dot(s, w_ref[0], precision=lax.Precision.HIGHEST,
                       preferred_element_type=F32) + b_ref[0]


def _ada(cvec, w_ada, b_ada):
    L = w_ada.shape[0]
    n_out = w_ada.shape[2]
    tn = 1536
    return pl.pallas_call(
        _ada_kernel,
        grid=(L, n_out // tn),
        in_specs=[
            pl.BlockSpec((8, D_MODEL), lambda l, j: (0, 0)),
            pl.BlockSpec((1, D_MODEL, tn), lambda l, j: (l, 0, j)),
            pl.BlockSpec((1, 1, tn), lambda l, j: (l, 0, j)),
        ],
        out_specs=pl.BlockSpec((1, 8, tn), lambda l, j: (l, 0, j)),
        out_shape=jax.ShapeDtypeStruct((L, 8, n_out), F32),
        compiler_params=_cparams(("arbitrary", "arbitrary")),
        name="ada_mod",
    )(cvec, w_ada, b_ada.reshape(L, 1, n_out))


def _in_kernel(x_ref, g_ref, mod_ref, w_ref, bg_ref, qkv_ref, gate_ref, *, tn):
    x = x_ref[...]
    ms = jnp.mean(x * x, axis=-1, keepdims=True)
    y = x * lax.rsqrt(ms + EPS) * g_ref[...]
    mod = mod_ref[0]
    sh = mod[:, 0:D_MODEL]
    sc = mod[:, D_MODEL:2 * D_MODEL]
    h = (y * (1 + sc) + sh).astype(BF16)
    for j in range(QKV_TOT // tn):
        qkv_ref[:, j * tn:(j + 1) * tn] = _dot(h, w_ref[:, j * tn:(j + 1) * tn]).astype(BF16)
    for j in range(GATE_TOT // tn):
        g = _dot(h, w_ref[:, QKV_TOT + j * tn:QKV_TOT + (j + 1) * tn]) + bg_ref[:, j * tn:(j + 1) * tn]
        gate_ref[:, j * tn:(j + 1) * tn] = jax.nn.sigmoid(g).astype(BF16)


def _in_proj(X, norm_g, mod, w_in_bf, b_gate, row_batch):
    R = X.shape[0]
    nt = R // ROW_TILE
    return pl.pallas_call(
        functools.partial(_in_kernel, tn=512),
        grid=(nt,),
        in_specs=[
            pl.BlockSpec((ROW_TILE, D_MODEL), lambda t: (t, 0)),
            pl.BlockSpec((1, D_MODEL), lambda t: (0, 0)),
            pl.BlockSpec((1, 1, N_MOD * D_MODEL), lambda t: (row_batch(t), 0, 0)),
            pl.BlockSpec((D_MODEL, QKV_TOT + GATE_TOT), lambda t: (0, 0), pipeline_mode=pl.Buffered(1)),
            pl.BlockSpec((1, GATE_TOT), lambda t: (0, 0)),
        ],
        out_specs=[
            pl.BlockSpec((ROW_TILE, QKV_TOT), lambda t: (t, 0)),
            pl.BlockSpec((ROW_TILE, GATE_TOT), lambda t: (t, 0)),
        ],
        out_shape=[
            jax.ShapeDtypeStruct((R, QKV_TOT), BF16),
            jax.ShapeDtypeStruct((R, GATE_TOT), BF16),
        ],
        compiler_params=_cparams(("arbitrary",)),
        name="in_proj",
    )(X, norm_g.reshape(1, D_MODEL), mod, w_in_bf, b_gate.reshape(1, GATE_TOT))


def _seg_rms(x, ones, seg, g):
    sq = x * x
    hi = sq.astype(BF16)
    lo = (sq - hi.astype(F32)).astype(BF16)
    ss = _dot(hi, ones) + _dot(lo, ones)
    return x * lax.rsqrt(ss * (1.0 / seg) + EPS) * g


def _rot_half(y, half):
    lane = lax.broadcasted_iota(jnp.int32, y.shape, 1)
    fwd = pltpu.roll(y, LANES - half, axis=1)
    bwd = pltpu.roll(y, half, axis=1)
    return jnp.where((lane % (2 * half)) < half, fwd, bwd)


def _qkv_post_kernel(qkv_ref, c64_ref, s64_ref, c32_ref, s32_ref, ones64_ref, ones32_ref, gains_ref,
                     qa_ref, qb_ref, qc_ref, qd_ref, ka_ref, kb_ref, kc_ref, kd_ref,
                     va_ref, vb_ref, vc_ref, vd_ref):
    gains = gains_ref[...]
    rope = {64: (c64_ref[...], s64_ref[...]), 32: (c32_ref[...], s32_ref[...])}
    ones = {64: ones64_ref[...], 32: ones32_ref[...]}

    def chunk(col):
        return qkv_ref[:, col:col + LANES].astype(F32)

    def normed(col, seg, gain_row, use_rope, scale):
        y = _seg_rms(chunk(col), ones[seg], seg, gains[gain_row:gain_row + 1, :])
        if use_rope:
            cos, sin = rope[seg]
            y = y * cos + _rot_half(y, seg // 2) * sin
        return y * scale if scale != 1.0 else y

    q_refs = (qa_ref, qb_ref, qc_ref, qd_ref)
    q_seg = (64, 32, 64, 64)
    q_rope = (True, True, True, False)
    for m in range(4):
        scale = float(q_seg[m]) ** -0.5 * LOG2E
        for c in range(2):
            y = normed(m * 256 + c * LANES, q_seg[m], 2 * m, q_rope[m], scale)
            q_refs[m][0, c * LANES:(c + 1) * LANES, :] = y.T.astype(BF16)

    k_refs = (ka_ref, kb_ref, kc_ref, kd_ref)
    v_refs = (va_ref, vb_ref, vc_ref, vd_ref)
    widths = (128, 256, 128, 256)
    kcol = Q_TOT
    vcol = Q_TOT + sum(widths)
    for m in range(4):
        for c in range(widths[m] // LANES):
            y = normed(kcol, q_seg[m], 2 * m + 1, q_rope[m], 1.0)
            k_refs[m][0, :, c * LANES:(c + 1) * LANES] = y.astype(BF16)
            v_refs[m][0, c * LANES:(c + 1) * LANES, :] = chunk(vcol).T.astype(BF16)
            kcol += LANES
            vcol += LANES


def _qkv_post(qkv, tables, ones64, ones32, gains, B, S, C):
    R = qkv.shape[0]
    nt = R // ROW_TILE
    per = S // ROW_TILE
    nlat = B * per
    T = S + C

    def bidx(t):
        return jnp.where(t < nlat, t // per, t - nlat)

    def pidx(t):
        return jnp.where(t < nlat, t % per, per)

    tab_spec = pl.BlockSpec((ROW_TILE, LANES), lambda t: (pidx(t), 0))
    const_spec = pl.BlockSpec((LANES, LANES), lambda t: (0, 0))
    widths = (128, 256, 128, 256)
    q_specs = [pl.BlockSpec((1, 256, ROW_TILE), lambda t: (bidx(t), 0, pidx(t))) for _ in range(4)]
    k_specs = [pl.BlockSpec((1, ROW_TILE, w), lambda t: (bidx(t), pidx(t), 0)) for w in widths]
    v_specs = [pl.BlockSpec((1, w, ROW_TILE), lambda t: (bidx(t), 0, pidx(t))) for w in widths]
    q_shapes = [jax.ShapeDtypeStruct((B, 256, T), BF16) for _ in range(4)]
    k_shapes = [jax.ShapeDtypeStruct((B, T, w), BF16) for w in widths]
    v_shapes = [jax.ShapeDtypeStruct((B, w, T), BF16) for w in widths]
    outs = pl.pallas_call(
        _qkv_post_kernel,
        grid=(nt,),
        in_specs=[pl.BlockSpec((ROW_TILE, QKV_TOT), lambda t: (t, 0)),
                  tab_spec, tab_spec, tab_spec, tab_spec, const_spec, const_spec,
                  pl.BlockSpec((8, LANES), lambda t: (0, 0))],
        out_specs=q_specs + k_specs + v_specs,
        out_shape=q_shapes + k_shapes + v_shapes,
        compiler_params=_cparams(("arbitrary",)),
        name="qkv_post",
    )(qkv, *tables, ones64, ones32, gains)
    return outs[0:4], outs[4:8], outs[8:12]


def _pad_queries(q, pieces, kw, unit):
    tq = q.shape[1]
    r = lax.broadcasted_iota(jnp.int32, (kw, tq), 0)
    blocks = []
    for row0, size, extra in pieces:
        tiled = jnp.concatenate([q[row0:row0 + size, :]] * (kw // size), axis=0)
        off = unit * HEAD_DIM + extra
        blocks.append(jnp.where((r >= off) & (r < off + size), tiled, 0.0))
    out = blocks[0] if len(blocks) == 1 else jnp.concatenate(blocks, axis=1)
    return out.astype(BF16)


_PIECES = {
    "gqa": ((0, 64, 0), (64, 64, 0)),
    "diff": ((0, 32, 0), (32, 32, 32)),
    "mha": ((0, 64, 0),),
}


def _flash_kernel(*refs, mode, kw, tq, tk, kv_start, n_chunks, use_sink, lam_init, aliased):
    refs = list(refs)
    qT_ref, k_ref, vT_ref = refs[0:3]
    pos = 3
    sink_ref = None
    if use_sink:
        sink_ref = refs[pos]
        pos += 1
    if mode == "diff":
        lamvec_ref, subg_ref = refs[pos:pos + 2]
        pos += 2
    if aliased:
        pos += 1
    o_ref = refs[pos]
    s_a, s_b = refs[pos + 1:pos + 3]

    unit = pl.program_id(1)
    pieces = _PIECES[mode]
    ng = len(pieces)
    n = ng * tq
    qpad = _pad_queries(qT_ref[0].astype(F32), pieces, kw, unit)

    if use_sink:
        m0 = jnp.concatenate([jnp.full((1, tq), sink_ref[ng * unit + g] * LOG2E, F32) for g in range(ng)],
                             axis=1)
        l0 = jnp.ones((1, n), F32)
    else:
        m0 = jnp.full((1, n), NEG_INF, F32)
        l0 = jnp.zeros((1, n), F32)
    acc0 = jnp.zeros((HEAD_DIM, n), F32)
    ones_rows = jnp.ones((SUM_ROWS, tk), BF16)

    def chunk_start(j):
        return pl.multiple_of(kv_start + j * tk, LANES)

    def scores(j, s_ref):
        s = _dot(k_ref[0, pl.ds(chunk_start(j), tk), :], qpad)
        s_ref[...] = s
        return jnp.max(s, axis=0, keepdims=True)

    def absorb(j, s_ref, mc, state):
        m, l, acc = state
        m_new = jnp.maximum(m, mc)
        alpha = jnp.exp2(m - m_new)
        p = jnp.exp2(s_ref[...] - m_new)
        v_aug = jnp.concatenate([vT_ref[0, :, pl.ds(chunk_start(j), tk)], ones_rows], axis=0)
        r = _dot(v_aug, p.astype(BF16))
        return m_new, alpha * l + r[HEAD_DIM:HEAD_DIM + 1], alpha * acc + r[0:HEAD_DIM]

    mc_a = scores(0, s_a)
    n_pairs = (n_chunks - 1) // 2

    def pair(i, carry):
        mc_a, state = carry[0], carry[1:]
        mc_b = scores(2 * i + 1, s_b)
        state = absorb(2 * i, s_a, mc_a, state)
        mc_a = scores(2 * i + 2, s_a)
        state = absorb(2 * i + 1, s_b, mc_b, state)
        return (mc_a,) + tuple(state)

    carry = (mc_a, m0, l0, acc0)
    if n_pairs > 0:
        carry = lax.fori_loop(0, n_pairs, pair, carry)
    mc_a, state = carry[0], carry[1:]
    if n_chunks % 2 == 0:
        mc_b = scores(n_chunks - 1, s_b)
        state = absorb(n_chunks - 2, s_a, mc_a, state)
        state = absorb(n_chunks - 1, s_b, mc_b, state)
    else:
        state = absorb(n_chunks - 1, s_a, mc_a, state)
    m, l, acc = state
    o = acc / l

    if mode == "gqa":
        for g in range(ng):
            o_ref[0, g * HEAD_DIM:(g + 1) * HEAD_DIM, :] = o[:, g * tq:(g + 1) * tq].astype(o_ref.dtype)
    elif mode == "mha":
        o_ref[0] = o.astype(o_ref.dtype)
    else:
        lv = lamvec_ref[...]
        lam = (jnp.exp(jnp.sum(lv[0:1] * lv[1:2], axis=1, keepdims=True))
               - jnp.exp(jnp.sum(lv[2:3] * lv[3:4], axis=1, keepdims=True)) + lam_init)
        d = o[:, 0:tq] - lam * o[:, tq:2 * tq]
        ms = jnp.mean(d * d, axis=0, keepdims=True)
        o_ref[0] = (d * lax.rsqrt(ms + EPS) * subg_ref[...] * (1.0 - lam_init)).astype(o_ref.dtype)


def _flash(qT, k, vT, *, mode, S, C, ctx_only, sink=None, lamvec=None, subg=None, lam_init=0.0, out_prev=None):
    B, _, T = qT.shape
    kw = k.shape[2]
    q_rows = 128 if mode == "gqa" else 64
    units = 256 // q_rows
    if ctx_only:
        tq, nq, q_blk0 = C, 1, S // C
        tk, kv_start, n_chunks = C, S, 1
    else:
        tq, nq, q_blk0 = 256, S // 256, 0
        tk = FLASH_KEYS if T % FLASH_KEYS == 0 else 256
        kv_start, n_chunks = 0, T // tk
    in_specs = [
        pl.BlockSpec((1, q_rows, tq), lambda b, u, i: (b, u, i + q_blk0)),
        pl.BlockSpec((1, T, kw), lambda b, u, i: (b, 0, 0)),
        pl.BlockSpec((1, HEAD_DIM, T), lambda b, u, i: (b, u, 0)),
    ]
    args = [qT, k, vT]
    if sink is not None:
        in_specs.append(pl.BlockSpec(memory_space=pltpu.SMEM))
        args.append(sink)
    if mode == "diff":
        in_specs += [pl.BlockSpec((8, LANES), lambda b, u, i: (0, 0)),
                     pl.BlockSpec((HEAD_DIM, 1), lambda b, u, i: (0, 0))]
        args += [lamvec, subg]
    aliases = {}
    if out_prev is not None:
        in_specs.append(pl.BlockSpec(memory_space=pl.ANY))
        aliases = {len(args): 0}
        args.append(out_prev)
    kern = functools.partial(_flash_kernel, mode=mode, kw=kw, tq=tq, tk=tk, kv_start=kv_start,
                             n_chunks=n_chunks, use_sink=sink is not None, lam_init=lam_init,
                             aliased=out_prev is not None)
    return pl.pallas_call(
        kern,
        grid=(B, units, nq),
        in_specs=in_specs,
        out_specs=pl.BlockSpec((1, q_rows, tq), lambda b, u, i: (b, u, i + q_blk0)),
        out_shape=jax.ShapeDtypeStruct((B, 256, T), BF16),
        input_output_aliases=aliases,
        scratch_shapes=[pltpu.VMEM((tk, len(_PIECES[mode]) * tq), F32)] * 2,
        compiler_params=_cparams(("arbitrary", "arbitrary", "arbitrary")),
        name="flash_" + mode + ("_ctx" if ctx_only else ""),
    )(*args)


WINDOW_BLOCKS = 2


def _window_kernel(qT_ref, k_ref, vT_ref, sink_ref, o_ref, *, S, C):
    step = pl.program_id(1)
    qb = WINDOW
    span = 3 * qb
    n = 2 * qb
    k_ctx = k_ref[0, S:S + C, :]
    for bi in range(WINDOW_BLOCKS):
        i = step * WINDOW_BLOCKS + bi
        cols = slice(bi * qb, (bi + 1) * qb)
        start = pl.multiple_of(jnp.clip((i - 1) * qb, 0, S - span), LANES)
        k_loc = k_ref[0, pl.ds(start, span), :]
        kpos = start + lax.broadcasted_iota(jnp.int32, (span, n), 0)
        qpos = i * qb + lax.broadcasted_iota(jnp.int32, (span, n), 1) % qb
        in_window = jnp.abs(kpos - qpos) <= WINDOW
        for unit in range(2):
            q = qT_ref[0, unit * 128:(unit + 1) * 128, cols].astype(F32)
            qpad = _pad_queries(q, _PIECES["gqa"], LANES, unit)
            s_loc = jnp.where(in_window, _dot(k_loc, qpad), NEG_INF)
            s_ctx = _dot(k_ctx, qpad)
            sink = jnp.concatenate(
                [jnp.full((1, qb), sink_ref[2 * unit + g] * LOG2E, F32) for g in range(2)], axis=1)
            m = jnp.maximum(jnp.maximum(jnp.max(s_loc, axis=0, keepdims=True),
                                        jnp.max(s_ctx, axis=0, keepdims=True)), sink)
            e_loc = jnp.exp2(s_loc - m)
            e_ctx = jnp.exp2(s_ctx - m)
            den = (jnp.sum(e_loc, axis=0, keepdims=True) + jnp.sum(e_ctx, axis=0, keepdims=True)
                   + jnp.exp2(sink - m))
            v_rows = slice(unit * HEAD_DIM, (unit + 1) * HEAD_DIM)
            o = (_dot(vT_ref[0, v_rows, pl.ds(start, span)], e_loc.astype(BF16))
                 + _dot(vT_ref[0, v_rows, S:S + C], e_ctx.astype(BF16))) / den
            for g in range(2):
                head = 2 * unit + g
                o_ref[0, head * HEAD_DIM:(head + 1) * HEAD_DIM, cols] = (
                    o[:, g * qb:(g + 1) * qb].astype(o_ref.dtype))


def _window(qT, k, vT, sink, S, C):
    B, _, T = qT.shape
    tq = WINDOW * WINDOW_BLOCKS
    return pl.pallas_call(
        functools.partial(_window_kernel, S=S, C=C),
        grid=(B, S // tq),
        in_specs=[
            pl.BlockSpec((1, 256, tq), lambda b, i: (b, 0, i)),
            pl.BlockSpec((1, T, LANES), lambda b, i: (b, 0, 0)),
            pl.BlockSpec((1, 2 * HEAD_DIM, T), lambda b, i: (b, 0, 0)),
            pl.BlockSpec(memory_space=pltpu.SMEM),
        ],
        out_specs=pl.BlockSpec((1, 256, tq), lambda b, i: (b, 0, i)),
        out_shape=jax.ShapeDtypeStruct((B, 256, T), BF16),
        compiler_params=_cparams(("arbitrary", "arbitrary")),
        name="window_attn",
    )(qT, k, vT, sink)


def _nbr_kernel(qT_ref, k_ref, vT_ref, tab_ref, o_ref, *, S, C):
    j = pl.program_id(1)
    rows = S // GRID_W
    n = 2 * GRID_W
    span = NBR_WIN_ROWS * GRID_W
    w0 = jnp.clip(2 * j - NA_ROWS // 2, 0, rows - NBR_WIN_ROWS)
    shift = w0 - 2 * j + NA_ROWS // 2 + 4
    start = pl.multiple_of(w0 * GRID_W, LANES)
    k_loc = k_ref[0, pl.ds(start, span), :]
    k_ctx = k_ref[0, S:S + C, :]
    kr = w0 + lax.broadcasted_iota(jnp.int32, (span, n), 0) // GRID_W
    qr = 2 * j + lax.broadcasted_iota(jnp.int32, (span, n), 1) // GRID_W
    r0 = jnp.clip(qr - NA_ROWS // 2, 0, rows - NA_ROWS)
    in_rows = (kr >= r0) & (kr < r0 + NA_ROWS)
    for head in range(4):
        h_rows = slice(head * HEAD_DIM, (head + 1) * HEAD_DIM)
        qpad = _pad_queries(qT_ref[0, h_rows, :].astype(F32), _PIECES["mha"], 256, head)
        bias = tab_ref[head, pl.ds(pl.multiple_of(shift * GRID_W, GRID_W), span), :]
        s_loc = jnp.where(in_rows, _dot(k_loc, qpad) + bias, NEG_INF)
        s_ctx = _dot(k_ctx, qpad)
        m = jnp.maximum(jnp.max(s_loc, axis=0, keepdims=True), jnp.max(s_ctx, axis=0, keepdims=True))
        e_loc = jnp.exp2(s_loc - m)
        e_ctx = jnp.exp2(s_ctx - m)
        den = jnp.sum(e_loc, axis=0, keepdims=True) + jnp.sum(e_ctx, axis=0, keepdims=True)
        o = (_dot(vT_ref[0, h_rows, pl.ds(start, span)], e_loc.astype(BF16))
             + _dot(vT_ref[0, h_rows, S:S + C], e_ctx.astype(BF16))) / den
        o_ref[0, h_rows, :] = o.astype(o_ref.dtype)


def _nbr_bias_table(rpb):
    u = np.arange(NBR_TABLE_ROWS)[:, None, None, None]
    kc = np.arange(GRID_W)[None, :, None, None]
    e = np.arange(2)[None, None, :, None]
    qc = np.arange(GRID_W)[None, None, None, :]
    dr = u - e - 1
    row_ok = (dr >= 0) & (dr < 2 * NA_ROWS - 1)
    dc = np.clip(kc - qc, -(NA_COLS - 1), NA_COLS - 1) + (NA_COLS - 1)
    c0 = np.clip(qc - NA_COLS // 2, 0, GRID_W - NA_COLS)
    col_ok = (kc >= c0) & (kc < c0 + NA_COLS)
    shape = (NBR_TABLE_ROWS, GRID_W, 2, GRID_W)
    n_dr, n_dc = 2 * NA_ROWS - 1, 2 * NA_COLS - 1
    sel_r = ((dr[:, 0, :, 0, None] == np.arange(n_dr)) & row_ok[:, 0, :, 0, None]).astype(np.float32)
    sel_c = (dc[0, :, 0, :, None] == np.arange(n_dc)).astype(np.float32)
    vals = jnp.einsum("uer,hrc,kqc->hukeq", sel_r, rpb.astype(F32) * LOG2E, sel_c,
                      precision=lax.Precision.HIGHEST)
    vals = jnp.where(np.broadcast_to(col_ok, shape), vals, NEG_INF)
    return vals.reshape(rpb.shape[0], NBR_TABLE_ROWS * GRID_W, 2 * GRID_W)


def _nbr(qT, k, vT, table, S, C):
    B, _, T = qT.shape
    return pl.pallas_call(
        functools.partial(_nbr_kernel, S=S, C=C),
        grid=(B, S // (2 * GRID_W)),
        in_specs=[
            pl.BlockSpec((1, 256, 2 * GRID_W), lambda b, j: (b, 0, j)),
            pl.BlockSpec((1, T, 256), lambda b, j: (b, 0, 0)),
            pl.BlockSpec((1, 256, T), lambda b, j: (b, 0, 0)),
            pl.BlockSpec((4, NBR_TABLE_ROWS * GRID_W, 2 * GRID_W), lambda b, j: (0, 0, 0)),
        ],
        out_specs=pl.BlockSpec((1, 256, 2 * GRID_W), lambda b, j: (b, 0, j)),
        out_shape=jax.ShapeDtypeStruct((B, 256, T), BF16),
        compiler_params=_cparams(("arbitrary", "arbitrary")),
        name="nbr_attn",
    )(qT, k, vT, table)


ROUTE_W, ROUTE_E, ROUTE_RANK = 0, 4, 8
ROUTE_ROWS = 16
MOE_GROUPS = 2


def _merge_kernel(oa_ref, ob_ref, oc_ref, od_ref, gate_ref, x_ref, mod_ref, wb_ref, wo_ref, n2_ref,
                  rw_ref, rb_ref, tri_ref, xo_ref, h_ref, route_ref, cnt_ref, base_ref, *, group_tiles):
    @pl.when(pl.program_id(0) % group_tiles == 0)
    def _():
        base_ref[...] = jnp.zeros_like(base_ref)

    acc = None
    for nbr, o_ref in enumerate((oa_ref, ob_ref, oc_ref, od_ref)):
        proj = lax.dot_general(o_ref[0], wb_ref[nbr], (((0,), (0,)), ((), ())),
                               preferred_element_type=F32)
        term = gate_ref[:, nbr * D_MODEL:(nbr + 1) * D_MODEL].astype(F32) * proj
        acc = term if acc is None else acc + term
    mix = _dot(acc.astype(BF16), wo_ref[...])
    mod = mod_ref[0]
    g1 = mod[:, 2 * D_MODEL:3 * D_MODEL]
    sh2 = mod[:, 3 * D_MODEL:4 * D_MODEL]
    sc2 = mod[:, 4 * D_MODEL:5 * D_MODEL]
    xn = x_ref[...] + g1 * mix
    xo_ref[...] = xn
    ms = jnp.mean(xn * xn, axis=-1, keepdims=True)
    h = (xn * lax.rsqrt(ms + EPS) * n2_ref[...]) * (1 + sc2) + sh2
    h_ref[...] = h.astype(BF16)
    logits = lax.dot_general(rw_ref[...], h, (((1,), (1,)), ((), ())), precision=lax.Precision.HIGHEST,
                             preferred_element_type=F32) + rb_ref[...]
    row_f = lax.broadcasted_iota(jnp.int32, logits.shape, 0).astype(F32)
    work = logits
    picks = []
    for _ in range(TOP_K):
        top = jnp.max(work, axis=0, keepdims=True)
        idx = jnp.min(jnp.where(work == top, row_f, float(N_EXPERTS)), axis=0, keepdims=True)
        hit = row_f == idx
        picks.append((top, idx, hit))
        work = jnp.where(hit, -jnp.inf, work)
    ex = [jnp.exp(top - picks[0][0]) for top, _, _ in picks]
    den = ex[0] + ex[1] + ex[2] + ex[3]
    chosen = jnp.zeros(logits.shape, F32)
    for _, _, hit in picks:
        chosen = chosen + hit.astype(F32)
    base = base_ref[...][:, 0:1]
    before = _dot(chosen.astype(BF16), tri_ref[...]) + base
    fields = ([ex[k] / den for k in range(TOP_K)] + [idx for _, idx, _ in picks]
              + [jnp.sum(jnp.where(hit, before, 0.0), axis=0, keepdims=True) for _, _, hit in picks])
    fields.append(jnp.zeros((ROUTE_ROWS - len(fields), logits.shape[1]), F32))
    route_ref[...] = jnp.concatenate(fields, axis=0)
    base_ref[...] = base_ref[...] + jnp.sum(chosen, axis=1, keepdims=True)
    cnt_ref[0] = base_ref[...]


def _merge(oTs, gate, X, mod, wb_bf, wo_bf, norm2_g, rw_t, rb_col, n_rows, B, S, C, row_batch):
    nt = n_rows // ROW_TILE
    per = S // ROW_TILE
    nlat = B * per

    def bidx(t):
        return jnp.where(t < nlat, t // per, t - nlat)

    def pidx(t):
        return jnp.where(t < nlat, t % per, per)

    o_spec = pl.BlockSpec((1, 256, ROW_TILE), lambda t: (bidx(t), 0, pidx(t)))
    r = np.arange(ROW_TILE)
    strict_upper = jnp.asarray((r[:, None] < r[None, :]).astype(np.float32), dtype=BF16)
    group_tiles = nt // MOE_GROUPS
    assert nt % MOE_GROUPS == 0
    return pl.pallas_call(
        functools.partial(_merge_kernel, group_tiles=group_tiles),
        grid=(nt,),
        in_specs=[o_spec, o_spec, o_spec, o_spec,
                  pl.BlockSpec((ROW_TILE, GATE_TOT), lambda t: (t, 0)),
                  pl.BlockSpec((ROW_TILE, D_MODEL), lambda t: (t, 0)),
                  pl.BlockSpec((1, 1, N_MOD * D_MODEL), lambda t: (row_batch(t), 0, 0)),
                  pl.BlockSpec((4, 256, D_MODEL), lambda t: (0, 0, 0)),
                  pl.BlockSpec((D_MODEL, D_MODEL), lambda t: (0, 0)),
                  pl.BlockSpec((1, D_MODEL), lambda t: (0, 0)),
                  pl.BlockSpec((N_EXPERTS, D_MODEL), lambda t: (0, 0)),
                  pl.BlockSpec((N_EXPERTS, 1), lambda t: (0, 0)),
                  pl.BlockSpec((ROW_TILE, ROW_TILE), lambda t: (0, 0))],
        out_specs=[pl.BlockSpec((ROW_TILE, D_MODEL), lambda t: (t, 0)),
                   pl.BlockSpec((ROW_TILE, D_MODEL), lambda t: (t, 0)),
                   pl.BlockSpec((ROUTE_ROWS, ROW_TILE), lambda t: (0, t)),
                   pl.BlockSpec((1, N_EXPERTS, LANES), lambda t: (t // group_tiles, 0, 0))],
        out_shape=[jax.ShapeDtypeStruct((n_rows, D_MODEL), F32),
                   jax.ShapeDtypeStruct((n_rows, D_MODEL), BF16),
                   jax.ShapeDtypeStruct((ROUTE_ROWS, n_rows), F32),
                   jax.ShapeDtypeStruct((MOE_GROUPS, N_EXPERTS, LANES), F32)],
        scratch_shapes=[pltpu.VMEM((N_EXPERTS, LANES), F32)],
        compiler_params=_cparams(("arbitrary",)),
        name="merge",
    )(*oTs, gate, X, mod, wb_bf, wo_bf, norm2_g.reshape(1, D_MODEL), rw_t, rb_col, strict_upper)


def _expert_kernel(be_ref, nu_ref, x_ref, wgu_ref, bgu_ref, wd_ref, bd_ref, y_ref, wgu_s, wd_s):
    i = pl.program_id(0)

    @pl.when(i < nu_ref[0])
    def _():
        prev = be_ref[jnp.maximum(i - 1, 0)]

        @pl.when((i == 0) | (be_ref[i] != prev))
        def _():
            wgu_s[...] = wgu_ref[0].astype(BF16)
            wd_s[...] = wd_ref[0].astype(BF16)

        gu = _dot(x_ref[...], wgu_s[...]) + bgu_ref[0]
        gate = jnp.minimum(gu[:, :D_FF], SWIGLU_LIMIT)
        up = jnp.clip(gu[:, D_FF:], -SWIGLU_LIMIT, SWIGLU_LIMIT)
        a = gate * jax.nn.sigmoid(SWIGLU_ALPHA * gate) * (up + 1)
        y_ref[...] = (_dot(a.astype(BF16), wd_s[...]) + bd_ref[0]).astype(y_ref.dtype)


def _experts(xbuf, blk_e, n_used, w_gate_up, b_gate_up, w_down, b_down):
    P = xbuf.shape[0]
    n_blk = P // EXPERT_ROWS
    E = w_gate_up.shape[0]

    def row_map(i, be, nu):
        return (jnp.maximum(jnp.minimum(i, nu[0] - 1), 0), 0)

    def w_map(i, be, nu):
        return (be[i], 0, 0)

    grid_spec = pltpu.PrefetchScalarGridSpec(
        num_scalar_prefetch=2,
        grid=(n_blk,),
        in_specs=[
            pl.BlockSpec((EXPERT_ROWS, D_MODEL), row_map),
            pl.BlockSpec((1, D_MODEL, 2 * D_FF), w_map),
            pl.BlockSpec((1, 1, 2 * D_FF), w_map),
            pl.BlockSpec((1, D_FF, D_MODEL), w_map),
            pl.BlockSpec((1, 1, D_MODEL), w_map),
        ],
        out_specs=pl.BlockSpec((EXPERT_ROWS, D_MODEL), row_map),
        scratch_shapes=[pltpu.VMEM((D_MODEL, 2 * D_FF), BF16), pltpu.VMEM((D_FF, D_MODEL), BF16)],
    )
    return pl.pallas_call(
        _expert_kernel,
        grid_spec=grid_spec,
        out_shape=jax.ShapeDtypeStruct((P, D_MODEL), BF16),
        compiler_params=_cparams(("arbitrary",)),
        name="experts",
    )(blk_e, n_used, xbuf, w_gate_up, b_gate_up.reshape(E, 1, 2 * D_FF), w_down, b_down.reshape(E, 1, D_MODEL))


def _combine_kernel(yg_ref, route_ref, x_ref, mod_ref, *rest):
    o_ref = rest[-1]
    route = route_ref[...]
    y = None
    for k in range(TOP_K):
        term = route[:, ROUTE_W + k:ROUTE_W + k + 1] * yg_ref[k].astype(F32)
        y = term if y is None else y + term
    g2 = mod_ref[0][:, 5 * D_MODEL:6 * D_MODEL]
    o_ref[...] = x_ref[...] + g2 * y


def _combine(yg, route, X, mod, row_batch, tile0, out_prev):
    n_rows = X.shape[0]
    in_specs = [pl.BlockSpec((TOP_K, ROW_TILE, D_MODEL), lambda t: (0, t, 0)),
                pl.BlockSpec((ROW_TILE, ROUTE_ROWS), lambda t: (t + tile0, 0)),
                pl.BlockSpec((ROW_TILE, D_MODEL), lambda t: (t + tile0, 0)),
                pl.BlockSpec((1, 1, N_MOD * D_MODEL), lambda t: (row_batch(t + tile0), 0, 0))]
    args = [yg, route, X, mod]
    aliases = {}
    if out_prev is not None:
        in_specs.append(pl.BlockSpec(memory_space=pl.ANY))
        aliases = {len(args): 0}
        args.append(out_prev)
    return pl.pallas_call(
        _combine_kernel,
        grid=(yg.shape[1] // ROW_TILE,),
        in_specs=in_specs,
        out_specs=pl.BlockSpec((ROW_TILE, D_MODEL), lambda t: (t + tile0, 0)),
        out_shape=jax.ShapeDtypeStruct((n_rows, D_MODEL), F32),
        input_output_aliases=aliases,
        compiler_params=_cparams(("arbitrary",)),
        name="moe_combine",
    )(*args)


def _moe(h_bf, route, counts_f, X, mod, row_batch, layer, w_gate_up, b_gate_up, w_down, b_down):
    n_groups = counts_f.shape[0]
    Tg = h_bf.shape[0] // n_groups
    n = Tg * TOP_K
    P = n + N_EXPERTS * EXPERT_ROWS
    n_blk = P // EXPERT_ROWS
    experts = jnp.arange(N_EXPERTS, dtype=jnp.int32)
    out = None
    for g in range(n_groups):
        route_g = lax.slice_in_dim(route, g * Tg, (g + 1) * Tg, axis=0)
        top_e = route_g[:, ROUTE_E:ROUTE_E + TOP_K].astype(jnp.int32)
        rank = route_g[:, ROUTE_RANK:ROUTE_RANK + TOP_K].astype(jnp.int32)
        counts = counts_f[g, :, 0].astype(jnp.int32)
        padded = (counts + EXPERT_ROWS - 1) // EXPERT_ROWS * EXPERT_ROWS
        pad_end = jnp.cumsum(padded)
        pad_start = pad_end - padded
        start_of = jnp.sum(jnp.where(top_e[:, :, None] == experts, pad_start, 0), axis=-1)
        dest = start_of + rank
        tok = jnp.broadcast_to(jnp.arange(g * Tg, (g + 1) * Tg, dtype=jnp.int32)[:, None], (Tg, TOP_K))
        slot_tok = jnp.full((P,), g * Tg, jnp.int32).at[dest.reshape(-1)].set(
            tok.reshape(-1), unique_indices=True, mode="promise_in_bounds")
        n_used = (pad_end[-1] // EXPERT_ROWS).astype(jnp.int32)
        blk_row = jnp.minimum(jnp.arange(n_blk, dtype=jnp.int32), n_used - 1) * EXPERT_ROWS
        blk_e = jnp.sum((pad_end[None, :] <= blk_row[:, None]).astype(jnp.int32), axis=1)
        blk_e = jnp.minimum(blk_e, N_EXPERTS - 1) + layer * N_EXPERTS
        xbuf = h_bf.at[slot_tok].get(mode="promise_in_bounds")
        y = _experts(xbuf, blk_e, n_used.reshape(1), w_gate_up, b_gate_up, w_down, b_down)
        yg = y.at[dest.T.reshape(-1)].get(mode="promise_in_bounds").reshape(TOP_K, Tg, D_MODEL)
        out = _combine(yg, route, X, mod, row_batch, g * (Tg // ROW_TILE), out)
    return out


def _rope_tables(S, dim):
    t = jnp.arange(S, dtype=jnp.int32)
    row = (t // GRID_W).astype(F32)
    col = (t % GRID_W).astype(F32)
    n_freq = dim // 4
    inv = ROPE_THETA ** (-jnp.arange(n_freq, dtype=F32) / n_freq)
    ang = jnp.concatenate([row[:, None] * inv, col[:, None] * inv], axis=-1)
    cos, sin = jnp.cos(ang), jnp.sin(ang)
    reps = LANES // dim
    cos_t = jnp.tile(jnp.concatenate([cos, cos], axis=-1), (1, reps))
    sin_t = jnp.tile(jnp.concatenate([-sin, sin], axis=-1), (1, reps))
    cos_t = jnp.concatenate([cos_t, jnp.ones((ROW_TILE, LANES), F32)], axis=0)
    sin_t = jnp.concatenate([sin_t, jnp.zeros((ROW_TILE, LANES), F32)], axis=0)
    return cos_t, sin_t


def _block_ones(seg):
    i = np.arange(LANES)
    return jnp.asarray((i[:, None] // seg == i[None, :] // seg).astype(np.float32), dtype=BF16)


def _lane_tile(v):
    return jnp.tile(v.astype(F32), LANES // v.shape[0])


def kernel(x, c, ctx, c_ctx, norm1_g, norm2_g, w_ada, b_ada, w_in, b_gate, a_qn, a_kn, b_qn, b_kn, lam_q1, lam_k1, lam_q2, lam_k2, subln_g, c_qn, c_kn, sink, d_qn, d_kn, rpb, w_branch, w_out, router_w, router_b, w_gate_up, b_gate_up, w_down, b_down):
    B, S, D = x.shape
    C = ctx.shape[1]
    L = w_in.shape[0]
    assert D == D_MODEL and C == ROW_TILE and S % ROW_TILE == 0 and B + 1 <= 8
    n_lat = B * S
    per = S // ROW_TILE
    nlat_tiles = B * per

    def row_batch(t):
        return jnp.where(t < nlat_tiles, t // per, B)

    cvec = jnp.zeros((8, D), F32).at[:B].set(c).at[B].set(c_ctx)
    mod_all = _ada(cvec, w_ada, b_ada)

    tables = _rope_tables(S, HEAD_DIM) + _rope_tables(S, B_DK)
    ones64, ones32 = _block_ones(HEAD_DIM), _block_ones(B_DK)

    E = w_gate_up.shape[1]
    w_gu_all = w_gate_up.reshape(L * E, D, 2 * D_FF)
    w_dn_all = w_down.reshape(L * E, D_FF, D)

    X = jnp.concatenate([x.reshape(n_lat, D), ctx.reshape(B * C, D)], axis=0)
    for l in range(L):
        last = l == L - 1
        lam_init = 0.8 - 0.6 * math.exp(-0.3 * l)
        mod = mod_all[l].reshape(8, 1, N_MOD * D)
        qkv, gate = _in_proj(X, norm1_g[l], mod, w_in[l].astype(BF16), b_gate[l], row_batch)
        gains = jnp.stack([_lane_tile(g[l]) for g in (a_qn, a_kn, b_qn, b_kn, c_qn, c_kn, d_qn, d_kn)])
        qTs, ks, vTs = _qkv_post(qkv, tables, ones64, ones32, gains, B, S, C)

        lamvec = jnp.zeros((8, LANES), F32)
        for r, v in enumerate((lam_q1, lam_k1, lam_q2, lam_k2)):
            lamvec = lamvec.at[r, :B_DK].set(v[l])
        subg = subln_g[l].reshape(HEAD_DIM, 1)
        sink_l = sink[l].astype(F32)
        nbr_table = _nbr_bias_table(rpb[l])

        o_a = _flash(qTs[0], ks[0], vTs[0], mode="gqa", S=S, C=C, ctx_only=False)
        o_b = _flash(qTs[1], ks[1], vTs[1], mode="diff", S=S, C=C, ctx_only=False,
                     lamvec=lamvec, subg=subg, lam_init=lam_init)
        o_c = _window(qTs[2], ks[2], vTs[2], sink_l, S, C)
        o_d = _nbr(qTs[3], ks[3], vTs[3], nbr_table, S, C)
        if not last:
            o_a = _flash(qTs[0], ks[0], vTs[0], mode="gqa", S=S, C=C, ctx_only=True, out_prev=o_a)
            o_b = _flash(qTs[1], ks[1], vTs[1], mode="diff", S=S, C=C, ctx_only=True,
                         lamvec=lamvec, subg=subg, lam_init=lam_init, out_prev=o_b)
            o_c = _flash(qTs[2], ks[2], vTs[2], mode="gqa", S=S, C=C, ctx_only=True, sink=sink_l, out_prev=o_c)
            o_d = _flash(qTs[3], ks[3], vTs[3], mode="mha", S=S, C=C, ctx_only=True, out_prev=o_d)

        n_rows = n_lat if last else X.shape[0]
        Xm, h2, route_t, counts = _merge((o_a, o_b, o_c, o_d), gate, X, mod, w_branch[l].astype(BF16),
                                         w_out[l].astype(BF16), norm2_g[l], router_w[l].T.astype(F32),
                                         router_b[l].astype(F32).reshape(N_EXPERTS, 1), n_rows, B, S, C,
                                         row_batch)
        X = _moe(h2, route_t.T, counts, Xm, mod, row_batch, l, w_gu_all, b_gate_up.reshape(L * E, -1),
                 w_dn_all, b_down.reshape(L * E, -1))
    return X[:n_lat].reshape(B, S, D)
```

```python
import functools
import math

import numpy as np
import jax
import jax.numpy as jnp
from jax import lax
from jax.experimental import pallas as pl
from jax.experimental.pallas import tpu as pltpu

F32 = jnp.float32
BF16 = jnp.bfloat16

D_MODEL = 1024
GRID_W = 64
HEAD_DIM = 64
B_DK = 32
WINDOW = 128
NA_ROWS = 8
NA_COLS = 16
ROPE_THETA = 10000.0
N_EXPERTS = 32
TOP_K = 4
D_FF = D_MODEL
SWIGLU_LIMIT = 7.0
SWIGLU_ALPHA = 1.702
N_MOD = 6
EPS = 1e-6
NEG_INF = -1e30
LOG2E = 1.4426950408889634

Q_TOT = 1024
KV_TOT = 1536
QKV_TOT = Q_TOT + KV_TOT
GATE_TOT = 4 * D_MODEL

ROW_TILE = 256
LANES = 128
EXPERT_ROWS = 256
NBR_WIN_ROWS = 10
NBR_TABLE_ROWS = NBR_WIN_ROWS + 8
FLASH_KEYS = 1152
SUM_ROWS = 16
VMEM_LIMIT = 52 * 1024 * 1024


def _cparams(sem):
    return pltpu.CompilerParams(dimension_semantics=sem, vmem_limit_bytes=VMEM_LIMIT)


def _dot(a, b):
    return jnp.dot(a, b, preferred_element_type=F32)


def _ada_kernel(c_ref, w_ref, b_ref, o_ref):
    c = c_ref[...]
    s = c * jax.nn.sigmoid(c)
    o_ref[0] = jnp.dot(s, w_ref[0], precision=lax.Precision.HIGHEST,
                       preferred_element_type=F32) + b_ref[0]


def _ada(cvec, w_ada, b_ada):
    L = w_ada.shape[0]
    n_out = w_ada.shape[2]
    tn = 1536
    return pl.pallas_call(
        _ada_kernel,
        grid=(L, n_out // tn),
        in_specs=[
            pl.BlockSpec((8, D_MODEL), lambda l, j: (0, 0)),
            pl.BlockSpec((1, D_MODEL, tn), lambda l, j: (l, 0, j)),
            pl.BlockSpec((1, 1, tn), lambda l, j: (l, 0, j)),
        ],
        out_specs=pl.BlockSpec((1, 8, tn), lambda l, j: (l, 0, j)),
        out_shape=jax.ShapeDtypeStruct((L, 8, n_out), F32),
        compiler_params=_cparams(("arbitrary", "arbitrary")),
        name="ada_mod",
    )(cvec, w_ada, b_ada.reshape(L, 1, n_out))


def _in_kernel(x_ref, g_ref, mod_ref, w_ref, bg_ref, qkv_ref, gate_ref, *, tn):
    x = x_ref[...]
    ms = jnp.mean(x * x, axis=-1, keepdims=True)
    y = x * lax.rsqrt(ms + EPS) * g_ref[...]
    mod = mod_ref[0]
    sh = mod[:, 0:D_MODEL]
    sc = mod[:, D_MODEL:2 * D_MODEL]
    h = (y * (1 + sc) + sh).astype(BF16)
    for j in range(QKV_TOT // tn):
        qkv_ref[:, j * tn:(j + 1) * tn] = _dot(h, w_ref[:, j * tn:(j + 1) * tn]).astype(BF16)
    for j in range(GATE_TOT // tn):
        g = _dot(h, w_ref[:, QKV_TOT + j * tn:QKV_TOT + (j + 1) * tn]) + bg_ref[:, j * tn:(j + 1) * tn]
        gate_ref[:, j * tn:(j + 1) * tn] = jax.nn.sigmoid(g).astype(BF16)


def _in_proj(X, norm_g, mod, w_in_bf, b_gate, row_batch):
    R = X.shape[0]
    nt = R // ROW_TILE
    return pl.pallas_call(
        functools.partial(_in_kernel, tn=512),
        grid=(nt,),
        in_specs=[
            pl.BlockSpec((ROW_TILE, D_MODEL), lambda t: (t, 0)),
            pl.BlockSpec((1, D_MODEL), lambda t: (0, 0)),
            pl.BlockSpec((1, 1, N_MOD * D_MODEL), lambda t: (row_batch(t), 0, 0)),
            pl.BlockSpec((D_MODEL, QKV_TOT + GATE_TOT), lambda t: (0, 0), pipeline_mode=pl.Buffered(1)),
            pl.BlockSpec((1, GATE_TOT), lambda t: (0, 0)),
        ],
        out_specs=[
            pl.BlockSpec((ROW_TILE, QKV_TOT), lambda t: (t, 0)),
            pl.BlockSpec((ROW_TILE, GATE_TOT), lambda t: (t, 0)),
        ],
        out_shape=[
            jax.ShapeDtypeStruct((R, QKV_TOT), BF16),
            jax.ShapeDtypeStruct((R, GATE_TOT), BF16),
        ],
        compiler_params=_cparams(("arbitrary",)),
        name="in_proj",
    )(X, norm_g.reshape(1, D_MODEL), mod, w_in_bf, b_gate.reshape(1, GATE_TOT))


def _seg_rms(x, ones, seg, g):
    sq = x * x
    hi = sq.astype(BF16)
    lo = (sq - hi.astype(F32)).astype(BF16)
    ss = _dot(hi, ones) + _dot(lo, ones)
    return x * lax.rsqrt(ss * (1.0 / seg) + EPS) * g


def _rot_half(y, half):
    lane = lax.broadcasted_iota(jnp.int32, y.shape, 1)
    fwd = pltpu.roll(y, LANES - half, axis=1)
    bwd = pltpu.roll(y, half, axis=1)
    return jnp.where((lane % (2 * half)) < half, fwd, bwd)


def _qkv_post_kernel(qkv_ref, c64_ref, s64_ref, c32_ref, s32_ref, ones64_ref, ones32_ref, gains_ref,
                     qa_ref, qb_ref, qc_ref, qd_ref, ka_ref, kb_ref, kc_ref, kd_ref,
                     va_ref, vb_ref, vc_ref, vd_ref):
    gains = gains_ref[...]
    rope = {64: (c64_ref[...], s64_ref[...]), 32: (c32_ref[...], s32_ref[...])}
    ones = {64: ones64_ref[...], 32: ones32_ref[...]}

    def chunk(col):
        return qkv_ref[:, col:col + LANES].astype(F32)

    def normed(col, seg, gain_row, use_rope, scale):
        y = _seg_rms(chunk(col), ones[seg], seg, gains[gain_row:gain_row + 1, :])
        if use_rope:
            cos, sin = rope[seg]
            y = y * cos + _rot_half(y, seg // 2) * sin
        return y * scale if scale != 1.0 else y

    q_refs = (qa_ref, qb_ref, qc_ref, qd_ref)
    q_seg = (64, 32, 64, 64)
    q_rope = (True, True, True, False)
    for m in range(4):
        scale = float(q_seg[m]) ** -0.5 * LOG2E
        for c in range(2):
            y = normed(m * 256 + c * LANES, q_seg[m], 2 * m, q_rope[m], scale)
            q_refs[m][0, c * LANES:(c + 1) * LANES, :] = y.T.astype(BF16)

    k_refs = (ka_ref, kb_ref, kc_ref, kd_ref)
    v_refs = (va_ref, vb_ref, vc_ref, vd_ref)
    widths = (128, 256, 128, 256)
    kcol = Q_TOT
    vcol = Q_TOT + sum(widths)
    for m in range(4):
        for c in range(widths[m] // LANES):
            y = normed(kcol, q_seg[m], 2 * m + 1, q_rope[m], 1.0)
            k_refs[m][0, :, c * LANES:(c + 1) * LANES] = y.astype(BF16)
            v_refs[m][0, c * LANES:(c + 1) * LANES, :] = chunk(vcol).T.astype(BF16)
            kcol += LANES
            vcol += LANES


def _qkv_post(qkv, tables, ones64, ones32, gains, B, S, C):
    R = qkv.shape[0]
    nt = R // ROW_TILE
    per = S // ROW_TILE
    nlat = B * per
    T = S + C

    def bidx(t):
        return jnp.where(t < nlat, t // per, t - nlat)

    def pidx(t):
        return jnp.where(t < nlat, t % per, per)

    tab_spec = pl.BlockSpec((ROW_TILE, LANES), lambda t: (pidx(t), 0))
    const_spec = pl.BlockSpec((LANES, LANES), lambda t: (0, 0))
    widths = (128, 256, 128, 256)
    q_specs = [pl.BlockSpec((1, 256, ROW_TILE), lambda t: (bidx(t), 0, pidx(t))) for _ in range(4)]
    k_specs = [pl.BlockSpec((1, ROW_TILE, w), lambda t: (bidx(t), pidx(t), 0)) for w in widths]
    v_specs = [pl.BlockSpec((1, w, ROW_TILE), lambda t: (bidx(t), 0, pidx(t))) for w in widths]
    q_shapes = [jax.ShapeDtypeStruct((B, 256, T), BF16) for _ in range(4)]
    k_shapes = [jax.ShapeDtypeStruct((B, T, w), BF16) for w in widths]
    v_shapes = [jax.ShapeDtypeStruct((B, w, T), BF16) for w in widths]
    outs = pl.pallas_call(
        _qkv_post_kernel,
        grid=(nt,),
        in_specs=[pl.BlockSpec((ROW_TILE, QKV_TOT), lambda t: (t, 0)),
                  tab_spec, tab_spec, tab_spec, tab_spec, const_spec, const_spec,
                  pl.BlockSpec((8, LANES), lambda t: (0, 0))],
        out_specs=q_specs + k_specs + v_specs,
        out_shape=q_shapes + k_shapes + v_shapes,
        compiler_params=_cparams(("arbitrary",)),
        name="qkv_post",
    )(qkv, *tables, ones64, ones32, gains)
    return outs[0:4], outs[4:8], outs[8:12]


def _pad_queries(q, pieces, kw, unit):
    tq = q.shape[1]
    r = lax.broadcasted_iota(jnp.int32, (kw, tq), 0)
    blocks = []
    for row0, size, extra in pieces:
        tiled = jnp.concatenate([q[row0:row0 + size, :]] * (kw // size), axis=0)
        off = unit * HEAD_DIM + extra
        blocks.append(jnp.where((r >= off) & (r < off + size), tiled, 0.0))
    out = blocks[0] if len(blocks) == 1 else jnp.concatenate(blocks, axis=1)
    return out.astype(BF16)


_PIECES = {
    "gqa": ((0, 64, 0), (64, 64, 0)),
    "diff": ((0, 32, 0), (32, 32, 32)),
    "mha": ((0, 64, 0),),
}


def _flash_kernel(*refs, mode, kw, tq, max_keys, chunks, use_sink, lam_init, aliased):
    refs = list(refs)
    qT_ref, k_ref, vT_ref = refs[0:3]
    pos = 3
    sink_ref = None
    if use_sink:
        sink_ref = refs[pos]
        pos += 1
    if mode == "diff":
        lamvec_ref, subg_ref = refs[pos:pos + 2]
        pos += 2
    if aliased:
        pos += 1
    o_ref = refs[pos]
    s_bufs = refs[pos + 1:pos + 4]

    unit = pl.program_id(1)
    pieces = _PIECES[mode]
    ng = len(pieces)
    n = ng * tq
    qpad = _pad_queries(qT_ref[0].astype(F32), pieces, kw, unit)

    if use_sink:
        m0 = jnp.concatenate([jnp.full((1, tq), sink_ref[ng * unit + g] * LOG2E, F32) for g in range(ng)],
                             axis=1)
        l0 = jnp.ones((1, n), F32)
    else:
        m0 = jnp.full((1, n), NEG_INF, F32)
        l0 = jnp.zeros((1, n), F32)
    acc0 = jnp.zeros((HEAD_DIM, n), F32)
    ones_rows = jnp.ones((SUM_ROWS, max_keys), BF16)

    def scores(start, size, s_ref):
        s = _dot(k_ref[0, pl.ds(start, size), :], qpad)
        s_ref[0:size, :] = s
        return jnp.max(s, axis=0, keepdims=True)

    def absorb(start, size, s_ref, mc, state):
        m, l, acc = state
        m_new = jnp.maximum(m, mc)
        alpha = jnp.exp2(m - m_new)
        p = jnp.exp2(s_ref[0:size, :] - m_new)
        v_aug = jnp.concatenate([vT_ref[0, :, pl.ds(start, size)], ones_rows[:, 0:size]], axis=0)
        r = _dot(v_aug, p.astype(BF16))
        return m_new, alpha * l + r[HEAD_DIM:HEAD_DIM + 1], alpha * acc + r[0:HEAD_DIM]

    state = (m0, l0, acc0)
    first, main, n_main, tail = chunks

    def main_start(i):
        return pl.multiple_of(main[0] + i * main[1], LANES)

    mc = scores(first[0], first[1], s_bufs[0])
    if n_main == 0:
        state = absorb(first[0], first[1], s_bufs[0], mc, state)
    else:
        mc_next = scores(main_start(0), main[1], s_bufs[1])
        state = absorb(first[0], first[1], s_bufs[0], mc, state)
        mc = mc_next

        def triple(i, carry):
            mc, state = carry[0], carry[1:]
            k = 3 * i
            mc2 = scores(main_start(k + 1), main[1], s_bufs[2])
            state = absorb(main_start(k), main[1], s_bufs[1], mc, state)
            mc0 = scores(main_start(k + 2), main[1], s_bufs[0])
            state = absorb(main_start(k + 1), main[1], s_bufs[2], mc2, state)
            mc1 = scores(main_start(k + 3), main[1], s_bufs[1])
            state = absorb(main_start(k + 2), main[1], s_bufs[0], mc0, state)
            return (mc1,) + tuple(state)

        n_loop = (n_main - 1) // 3
        if n_loop > 0:
            carry = lax.fori_loop(0, n_loop, triple, (mc,) + tuple(state))
            mc, state = carry[0], carry[1:]
        last = main_start(n_main - 1)
        if tail[1] > 0:
            mc_next = scores(tail[0], tail[1], s_bufs[2])
            state = absorb(last, main[1], s_bufs[1], mc, state)
            state = absorb(tail[0], tail[1], s_bufs[2], mc_next, state)
        else:
            state = absorb(last, main[1], s_bufs[1], mc, state)
    m, l, acc = state
    o = acc / l

    if mode == "gqa":
        for g in range(ng):
            o_ref[0, g * HEAD_DIM:(g + 1) * HEAD_DIM, :] = o[:, g * tq:(g + 1) * tq].astype(o_ref.dtype)
    elif mode == "mha":
        o_ref[0] = o.astype(o_ref.dtype)
    else:
        lv = lamvec_ref[...]
        lam = (jnp.exp(jnp.sum(lv[0:1] * lv[1:2], axis=1, keepdims=True))
               - jnp.exp(jnp.sum(lv[2:3] * lv[3:4], axis=1, keepdims=True)) + lam_init)
        d = o[:, 0:tq] - lam * o[:, tq:2 * tq]
        ms = jnp.mean(d * d, axis=0, keepdims=True)
        o_ref[0] = (d * lax.rsqrt(ms + EPS) * subg_ref[...] * (1.0 - lam_init)).astype(o_ref.dtype)


def _flash(qT, k, vT, *, mode, S, C, ctx_only, sink=None, lamvec=None, subg=None, lam_init=0.0, out_prev=None):
    B, _, T = qT.shape
    kw = k.shape[2]
    q_rows = 128 if mode == "gqa" else 64
    units = 256 // q_rows
    if ctx_only:
        tq, nq, q_blk0 = C, 1, S // C
        chunks = ((S, C), (0, 0), 0, (0, 0))
    else:
        tq, nq, q_blk0 = 256, S // 256, 0
        tk = FLASH_KEYS if S >= 4 * FLASH_KEYS else 256
        n_main = (S // tk - 1) // 3 * 3 + 1
        chunks = ((S, C), (0, tk), n_main, (n_main * tk, S - n_main * tk))
    max_keys = max(c[1] for c in (chunks[0], chunks[1], chunks[3]))
    in_specs = [
        pl.BlockSpec((1, q_rows, tq), lambda b, u, i: (b, u, i + q_blk0)),
        pl.BlockSpec((1, T, kw), lambda b, u, i: (b, 0, 0)),
        pl.BlockSpec((1, HEAD_DIM, T), lambda b, u, i: (b, u, 0)),
    ]
    args = [qT, k, vT]
    if sink is not None:
        in_specs.append(pl.BlockSpec(memory_space=pltpu.SMEM))
        args.append(sink)
    if mode == "diff":
        in_specs += [pl.BlockSpec((8, LANES), lambda b, u, i: (0, 0)),
                     pl.BlockSpec((HEAD_DIM, 1), lambda b, u, i: (0, 0))]
        args += [lamvec, subg]
    aliases = {}
    if out_prev is not None:
        in_specs.append(pl.BlockSpec(memory_space=pl.ANY))
        aliases = {len(args): 0}
        args.append(out_prev)
    kern = functools.partial(_flash_kernel, mode=mode, kw=kw, tq=tq, max_keys=max_keys, chunks=chunks,
                             use_sink=sink is not None, lam_init=lam_init,
                             aliased=out_prev is not None)
    return pl.pallas_call(
        kern,
        grid=(B, units, nq),
        in_specs=in_specs,
        out_specs=pl.BlockSpec((1, q_rows, tq), lambda b, u, i: (b, u, i + q_blk0)),
        out_shape=jax.ShapeDtypeStruct((B, 256, T), BF16),
        input_output_aliases=aliases,
        scratch_shapes=[pltpu.VMEM((max_keys, len(_PIECES[mode]) * tq), F32)] * 3,
        compiler_params=_cparams(("arbitrary", "arbitrary", "arbitrary")),
        name="flash_" + mode + ("_ctx" if ctx_only else ""),
    )(*args)


WINDOW_BLOCKS = 2


def _window_kernel(qT_ref, k_ref, vT_ref, sink_ref, o_ref, *, S, C):
    step = pl.program_id(1)
    qb = WINDOW
    span = 3 * qb
    n = 2 * qb
    k_ctx = k_ref[0, S:S + C, :]
    for bi in range(WINDOW_BLOCKS):
        i = step * WINDOW_BLOCKS + bi
        cols = slice(bi * qb, (bi + 1) * qb)
        start = pl.multiple_of(jnp.clip((i - 1) * qb, 0, S - span), LANES)
        k_loc = k_ref[0, pl.ds(start, span), :]
        kpos = start + lax.broadcasted_iota(jnp.int32, (span, n), 0)
        qpos = i * qb + lax.broadcasted_iota(jnp.int32, (span, n), 1) % qb
        in_window = jnp.abs(kpos - qpos) <= WINDOW
        for unit in range(2):
            q = qT_ref[0, unit * 128:(unit + 1) * 128, cols].astype(F32)
            qpad = _pad_queries(q, _PIECES["gqa"], LANES, unit)
            s_loc = jnp.where(in_window, _dot(k_loc, qpad), NEG_INF)
            s_ctx = _dot(k_ctx, qpad)
            sink = jnp.concatenate(
                [jnp.full((1, qb), sink_ref[2 * unit + g] * LOG2E, F32) for g in range(2)], axis=1)
            m = jnp.maximum(jnp.maximum(jnp.max(s_loc, axis=0, keepdims=True),
                                        jnp.max(s_ctx, axis=0, keepdims=True)), sink)
            e_loc = jnp.exp2(s_loc - m)
            e_ctx = jnp.exp2(s_ctx - m)
            den = (jnp.sum(e_loc, axis=0, keepdims=True) + jnp.sum(e_ctx, axis=0, keepdims=True)
                   + jnp.exp2(sink - m))
            v_rows = slice(unit * HEAD_DIM, (unit + 1) * HEAD_DIM)
            o = (_dot(vT_ref[0, v_rows, pl.ds(start, span)], e_loc.astype(BF16))
                 + _dot(vT_ref[0, v_rows, S:S + C], e_ctx.astype(BF16))) / den
            for g in range(2):
                head = 2 * unit + g
                o_ref[0, head * HEAD_DIM:(head + 1) * HEAD_DIM, cols] = (
                    o[:, g * qb:(g + 1) * qb].astype(o_ref.dtype))


def _window(qT, k, vT, sink, S, C):
    B, _, T = qT.shape
    tq = WINDOW * WINDOW_BLOCKS
    return pl.pallas_call(
        functools.partial(_window_kernel, S=S, C=C),
        grid=(B, S // tq),
        in_specs=[
            pl.BlockSpec((1, 256, tq), lambda b, i: (b, 0, i)),
            pl.BlockSpec((1, T, LANES), lambda b, i: (b, 0, 0)),
            pl.BlockSpec((1, 2 * HEAD_DIM, T), lambda b, i: (b, 0, 0)),
            pl.BlockSpec(memory_space=pltpu.SMEM),
        ],
        out_specs=pl.BlockSpec((1, 256, tq), lambda b, i: (b, 0, i)),
        out_shape=jax.ShapeDtypeStruct((B, 256, T), BF16),
        compiler_params=_cparams(("arbitrary", "arbitrary")),
        name="window_attn",
    )(qT, k, vT, sink)


def _nbr_kernel(qT_ref, k_ref, vT_ref, tab_ref, o_ref, *, S, C):
    j = pl.program_id(1)
    rows = S // GRID_W
    n = 2 * GRID_W
    span = NBR_WIN_ROWS * GRID_W
    w0 = jnp.clip(2 * j - NA_ROWS // 2, 0, rows - NBR_WIN_ROWS)
    shift = w0 - 2 * j + NA_ROWS // 2 + 4
    start = pl.multiple_of(w0 * GRID_W, LANES)
    k_loc = k_ref[0, pl.ds(start, span), :]
    k_ctx = k_ref[0, S:S + C, :]
    kr = w0 + lax.broadcasted_iota(jnp.int32, (span, n), 0) // GRID_W
    qr = 2 * j + lax.broadcasted_iota(jnp.int32, (span, n), 1) // GRID_W
    r0 = jnp.clip(qr - NA_ROWS // 2, 0, rows - NA_ROWS)
    in_rows = (kr >= r0) & (kr < r0 + NA_ROWS)
    for head in range(4):
        h_rows = slice(head * HEAD_DIM, (head + 1) * HEAD_DIM)
        qpad = _pad_queries(qT_ref[0, h_rows, :].astype(F32), _PIECES["mha"], 256, head)
        bias = tab_ref[head, pl.ds(pl.multiple_of(shift * GRID_W, GRID_W), span), :]
        s_loc = jnp.where(in_rows, _dot(k_loc, qpad) + bias, NEG_INF)
        s_ctx = _dot(k_ctx, qpad)
        m = jnp.maximum(jnp.max(s_loc, axis=0, keepdims=True), jnp.max(s_ctx, axis=0, keepdims=True))
        e_loc = jnp.exp2(s_loc - m)
        e_ctx = jnp.exp2(s_ctx - m)
        den = jnp.sum(e_loc, axis=0, keepdims=True) + jnp.sum(e_ctx, axis=0, keepdims=True)
        o = (_dot(vT_ref[0, h_rows, pl.ds(start, span)], e_loc.astype(BF16))
             + _dot(vT_ref[0, h_rows, S:S + C], e_ctx.astype(BF16))) / den
        o_ref[0, h_rows, :] = o.astype(o_ref.dtype)


def _nbr_bias_table(rpb):
    u = np.arange(NBR_TABLE_ROWS)[:, None, None, None]
    kc = np.arange(GRID_W)[None, :, None, None]
    e = np.arange(2)[None, None, :, None]
    qc = np.arange(GRID_W)[None, None, None, :]
    dr = u - e - 1
    row_ok = (dr >= 0) & (dr < 2 * NA_ROWS - 1)
    dc = np.clip(kc - qc, -(NA_COLS - 1), NA_COLS - 1) + (NA_COLS - 1)
    c0 = np.clip(qc - NA_COLS // 2, 0, GRID_W - NA_COLS)
    col_ok = (kc >= c0) & (kc < c0 + NA_COLS)
    shape = (NBR_TABLE_ROWS, GRID_W, 2, GRID_W)
    n_dr, n_dc = 2 * NA_ROWS - 1, 2 * NA_COLS - 1
    sel_r = ((dr[:, 0, :, 0, None] == np.arange(n_dr)) & row_ok[:, 0, :, 0, None]).astype(np.float32)
    sel_c = (dc[0, :, 0, :, None] == np.arange(n_dc)).astype(np.float32)
    vals = jnp.einsum("uer,hrc,kqc->hukeq", sel_r, rpb.astype(F32) * LOG2E, sel_c,
                      precision=lax.Precision.HIGHEST)
    vals = jnp.where(np.broadcast_to(col_ok, shape), vals, NEG_INF)
    return vals.reshape(rpb.shape[0], NBR_TABLE_ROWS * GRID_W, 2 * GRID_W)


def _nbr(qT, k, vT, table, S, C):
    B, _, T = qT.shape
    return pl.pallas_call(
        functools.partial(_nbr_kernel, S=S, C=C),
        grid=(B, S // (2 * GRID_W)),
        in_specs=[
            pl.BlockSpec((1, 256, 2 * GRID_W), lambda b, j: (b, 0, j)),
            pl.BlockSpec((1, T, 256), lambda b, j: (b, 0, 0)),
            pl.BlockSpec((1, 256, T), lambda b, j: (b, 0, 0)),
            pl.BlockSpec((4, NBR_TABLE_ROWS * GRID_W, 2 * GRID_W), lambda b, j: (0, 0, 0)),
        ],
        out_specs=pl.BlockSpec((1, 256, 2 * GRID_W), lambda b, j: (b, 0, j)),
        out_shape=jax.ShapeDtypeStruct((B, 256, T), BF16),
        compiler_params=_cparams(("arbitrary", "arbitrary")),
        name="nbr_attn",
    )(qT, k, vT, table)


ROUTE_W, ROUTE_E, ROUTE_RANK = 0, 4, 8
ROUTE_ROWS = 16
MOE_GROUPS = 1


def _merge_kernel(oa_ref, ob_ref, oc_ref, od_ref, gate_ref, x_ref, mod_ref, wb_ref, wo_ref, n2_ref,
                  rw_ref, rb_ref, tri_ref, xo_ref, h_ref, route_ref, cnt_ref, base_ref, *, group_tiles):
    @pl.when(pl.program_id(0) % group_tiles == 0)
    def _():
        base_ref[...] = jnp.zeros_like(base_ref)

    acc = None
    for nbr, o_ref in enumerate((oa_ref, ob_ref, oc_ref, od_ref)):
        proj = lax.dot_general(o_ref[0], wb_ref[nbr], (((0,), (0,)), ((), ())),
                               preferred_element_type=F32)
        term = gate_ref[:, nbr * D_MODEL:(nbr + 1) * D_MODEL].astype(F32) * proj
        acc = term if acc is None else acc + term
    mix = _dot(acc.astype(BF16), wo_ref[...])
    mod = mod_ref[0]
    g1 = mod[:, 2 * D_MODEL:3 * D_MODEL]
    sh2 = mod[:, 3 * D_MODEL:4 * D_MODEL]
    sc2 = mod[:, 4 * D_MODEL:5 * D_MODEL]
    xn = x_ref[...] + g1 * mix
    xo_ref[...] = xn
    ms = jnp.mean(xn * xn, axis=-1, keepdims=True)
    h = (xn * lax.rsqrt(ms + EPS) * n2_ref[...]) * (1 + sc2) + sh2
    h_ref[...] = h
    logits = lax.dot_general(rw_ref[...], h, (((1,), (1,)), ((), ())), precision=lax.Precision.HIGHEST,
                             preferred_element_type=F32) + rb_ref[...]
    row_f = lax.broadcasted_iota(jnp.int32, logits.shape, 0).astype(F32)
    work = logits
    picks = []
    for _ in range(TOP_K):
        top = jnp.max(work, axis=0, keepdims=True)
        idx = jnp.min(jnp.where(work == top, row_f, float(N_EXPERTS)), axis=0, keepdims=True)
        hit = row_f == idx
        picks.append((top, idx, hit))
        work = jnp.where(hit, -jnp.inf, work)
    ex = [jnp.exp(top - picks[0][0]) for top, _, _ in picks]
    den = ex[0] + ex[1] + ex[2] + ex[3]
    chosen = jnp.zeros(logits.shape, F32)
    for _, _, hit in picks:
        chosen = chosen + hit.astype(F32)
    base = base_ref[...][:, 0:1]
    before = _dot(chosen.astype(BF16), tri_ref[...]) + base
    fields = ([ex[k] / den for k in range(TOP_K)] + [idx for _, idx, _ in picks]
              + [jnp.sum(jnp.where(hit, before, 0.0), axis=0, keepdims=True) for _, _, hit in picks])
    fields.append(jnp.zeros((ROUTE_ROWS - len(fields), logits.shape[1]), F32))
    route_ref[...] = jnp.concatenate(fields, axis=0)
    base_ref[...] = base_ref[...] + jnp.sum(chosen, axis=1, keepdims=True)
    cnt_ref[0] = base_ref[...]


def _merge(oTs, gate, X, mod, wb_bf, wo_bf, norm2_g, rw_t, rb_col, n_rows, B, S, C, row_batch):
    nt = n_rows // ROW_TILE
    per = S // ROW_TILE
    nlat = B * per

    def bidx(t):
        return jnp.where(t < nlat, t // per, t - nlat)

    def pidx(t):
        return jnp.where(t < nlat, t % per, per)

    o_spec = pl.BlockSpec((1, 256, ROW_TILE), lambda t: (bidx(t), 0, pidx(t)))
    r = np.arange(ROW_TILE)
    strict_upper = jnp.asarray((r[:, None] < r[None, :]).astype(np.float32), dtype=BF16)
    group_tiles = nt // MOE_GROUPS
    assert nt % MOE_GROUPS == 0
    return pl.pallas_call(
        functools.partial(_merge_kernel, group_tiles=group_tiles),
        grid=(nt,),
        in_specs=[o_spec, o_spec, o_spec, o_spec,
                  pl.BlockSpec((ROW_TILE, GATE_TOT), lambda t: (t, 0)),
                  pl.BlockSpec((ROW_TILE, D_MODEL), lambda t: (t, 0)),
                  pl.BlockSpec((1, 1, N_MOD * D_MODEL), lambda t: (row_batch(t), 0, 0)),
                  pl.BlockSpec((4, 256, D_MODEL), lambda t: (0, 0, 0)),
                  pl.BlockSpec((D_MODEL, D_MODEL), lambda t: (0, 0)),
                  pl.BlockSpec((1, D_MODEL), lambda t: (0, 0)),
                  pl.BlockSpec((N_EXPERTS, D_MODEL), lambda t: (0, 0)),
                  pl.BlockSpec((N_EXPERTS, 1), lambda t: (0, 0)),
                  pl.BlockSpec((ROW_TILE, ROW_TILE), lambda t: (0, 0))],
        out_specs=[pl.BlockSpec((ROW_TILE, D_MODEL), lambda t: (t, 0)),
                   pl.BlockSpec((ROW_TILE, D_MODEL), lambda t: (t, 0)),
                   pl.BlockSpec((ROUTE_ROWS, ROW_TILE), lambda t: (0, t)),
                   pl.BlockSpec((1, N_EXPERTS, LANES), lambda t: (t // group_tiles, 0, 0))],
        out_shape=[jax.ShapeDtypeStruct((n_rows, D_MODEL), F32),
                   jax.ShapeDtypeStruct((n_rows, D_MODEL), F32),
                   jax.ShapeDtypeStruct((ROUTE_ROWS, n_rows), F32),
                   jax.ShapeDtypeStruct((MOE_GROUPS, N_EXPERTS, LANES), F32)],
        scratch_shapes=[pltpu.VMEM((N_EXPERTS, LANES), F32)],
        compiler_params=_cparams(("arbitrary",)),
        name="merge",
    )(*oTs, gate, X, mod, wb_bf, wo_bf, norm2_g.reshape(1, D_MODEL), rw_t, rb_col, strict_upper)


def _expert_kernel(tok_ref, be_ref, off_ref, nu_ref, h_hbm, wgu_ref, bgu_ref, wd_ref, bd_ref, y_ref,
                   x_a, x_b, sems, wgu_s, wd_s):
    i = pl.program_id(0)
    n_used = nu_ref[0]

    def start_gather(block, x_dst, sem):
        base = off_ref[block]
        for r in range(EXPERT_ROWS):
            pltpu.make_async_copy(h_hbm.at[pl.ds(tok_ref[base + r], 1), :], x_dst.at[pl.ds(r, 1), :],
                                  sem).start()

    def wait_gather(x_dst, sem):
        pltpu.make_async_copy(h_hbm.at[pl.ds(0, EXPERT_ROWS), :], x_dst, sem).wait()

    def block(x_cur, sem_cur, x_nxt, sem_nxt):
        wait_gather(x_cur, sem_cur)
        start_gather(jnp.minimum(i + 1, n_used - 1), x_nxt, sem_nxt)
        x = x_cur[...].astype(BF16)
        gu = _dot(x, wgu_s[...]) + bgu_ref[0]
        gate = jnp.minimum(gu[:, :D_FF], SWIGLU_LIMIT)
        up = jnp.clip(gu[:, D_FF:], -SWIGLU_LIMIT, SWIGLU_LIMIT)
        a = gate * jax.nn.sigmoid(SWIGLU_ALPHA * gate) * (up + 1)
        y_ref[...] = (_dot(a.astype(BF16), wd_s[...]) + bd_ref[0]).astype(y_ref.dtype)

        @pl.when(i == n_used - 1)
        def _():
            wait_gather(x_nxt, sem_nxt)

    @pl.when(i < n_used)
    def _():
        @pl.when(i == 0)
        def _():
            start_gather(0, x_a, sems.at[0])

        prev = be_ref[jnp.maximum(i - 1, 0)]

        @pl.when((i == 0) | (be_ref[i] != prev))
        def _():
            wgu_s[...] = wgu_ref[0].astype(BF16)
            wd_s[...] = wd_ref[0].astype(BF16)

        @pl.when(i % 2 == 0)
        def _():
            block(x_a, sems.at[0], x_b, sems.at[1])

        @pl.when(i % 2 == 1)
        def _():
            block(x_b, sems.at[1], x_a, sems.at[0])


def _experts(h, tok_sorted, blk_e, blk_off, n_used, n_blk, w_gate_up, b_gate_up, w_down, b_down):
    E = w_gate_up.shape[0]

    def row_map(i, tok, be, off, nu):
        return (jnp.maximum(jnp.minimum(i, nu[0] - 1), 0), 0)

    def w_map(i, tok, be, off, nu):
        return (be[i], 0, 0)

    grid_spec = pltpu.PrefetchScalarGridSpec(
        num_scalar_prefetch=4,
        grid=(n_blk,),
        in_specs=[
            pl.BlockSpec(memory_space=pl.ANY),
            pl.BlockSpec((1, D_MODEL, 2 * D_FF), w_map),
            pl.BlockSpec((1, 1, 2 * D_FF), w_map),
            pl.BlockSpec((1, D_FF, D_MODEL), w_map),
            pl.BlockSpec((1, 1, D_MODEL), w_map),
        ],
        out_specs=pl.BlockSpec((EXPERT_ROWS, D_MODEL), row_map),
        scratch_shapes=[pltpu.VMEM((EXPERT_ROWS, D_MODEL), F32), pltpu.VMEM((EXPERT_ROWS, D_MODEL), F32),
                        pltpu.SemaphoreType.DMA((2,)),
                        pltpu.VMEM((D_MODEL, 2 * D_FF), BF16), pltpu.VMEM((D_FF, D_MODEL), BF16)],
    )
    return pl.pallas_call(
        _expert_kernel,
        grid_spec=grid_spec,
        out_shape=jax.ShapeDtypeStruct((n_blk * EXPERT_ROWS, D_MODEL), BF16),
        compiler_params=_cparams(("arbitrary",)),
        name="experts",
    )(tok_sorted, blk_e, blk_off, n_used, h, w_gate_up, b_gate_up.reshape(E, 1, 2 * D_FF), w_down,
      b_down.reshape(E, 1, D_MODEL))


def _combine_kernel(yg_ref, route_ref, x_ref, mod_ref, *rest):
    o_ref = rest[-1]
    route = route_ref[...]
    y = None
    for k in range(TOP_K):
        term = route[:, ROUTE_W + k:ROUTE_W + k + 1] * yg_ref[k].astype(F32)
        y = term if y is None else y + term
    g2 = mod_ref[0][:, 5 * D_MODEL:6 * D_MODEL]
    o_ref[...] = x_ref[...] + g2 * y


def _combine(yg, route, X, mod, row_batch, tile0, out_prev):
    n_rows = X.shape[0]
    in_specs = [pl.BlockSpec((TOP_K, ROW_TILE, D_MODEL), lambda t: (0, t, 0)),
                pl.BlockSpec((ROW_TILE, ROUTE_ROWS), lambda t: (t + tile0, 0)),
                pl.BlockSpec((ROW_TILE, D_MODEL), lambda t: (t + tile0, 0)),
                pl.BlockSpec((1, 1, N_MOD * D_MODEL), lambda t: (row_batch(t + tile0), 0, 0))]
    args = [yg, route, X, mod]
    aliases = {}
    if out_prev is not None:
        in_specs.append(pl.BlockSpec(memory_space=pl.ANY))
        aliases = {len(args): 0}
        args.append(out_prev)
    return pl.pallas_call(
        _combine_kernel,
        grid=(yg.shape[1] // ROW_TILE,),
        in_specs=in_specs,
        out_specs=pl.BlockSpec((ROW_TILE, D_MODEL), lambda t: (t + tile0, 0)),
        out_shape=jax.ShapeDtypeStruct((n_rows, D_MODEL), F32),
        input_output_aliases=aliases,
        compiler_params=_cparams(("arbitrary",)),
        name="moe_combine",
    )(*args)


def _moe(h, route, counts_f, X, mod, row_batch, layer, w_gate_up, b_gate_up, w_down, b_down):
    n_groups = counts_f.shape[0]
    Tg = h.shape[0] // n_groups
    n = Tg * TOP_K
    n_blk = n // EXPERT_ROWS + N_EXPERTS
    experts = jnp.arange(N_EXPERTS, dtype=jnp.int32)
    out = None
    for g in range(n_groups):
        route_g = lax.slice_in_dim(route, g * Tg, (g + 1) * Tg, axis=0)
        top_e = route_g[:, ROUTE_E:ROUTE_E + TOP_K].astype(jnp.int32)
        rank = route_g[:, ROUTE_RANK:ROUTE_RANK + TOP_K].astype(jnp.int32)
        counts = counts_f[g, :, 0].astype(jnp.int32)
        padded = (counts + EXPERT_ROWS - 1) // EXPERT_ROWS * EXPERT_ROWS
        pad_end = jnp.cumsum(padded)
        pad_start = pad_end - padded
        start_of = jnp.sum(jnp.where(top_e[:, :, None] == experts, pad_start, 0), axis=-1)
        dest = start_of + rank
        tok = jnp.broadcast_to(jnp.arange(g * Tg, (g + 1) * Tg, dtype=jnp.int32)[:, None], (Tg, TOP_K))
        _, tok_sorted = lax.sort_key_val(dest.reshape(-1), tok.reshape(-1))
        tok_sorted = jnp.concatenate([tok_sorted, jnp.zeros((EXPERT_ROWS,), jnp.int32)])
        n_used = (pad_end[-1] // EXPERT_ROWS).astype(jnp.int32)
        blk_row = jnp.minimum(jnp.arange(n_blk, dtype=jnp.int32), n_used - 1) * EXPERT_ROWS
        blk_e = jnp.sum((pad_end[None, :] <= blk_row[:, None]).astype(jnp.int32), axis=1)
        blk_e = jnp.minimum(blk_e, N_EXPERTS - 1)
        pad_before = pad_start - (jnp.cumsum(counts) - counts)
        blk_off = blk_row - jnp.sum(jnp.where(blk_e[:, None] == experts, pad_before, 0), axis=-1)
        y = _experts(h, tok_sorted, blk_e + layer * N_EXPERTS, blk_off, n_used.reshape(1), n_blk,
                     w_gate_up, b_gate_up, w_down, b_down)
        yg = y.at[dest.T.reshape(-1)].get(mode="promise_in_bounds").reshape(TOP_K, Tg, D_MODEL)
        out = _combine(yg, route, X, mod, row_batch, g * (Tg // ROW_TILE), out)
    return out


def _rope_tables(S, dim):
    t = jnp.arange(S, dtype=jnp.int32)
    row = (t // GRID_W).astype(F32)
    col = (t % GRID_W).astype(F32)
    n_freq = dim // 4
    inv = ROPE_THETA ** (-jnp.arange(n_freq, dtype=F32) / n_freq)
    ang = jnp.concatenate([row[:, None] * inv, col[:, None] * inv], axis=-1)
    cos, sin = jnp.cos(ang), jnp.sin(ang)
    reps = LANES // dim
    cos_t = jnp.tile(jnp.concatenate([cos, cos], axis=-1), (1, reps))
    sin_t = jnp.tile(jnp.concatenate([-sin, sin], axis=-1), (1, reps))
    cos_t = jnp.concatenate([cos_t, jnp.ones((ROW_TILE, LANES), F32)], axis=0)
    sin_t = jnp.concatenate([sin_t, jnp.zeros((ROW_TILE, LANES), F32)], axis=0)
    return cos_t, sin_t


def _block_ones(seg):
    i = np.arange(LANES)
    return jnp.asarray((i[:, None] // seg == i[None, :] // seg).astype(np.float32), dtype=BF16)


def _lane_tile(v):
    return jnp.tile(v.astype(F32), LANES // v.shape[0])


def kernel(x, c, ctx, c_ctx, norm1_g, norm2_g, w_ada, b_ada, w_in, b_gate, a_qn, a_kn, b_qn, b_kn, lam_q1, lam_k1, lam_q2, lam_k2, subln_g, c_qn, c_kn, sink, d_qn, d_kn, rpb, w_branch, w_out, router_w, router_b, w_gate_up, b_gate_up, w_down, b_down):
    B, S, D = x.shape
    C = ctx.shape[1]
    L = w_in.shape[0]
    assert D == D_MODEL and C == ROW_TILE and S % ROW_TILE == 0 and B + 1 <= 8
    n_lat = B * S
    per = S // ROW_TILE
    nlat_tiles = B * per

    def row_batch(t):
        return jnp.where(t < nlat_tiles, t // per, B)

    cvec = jnp.zeros((8, D), F32).at[:B].set(c).at[B].set(c_ctx)
    mod_all = _ada(cvec, w_ada, b_ada)

    tables = _rope_tables(S, HEAD_DIM) + _rope_tables(S, B_DK)
    ones64, ones32 = _block_ones(HEAD_DIM), _block_ones(B_DK)

    E = w_gate_up.shape[1]
    w_gu_all = w_gate_up.reshape(L * E, D, 2 * D_FF)
    w_dn_all = w_down.reshape(L * E, D_FF, D)

    X = jnp.concatenate([x.reshape(n_lat, D), ctx.reshape(B * C, D)], axis=0)
    for l in range(L):
        last = l == L - 1
        lam_init = 0.8 - 0.6 * math.exp(-0.3 * l)
        mod = mod_all[l].reshape(8, 1, N_MOD * D)
        qkv, gate = _in_proj(X, norm1_g[l], mod, w_in[l].astype(BF16), b_gate[l], row_batch)
        gains = jnp.stack([_lane_tile(g[l]) for g in (a_qn, a_kn, b_qn, b_kn, c_qn, c_kn, d_qn, d_kn)])
        qTs, ks, vTs = _qkv_post(qkv, tables, ones64, ones32, gains, B, S, C)

        lamvec = jnp.zeros((8, LANES), F32)
        for r, v in enumerate((lam_q1, lam_k1, lam_q2, lam_k2)):
            lamvec = lamvec.at[r, :B_DK].set(v[l])
        subg = subln_g[l].reshape(HEAD_DIM, 1)
        sink_l = sink[l].astype(F32)
        nbr_table = _nbr_bias_table(rpb[l])

        o_a = _flash(qTs[0], ks[0], vTs[0], mode="gqa", S=S, C=C, ctx_only=False)
        o_b = _flash(qTs[1], ks[1], vTs[1], mode="diff", S=S, C=C, ctx_only=False,
                     lamvec=lamvec, subg=subg, lam_init=lam_init)
        o_c = _window(qTs[2], ks[2], vTs[2], sink_l, S, C)
        o_d = _nbr(qTs[3], ks[3], vTs[3], nbr_table, S, C)
        if not last:
            o_a = _flash(qTs[0], ks[0], vTs[0], mode="gqa", S=S, C=C, ctx_only=True, out_prev=o_a)
            o_b = _flash(qTs[1], ks[1], vTs[1], mode="diff", S=S, C=C, ctx_only=True,
                         lamvec=lamvec, subg=subg, lam_init=lam_init, out_prev=o_b)
            o_c = _flash(qTs[2], ks[2], vTs[2], mode="gqa", S=S, C=C, ctx_only=True, sink=sink_l, out_prev=o_c)
            o_d = _flash(qTs[3], ks[3], vTs[3], mode="mha", S=S, C=C, ctx_only=True, out_prev=o_d)

        n_rows = n_lat if last else X.shape[0]
        Xm, h2, route_t, counts = _merge((o_a, o_b, o_c, o_d), gate, X, mod, w_branch[l].astype(BF16),
                                         w_out[l].astype(BF16), norm2_g[l], router_w[l].T.astype(F32),
                                         router_b[l].astype(F32).reshape(N_EXPERTS, 1), n_rows, B, S, C,
                                         row_batch)
        X = _moe(h2, route_t.T, counts, Xm, mod, row_batch, l, w_gu_all, b_gate_up.reshape(L * E, -1),
                 w_dn_all, b_down.reshape(L * E, -1))
    return X[:n_lat].reshape(B, S, D)
```

```python
import functools
import math

import numpy as np
import jax
import jax.numpy as jnp
from jax import lax
from jax.experimental import pallas as pl
from jax.experimental.pallas import tpu as pltpu

F32 = jnp.float32
BF16 = jnp.bfloat16

D_MODEL = 1024
GRID_W = 64
HEAD_DIM = 64
B_DK = 32
WINDOW = 128
NA_ROWS = 8
NA_COLS = 16
ROPE_THETA = 10000.0
N_EXPERTS = 32
TOP_K = 4
D_FF = D_MODEL
SWIGLU_LIMIT = 7.0
SWIGLU_ALPHA = 1.702
N_MOD = 6
EPS = 1e-6
NEG_INF = -1e30
LOG2E = 1.4426950408889634

Q_TOT = 1024
KV_TOT = 1536
QKV_TOT = Q_TOT + KV_TOT
GATE_TOT = 4 * D_MODEL

ROW_TILE = 256
LANES = 128
EXPERT_ROWS = 256
NBR_WIN_ROWS = 10
NBR_TABLE_ROWS = NBR_WIN_ROWS + 8
FLASH_KEYS = 1152
SUM_ROWS = 16
VMEM_LIMIT = 52 * 1024 * 1024


def _cparams(sem):
    return pltpu.CompilerParams(dimension_semantics=sem, vmem_limit_bytes=VMEM_LIMIT)


def _dot(a, b):
    return jnp.dot(a, b, preferred_element_type=F32)


def _ada_kernel(c_ref, w_ref, b_ref, o_ref):
    c = c_ref[...]
    s = c * jax.nn.sigmoid(c)
    o_ref[0] = jnp.dot(s, w_ref[0], precision=lax.Precision.HIGHEST,
                       preferred_element_type=F32) + b_ref[0]


def _ada(cvec, w_ada, b_ada):
    L = w_ada.shape[0]
    n_out = w_ada.shape[2]
    tn = 1536
    return pl.pallas_call(
        _ada_kernel,
        grid=(L, n_out // tn),
        in_specs=[
            pl.BlockSpec((8, D_MODEL), lambda l, j: (0, 0)),
            pl.BlockSpec((1, D_MODEL, tn), lambda l, j: (l, 0, j)),
            pl.BlockSpec((1, 1, tn), lambda l, j: (l, 0, j)),
        ],
        out_specs=pl.BlockSpec((1, 8, tn), lambda l, j: (l, 0, j)),
        out_shape=jax.ShapeDtypeStruct((L, 8, n_out), F32),
        compiler_params=_cparams(("arbitrary", "arbitrary")),
        name="ada_mod",
    )(cvec, w_ada, b_ada.reshape(L, 1, n_out))


def _in_kernel(x_ref, g_ref, mod_ref, w_ref, bg_ref, qkv_ref, gate_ref, *, tn):
    x = x_ref[...]
    ms = jnp.mean(x * x, axis=-1, keepdims=True)
    y = x * lax.rsqrt(ms + EPS) * g_ref[...]
    mod = mod_ref[0]
    sh = mod[:, 0:D_MODEL]
    sc = mod[:, D_MODEL:2 * D_MODEL]
    h = (y * (1 + sc) + sh).astype(BF16)
    for j in range(QKV_TOT // tn):
        qkv_ref[:, j * tn:(j + 1) * tn] = _dot(h, w_ref[:, j * tn:(j + 1) * tn]).astype(BF16)
    for j in range(GATE_TOT // tn):
        g = _dot(h, w_ref[:, QKV_TOT + j * tn:QKV_TOT + (j + 1) * tn]) + bg_ref[:, j * tn:(j + 1) * tn]
        gate_ref[:, j * tn:(j + 1) * tn] = jax.nn.sigmoid(g).astype(BF16)


def _in_proj(X, norm_g, mod, w_in_bf, b_gate, row_batch):
    R = X.shape[0]
    nt = R // ROW_TILE
    return pl.pallas_call(
        functools.partial(_in_kernel, tn=512),
        grid=(nt,),
        in_specs=[
            pl.BlockSpec((ROW_TILE, D_MODEL), lambda t: (t, 0)),
            pl.BlockSpec((1, D_MODEL), lambda t: (0, 0)),
            pl.BlockSpec((1, 1, N_MOD * D_MODEL), lambda t: (row_batch(t), 0, 0)),
            pl.BlockSpec((D_MODEL, QKV_TOT + GATE_TOT), lambda t: (0, 0), pipeline_mode=pl.Buffered(1)),
            pl.BlockSpec((1, GATE_TOT), lambda t: (0, 0)),
        ],
        out_specs=[
            pl.BlockSpec((ROW_TILE, QKV_TOT), lambda t: (t, 0)),
            pl.BlockSpec((ROW_TILE, GATE_TOT), lambda t: (t, 0)),
        ],
        out_shape=[
            jax.ShapeDtypeStruct((R, QKV_TOT), BF16),
            jax.ShapeDtypeStruct((R, GATE_TOT), BF16),
        ],
        compiler_params=_cparams(("arbitrary",)),
        name="in_proj",
    )(X, norm_g.reshape(1, D_MODEL), mod, w_in_bf, b_gate.reshape(1, GATE_TOT))


def _seg_rms(x, ones, seg, g):
    sq = x * x
    hi = sq.astype(BF16)
    lo = (sq - hi.astype(F32)).astype(BF16)
    ss = _dot(hi, ones) + _dot(lo, ones)
    return x * lax.rsqrt(ss * (1.0 / seg) + EPS) * g


def _rot_half(y, half):
    lane = lax.broadcasted_iota(jnp.int32, y.shape, 1)
    fwd = pltpu.roll(y, LANES - half, axis=1)
    bwd = pltpu.roll(y, half, axis=1)
    return jnp.where((lane % (2 * half)) < half, fwd, bwd)


def _qkv_post_kernel(qkv_ref, c64_ref, s64_ref, c32_ref, s32_ref, ones64_ref, ones32_ref, gains_ref,
                     qa_ref, qb_ref, qc_ref, qd_ref, ka_ref, kb_ref, kc_ref, kd_ref,
                     va_ref, vb_ref, vc_ref, vd_ref):
    gains = gains_ref[...]
    rope = {64: (c64_ref[...], s64_ref[...]), 32: (c32_ref[...], s32_ref[...])}
    ones = {64: ones64_ref[...], 32: ones32_ref[...]}

    def chunk(col):
        return qkv_ref[:, col:col + LANES].astype(F32)

    def normed(col, seg, gain_row, use_rope, scale):
        y = _seg_rms(chunk(col), ones[seg], seg, gains[gain_row:gain_row + 1, :])
        if use_rope:
            cos, sin = rope[seg]
            y = y * cos + _rot_half(y, seg // 2) * sin
        return y * scale if scale != 1.0 else y

    q_refs = (qa_ref, qb_ref, qc_ref, qd_ref)
    q_seg = (64, 32, 64, 64)
    q_rope = (True, True, True, False)
    for m in range(4):
        scale = float(q_seg[m]) ** -0.5 * LOG2E
        for c in range(2):
            y = normed(m * 256 + c * LANES, q_seg[m], 2 * m, q_rope[m], scale)
            q_refs[m][0, c * LANES:(c + 1) * LANES, :] = y.T.astype(BF16)

    k_refs = (ka_ref, kb_ref, kc_ref, kd_ref)
    v_refs = (va_ref, vb_ref, vc_ref, vd_ref)
    widths = (128, 256, 128, 256)
    kcol = Q_TOT
    vcol = Q_TOT + sum(widths)
    for m in range(4):
        for c in range(widths[m] // LANES):
            y = normed(kcol, q_seg[m], 2 * m + 1, q_rope[m], 1.0)
            k_refs[m][0, :, c * LANES:(c + 1) * LANES] = y.astype(BF16)
            v_refs[m][0, c * LANES:(c + 1) * LANES, :] = chunk(vcol).T.astype(BF16)
            kcol += LANES
            vcol += LANES


def _qkv_post(qkv, tables, ones64, ones32, gains, B, S, C):
    R = qkv.shape[0]
    nt = R // ROW_TILE
    per = S // ROW_TILE
    nlat = B * per
    T = S + C

    def bidx(t):
        return jnp.where(t < nlat, t // per, t - nlat)

    def pidx(t):
        return jnp.where(t < nlat, t % per, per)

    tab_spec = pl.BlockSpec((ROW_TILE, LANES), lambda t: (pidx(t), 0))
    const_spec = pl.BlockSpec((LANES, LANES), lambda t: (0, 0))
    widths = (128, 256, 128, 256)
    q_specs = [pl.BlockSpec((1, 256, ROW_TILE), lambda t: (bidx(t), 0, pidx(t))) for _ in range(4)]
    k_specs = [pl.BlockSpec((1, ROW_TILE, w), lambda t: (bidx(t), pidx(t), 0)) for w in widths]
    v_specs = [pl.BlockSpec((1, w, ROW_TILE), lambda t: (bidx(t), 0, pidx(t))) for w in widths]
    q_shapes = [jax.ShapeDtypeStruct((B, 256, T), BF16) for _ in range(4)]
    k_shapes = [jax.ShapeDtypeStruct((B, T, w), BF16) for w in widths]
    v_shapes = [jax.ShapeDtypeStruct((B, w, T), BF16) for w in widths]
    outs = pl.pallas_call(
        _qkv_post_kernel,
        grid=(nt,),
        in_specs=[pl.BlockSpec((ROW_TILE, QKV_TOT), lambda t: (t, 0)),
                  tab_spec, tab_spec, tab_spec, tab_spec, const_spec, const_spec,
                  pl.BlockSpec((8, LANES), lambda t: (0, 0))],
        out_specs=q_specs + k_specs + v_specs,
        out_shape=q_shapes + k_shapes + v_shapes,
        compiler_params=_cparams(("arbitrary",)),
        name="qkv_post",
    )(qkv, *tables, ones64, ones32, gains)
    return outs[0:4], outs[4:8], outs[8:12]


def _pad_queries(q, pieces, kw, unit):
    tq = q.shape[1]
    r = lax.broadcasted_iota(jnp.int32, (kw, tq), 0)
    blocks = []
    for row0, size, extra in pieces:
        tiled = jnp.concatenate([q[row0:row0 + size, :]] * (kw // size), axis=0)
        off = unit * HEAD_DIM + extra
        blocks.append(jnp.where((r >= off) & (r < off + size), tiled, 0.0))
    out = blocks[0] if len(blocks) == 1 else jnp.concatenate(blocks, axis=1)
    return out.astype(BF16)


_PIECES = {
    "gqa": ((0, 64, 0), (64, 64, 0)),
    "diff": ((0, 32, 0), (32, 32, 32)),
    "mha": ((0, 64, 0),),
}


def _flash_kernel(*refs, mode, kw, tq, max_keys, chunks, use_sink, lam_init, aliased):
    refs = list(refs)
    qT_ref, k_ref, vT_ref = refs[0:3]
    pos = 3
    sink_ref = None
    if use_sink:
        sink_ref = refs[pos]
        pos += 1
    if mode == "diff":
        lamvec_ref, subg_ref = refs[pos:pos + 2]
        pos += 2
    if aliased:
        pos += 1
    o_ref = refs[pos]
    s_bufs = refs[pos + 1:pos + 4]

    unit = pl.program_id(1)
    pieces = _PIECES[mode]
    ng = len(pieces)
    n = ng * tq
    qpad = _pad_queries(qT_ref[0].astype(F32), pieces, kw, unit)

    if use_sink:
        m0 = jnp.concatenate([jnp.full((1, tq), sink_ref[ng * unit + g] * LOG2E, F32) for g in range(ng)],
                             axis=1)
        l0 = jnp.ones((1, n), F32)
    else:
        m0 = jnp.full((1, n), NEG_INF, F32)
        l0 = jnp.zeros((1, n), F32)
    acc0 = jnp.zeros((HEAD_DIM, n), F32)
    ones_rows = jnp.ones((SUM_ROWS, max_keys), BF16)

    def scores(start, size, s_ref):
        s = _dot(k_ref[0, pl.ds(start, size), :], qpad)
        s_ref[0:size, :] = s
        return jnp.max(s, axis=0, keepdims=True)

    def absorb(start, size, s_ref, mc, state):
        m, l, acc = state
        m_new = jnp.maximum(m, mc)
        alpha = jnp.exp2(m - m_new)
        p = jnp.exp2(s_ref[0:size, :] - m_new)
        v_aug = jnp.concatenate([vT_ref[0, :, pl.ds(start, size)], ones_rows[:, 0:size]], axis=0)
        r = _dot(v_aug, p.astype(BF16))
        return m_new, alpha * l + r[HEAD_DIM:HEAD_DIM + 1], alpha * acc + r[0:HEAD_DIM]

    state = (m0, l0, acc0)
    first, main, n_main, tail = chunks

    def main_start(i):
        return pl.multiple_of(main[0] + i * main[1], LANES)

    mc = scores(first[0], first[1], s_bufs[0])
    if n_main == 0:
        state = absorb(first[0], first[1], s_bufs[0], mc, state)
    else:
        mc_next = scores(main_start(0), main[1], s_bufs[1])
        state = absorb(first[0], first[1], s_bufs[0], mc, state)
        mc = mc_next

        def triple(i, carry):
            mc, state = carry[0], carry[1:]
            k = 3 * i
            mc2 = scores(main_start(k + 1), main[1], s_bufs[2])
            state = absorb(main_start(k), main[1], s_bufs[1], mc, state)
            mc0 = scores(main_start(k + 2), main[1], s_bufs[0])
            state = absorb(main_start(k + 1), main[1], s_bufs[2], mc2, state)
            mc1 = scores(main_start(k + 3), main[1], s_bufs[1])
            state = absorb(main_start(k + 2), main[1], s_bufs[0], mc0, state)
            return (mc1,) + tuple(state)

        n_loop = (n_main - 1) // 3
        if n_loop > 0:
            carry = lax.fori_loop(0, n_loop, triple, (mc,) + tuple(state))
            mc, state = carry[0], carry[1:]
        last = main_start(n_main - 1)
        if tail[1] > 0:
            mc_next = scores(tail[0], tail[1], s_bufs[2])
            state = absorb(last, main[1], s_bufs[1], mc, state)
            state = absorb(tail[0], tail[1], s_bufs[2], mc_next, state)
        else:
            state = absorb(last, main[1], s_bufs[1], mc, state)
    m, l, acc = state
    o = acc / l

    if mode == "gqa":
        for g in range(ng):
            o_ref[0, g * HEAD_DIM:(g + 1) * HEAD_DIM, :] = o[:, g * tq:(g + 1) * tq].astype(o_ref.dtype)
    elif mode == "mha":
        o_ref[0] = o.astype(o_ref.dtype)
    else:
        lv = lamvec_ref[...]
        lam = (jnp.exp(jnp.sum(lv[0:1] * lv[1:2], axis=1, keepdims=True))
               - jnp.exp(jnp.sum(lv[2:3] * lv[3:4], axis=1, keepdims=True)) + lam_init)
        d = o[:, 0:tq] - lam * o[:, tq:2 * tq]
        ms = jnp.mean(d * d, axis=0, keepdims=True)
        o_ref[0] = (d * lax.rsqrt(ms + EPS) * subg_ref[...] * (1.0 - lam_init)).astype(o_ref.dtype)


def _flash(qT, k, vT, *, mode, S, C, ctx_only, sink=None, lamvec=None, subg=None, lam_init=0.0, out_prev=None):
    B, _, T = qT.shape
    kw = k.shape[2]
    q_rows = 128 if mode == "gqa" else 64
    units = 256 // q_rows
    if ctx_only:
        tq, nq, q_blk0 = C, 1, S // C
        chunks = ((S, C), (0, 0), 0, (0, 0))
    else:
        tq, nq, q_blk0 = 256, S // 256, 0
        tk = FLASH_KEYS if S >= 4 * FLASH_KEYS else 256
        n_main = (S // tk - 1) // 3 * 3 + 1
        chunks = ((S, C), (0, tk), n_main, (n_main * tk, S - n_main * tk))
    max_keys = max(c[1] for c in (chunks[0], chunks[1], chunks[3]))
    in_specs = [
        pl.BlockSpec((1, q_rows, tq), lambda b, u, i: (b, u, i + q_blk0)),
        pl.BlockSpec((1, T, kw), lambda b, u, i: (b, 0, 0)),
        pl.BlockSpec((1, HEAD_DIM, T), lambda b, u, i: (b, u, 0)),
    ]
    args = [qT, k, vT]
    if sink is not None:
        in_specs.append(pl.BlockSpec(memory_space=pltpu.SMEM))
        args.append(sink)
    if mode == "diff":
        in_specs += [pl.BlockSpec((8, LANES), lambda b, u, i: (0, 0)),
                     pl.BlockSpec((HEAD_DIM, 1), lambda b, u, i: (0, 0))]
        args += [lamvec, subg]
    aliases = {}
    if out_prev is not None:
        in_specs.append(pl.BlockSpec(memory_space=pl.ANY))
        aliases = {len(args): 0}
        args.append(out_prev)
    kern = functools.partial(_flash_kernel, mode=mode, kw=kw, tq=tq, max_keys=max_keys, chunks=chunks,
                             use_sink=sink is not None, lam_init=lam_init,
                             aliased=out_prev is not None)
    return pl.pallas_call(
        kern,
        grid=(B, units, nq),
        in_specs=in_specs,
        out_specs=pl.BlockSpec((1, q_rows, tq), lambda b, u, i: (b, u, i + q_blk0)),
        out_shape=jax.ShapeDtypeStruct((B, 256, T), BF16),
        input_output_aliases=aliases,
        scratch_shapes=[pltpu.VMEM((max_keys, len(_PIECES[mode]) * tq), F32)] * 3,
        compiler_params=_cparams(("arbitrary", "arbitrary", "arbitrary")),
        name="flash_" + mode + ("_ctx" if ctx_only else ""),
    )(*args)


WINDOW_BLOCKS = 2


def _window_kernel(qT_ref, k_ref, vT_ref, sink_ref, o_ref, *, S, C):
    step = pl.program_id(1)
    qb = WINDOW
    span = 3 * qb
    n = 2 * qb
    k_ctx = k_ref[0, S:S + C, :]
    for bi in range(WINDOW_BLOCKS):
        i = step * WINDOW_BLOCKS + bi
        cols = slice(bi * qb, (bi + 1) * qb)
        start = pl.multiple_of(jnp.clip((i - 1) * qb, 0, S - span), LANES)
        k_loc = k_ref[0, pl.ds(start, span), :]
        kpos = start + lax.broadcasted_iota(jnp.int32, (span, n), 0)
        qpos = i * qb + lax.broadcasted_iota(jnp.int32, (span, n), 1) % qb
        in_window = jnp.abs(kpos - qpos) <= WINDOW
        for unit in range(2):
            q = qT_ref[0, unit * 128:(unit + 1) * 128, cols].astype(F32)
            qpad = _pad_queries(q, _PIECES["gqa"], LANES, unit)
            s_loc = jnp.where(in_window, _dot(k_loc, qpad), NEG_INF)
            s_ctx = _dot(k_ctx, qpad)
            sink = jnp.concatenate(
                [jnp.full((1, qb), sink_ref[2 * unit + g] * LOG2E, F32) for g in range(2)], axis=1)
            m = jnp.maximum(jnp.maximum(jnp.max(s_loc, axis=0, keepdims=True),
                                        jnp.max(s_ctx, axis=0, keepdims=True)), sink)
            e_loc = jnp.exp2(s_loc - m)
            e_ctx = jnp.exp2(s_ctx - m)
            den = (jnp.sum(e_loc, axis=0, keepdims=True) + jnp.sum(e_ctx, axis=0, keepdims=True)
                   + jnp.exp2(sink - m))
            v_rows = slice(unit * HEAD_DIM, (unit + 1) * HEAD_DIM)
            o = (_dot(vT_ref[0, v_rows, pl.ds(start, span)], e_loc.astype(BF16))
                 + _dot(vT_ref[0, v_rows, S:S + C], e_ctx.astype(BF16))) / den
            for g in range(2):
                head = 2 * unit + g
                o_ref[0, head * HEAD_DIM:(head + 1) * HEAD_DIM, cols] = (
                    o[:, g * qb:(g + 1) * qb].astype(o_ref.dtype))


def _window(qT, k, vT, sink, S, C):
    B, _, T = qT.shape
    tq = WINDOW * WINDOW_BLOCKS
    return pl.pallas_call(
        functools.partial(_window_kernel, S=S, C=C),
        grid=(B, S // tq),
        in_specs=[
            pl.BlockSpec((1, 256, tq), lambda b, i: (b, 0, i)),
            pl.BlockSpec((1, T, LANES), lambda b, i: (b, 0, 0)),
            pl.BlockSpec((1, 2 * HEAD_DIM, T), lambda b, i: (b, 0, 0)),
            pl.BlockSpec(memory_space=pltpu.SMEM),
        ],
        out_specs=pl.BlockSpec((1, 256, tq), lambda b, i: (b, 0, i)),
        out_shape=jax.ShapeDtypeStruct((B, 256, T), BF16),
        compiler_params=_cparams(("arbitrary", "arbitrary")),
        name="window_attn",
    )(qT, k, vT, sink)


def _nbr_kernel(qT_ref, k_ref, vT_ref, tab_ref, o_ref, *, S, C):
    j = pl.program_id(1)
    rows = S // GRID_W
    n = 2 * GRID_W
    span = NBR_WIN_ROWS * GRID_W
    w0 = jnp.clip(2 * j - NA_ROWS // 2, 0, rows - NBR_WIN_ROWS)
    shift = w0 - 2 * j + NA_ROWS // 2 + 4
    start = pl.multiple_of(w0 * GRID_W, LANES)
    k_loc = k_ref[0, pl.ds(start, span), :]
    k_ctx = k_ref[0, S:S + C, :]
    kr = w0 + lax.broadcasted_iota(jnp.int32, (span, n), 0) // GRID_W
    qr = 2 * j + lax.broadcasted_iota(jnp.int32, (span, n), 1) // GRID_W
    r0 = jnp.clip(qr - NA_ROWS // 2, 0, rows - NA_ROWS)
    in_rows = (kr >= r0) & (kr < r0 + NA_ROWS)
    for head in range(4):
        h_rows = slice(head * HEAD_DIM, (head + 1) * HEAD_DIM)
        qpad = _pad_queries(qT_ref[0, h_rows, :].astype(F32), _PIECES["mha"], 256, head)
        bias = tab_ref[head, pl.ds(pl.multiple_of(shift * GRID_W, GRID_W), span), :]
        s_loc = jnp.where(in_rows, _dot(k_loc, qpad) + bias, NEG_INF)
        s_ctx = _dot(k_ctx, qpad)
        m = jnp.maximum(jnp.max(s_loc, axis=0, keepdims=True), jnp.max(s_ctx, axis=0, keepdims=True))
        e_loc = jnp.exp2(s_loc - m)
        e_ctx = jnp.exp2(s_ctx - m)
        den = jnp.sum(e_loc, axis=0, keepdims=True) + jnp.sum(e_ctx, axis=0, keepdims=True)
        o = (_dot(vT_ref[0, h_rows, pl.ds(start, span)], e_loc.astype(BF16))
             + _dot(vT_ref[0, h_rows, S:S + C], e_ctx.astype(BF16))) / den
        o_ref[0, h_rows, :] = o.astype(o_ref.dtype)


def _nbr_bias_table(rpb):
    u = np.arange(NBR_TABLE_ROWS)[:, None, None, None]
    kc = np.arange(GRID_W)[None, :, None, None]
    e = np.arange(2)[None, None, :, None]
    qc = np.arange(GRID_W)[None, None, None, :]
    dr = u - e - 1
    row_ok = (dr >= 0) & (dr < 2 * NA_ROWS - 1)
    dc = np.clip(kc - qc, -(NA_COLS - 1), NA_COLS - 1) + (NA_COLS - 1)
    c0 = np.clip(qc - NA_COLS // 2, 0, GRID_W - NA_COLS)
    col_ok = (kc >= c0) & (kc < c0 + NA_COLS)
    shape = (NBR_TABLE_ROWS, GRID_W, 2, GRID_W)
    n_dr, n_dc = 2 * NA_ROWS - 1, 2 * NA_COLS - 1
    sel_r = ((dr[:, 0, :, 0, None] == np.arange(n_dr)) & row_ok[:, 0, :, 0, None]).astype(np.float32)
    sel_c = (dc[0, :, 0, :, None] == np.arange(n_dc)).astype(np.float32)
    vals = jnp.einsum("uer,hrc,kqc->hukeq", sel_r, rpb.astype(F32) * LOG2E, sel_c,
                      precision=lax.Precision.HIGHEST)
    vals = jnp.where(np.broadcast_to(col_ok, shape), vals, NEG_INF)
    return vals.reshape(rpb.shape[0], NBR_TABLE_ROWS * GRID_W, 2 * GRID_W)


def _nbr(qT, k, vT, table, S, C):
    B, _, T = qT.shape
    return pl.pallas_call(
        functools.partial(_nbr_kernel, S=S, C=C),
        grid=(B, S // (2 * GRID_W)),
        in_specs=[
            pl.BlockSpec((1, 256, 2 * GRID_W), lambda b, j: (b, 0, j)),
            pl.BlockSpec((1, T, 256), lambda b, j: (b, 0, 0)),
            pl.BlockSpec((1, 256, T), lambda b, j: (b, 0, 0)),
            pl.BlockSpec((4, NBR_TABLE_ROWS * GRID_W, 2 * GRID_W), lambda b, j: (0, 0, 0)),
        ],
        out_specs=pl.BlockSpec((1, 256, 2 * GRID_W), lambda b, j: (b, 0, j)),
        out_shape=jax.ShapeDtypeStruct((B, 256, T), BF16),
        compiler_params=_cparams(("arbitrary", "arbitrary")),
        name="nbr_attn",
    )(qT, k, vT, table)


ROUTE_W, ROUTE_E, ROUTE_RANK = 0, 4, 8
ROUTE_ROWS = 16
MOE_GROUPS = 1


def _merge_kernel(oa_ref, ob_ref, oc_ref, od_ref, gate_ref, x_ref, mod_ref, wb_ref, wo_ref, n2_ref,
                  rw_ref, rb_ref, tri_ref, xo_ref, h_ref, route_ref, cnt_ref, base_ref, *, group_tiles):
    @pl.when(pl.program_id(0) % group_tiles == 0)
    def _():
        base_ref[...] = jnp.zeros_like(base_ref)

    acc = None
    for nbr, o_ref in enumerate((oa_ref, ob_ref, oc_ref, od_ref)):
        proj = lax.dot_general(o_ref[0], wb_ref[nbr], (((0,), (0,)), ((), ())),
                               preferred_element_type=F32)
        term = gate_ref[:, nbr * D_MODEL:(nbr + 1) * D_MODEL].astype(F32) * proj
        acc = term if acc is None else acc + term
    mix = _dot(acc.astype(BF16), wo_ref[...])
    mod = mod_ref[0]
    g1 = mod[:, 2 * D_MODEL:3 * D_MODEL]
    sh2 = mod[:, 3 * D_MODEL:4 * D_MODEL]
    sc2 = mod[:, 4 * D_MODEL:5 * D_MODEL]
    xn = x_ref[...] + g1 * mix
    xo_ref[...] = xn
    ms = jnp.mean(xn * xn, axis=-1, keepdims=True)
    h = (xn * lax.rsqrt(ms + EPS) * n2_ref[...]) * (1 + sc2) + sh2
    h_ref[...] = h
    logits = lax.dot_general(rw_ref[...], h, (((1,), (1,)), ((), ())), precision=lax.Precision.HIGHEST,
                             preferred_element_type=F32) + rb_ref[...]
    row_f = lax.broadcasted_iota(jnp.int32, logits.shape, 0).astype(F32)
    work = logits
    picks = []
    for _ in range(TOP_K):
        top = jnp.max(work, axis=0, keepdims=True)
        idx = jnp.min(jnp.where(work == top, row_f, float(N_EXPERTS)), axis=0, keepdims=True)
        hit = row_f == idx
        picks.append((top, idx, hit))
        work = jnp.where(hit, -jnp.inf, work)
    ex = [jnp.exp(top - picks[0][0]) for top, _, _ in picks]
    den = ex[0] + ex[1] + ex[2] + ex[3]
    chosen = jnp.zeros(logits.shape, F32)
    for _, _, hit in picks:
        chosen = chosen + hit.astype(F32)
    base = base_ref[...][:, 0:1]
    before = _dot(chosen.astype(BF16), tri_ref[...]) + base
    fields = ([ex[k] / den for k in range(TOP_K)] + [idx for _, idx, _ in picks]
              + [jnp.sum(jnp.where(hit, before, 0.0), axis=0, keepdims=True) for _, _, hit in picks])
    fields.append(jnp.zeros((ROUTE_ROWS - len(fields), logits.shape[1]), F32))
    route_ref[...] = jnp.concatenate(fields, axis=0)
    base_ref[...] = base_ref[...] + jnp.sum(chosen, axis=1, keepdims=True)
    cnt_ref[0] = base_ref[...]


def _merge(oTs, gate, X, mod, wb_bf, wo_bf, norm2_g, rw_t, rb_col, n_rows, B, S, C, row_batch):
    nt = n_rows // ROW_TILE
    per = S // ROW_TILE
    nlat = B * per

    def bidx(t):
        return jnp.where(t < nlat, t // per, t - nlat)

    def pidx(t):
        return jnp.where(t < nlat, t % per, per)

    o_spec = pl.BlockSpec((1, 256, ROW_TILE), lambda t: (bidx(t), 0, pidx(t)))
    r = np.arange(ROW_TILE)
    strict_upper = jnp.asarray((r[:, None] < r[None, :]).astype(np.float32), dtype=BF16)
    group_tiles = nt // MOE_GROUPS
    assert nt % MOE_GROUPS == 0
    return pl.pallas_call(
        functools.partial(_merge_kernel, group_tiles=group_tiles),
        grid=(nt,),
        in_specs=[o_spec, o_spec, o_spec, o_spec,
                  pl.BlockSpec((ROW_TILE, GATE_TOT), lambda t: (t, 0)),
                  pl.BlockSpec((ROW_TILE, D_MODEL), lambda t: (t, 0)),
                  pl.BlockSpec((1, 1, N_MOD * D_MODEL), lambda t: (row_batch(t), 0, 0)),
                  pl.BlockSpec((4, 256, D_MODEL), lambda t: (0, 0, 0)),
                  pl.BlockSpec((D_MODEL, D_MODEL), lambda t: (0, 0)),
                  pl.BlockSpec((1, D_MODEL), lambda t: (0, 0)),
                  pl.BlockSpec((N_EXPERTS, D_MODEL), lambda t: (0, 0)),
                  pl.BlockSpec((N_EXPERTS, 1), lambda t: (0, 0)),
                  pl.BlockSpec((ROW_TILE, ROW_TILE), lambda t: (0, 0))],
        out_specs=[pl.BlockSpec((ROW_TILE, D_MODEL), lambda t: (t, 0)),
                   pl.BlockSpec((ROW_TILE, D_MODEL), lambda t: (t, 0)),
                   pl.BlockSpec((ROUTE_ROWS, ROW_TILE), lambda t: (0, t)),
                   pl.BlockSpec((1, N_EXPERTS, LANES), lambda t: (t // group_tiles, 0, 0))],
        out_shape=[jax.ShapeDtypeStruct((n_rows, D_MODEL), F32),
                   jax.ShapeDtypeStruct((n_rows, D_MODEL), F32),
                   jax.ShapeDtypeStruct((ROUTE_ROWS, n_rows), F32),
                   jax.ShapeDtypeStruct((MOE_GROUPS, N_EXPERTS, LANES), F32)],
        scratch_shapes=[pltpu.VMEM((N_EXPERTS, LANES), F32)],
        compiler_params=_cparams(("arbitrary",)),
        name="merge",
    )(*oTs, gate, X, mod, wb_bf, wo_bf, norm2_g.reshape(1, D_MODEL), rw_t, rb_col, strict_upper)


def _expert_kernel(tok_ref, be_ref, off_ref, nu_ref, h_hbm, wgu_ref, bgu_ref, wd_ref, bd_ref, y_ref,
                   x_0, x_1, x_2, sems, wgu_s, wd_s):
    i = pl.program_id(0)
    n_used = nu_ref[0]
    x_bufs = (x_0, x_1, x_2)

    def start_gather(block, x_dst, sem):
        base = off_ref[block]
        for r in range(EXPERT_ROWS):
            pltpu.make_async_copy(h_hbm.at[pl.ds(tok_ref[base + r], 1), :], x_dst.at[pl.ds(r, 1), :],
                                  sem).start()

    def wait_gather(x_dst, sem):
        pltpu.make_async_copy(h_hbm.at[pl.ds(0, EXPERT_ROWS), :], x_dst, sem).wait()

    def block(cur):
        nxt, ahead = (cur + 1) % 3, (cur + 2) % 3
        wait_gather(x_bufs[cur], sems.at[cur])
        start_gather(jnp.minimum(i + 2, n_used - 1), x_bufs[ahead], sems.at[ahead])
        x = x_bufs[cur][...].astype(BF16)
        gu = _dot(x, wgu_s[...]) + bgu_ref[0]
        gate = jnp.minimum(gu[:, :D_FF], SWIGLU_LIMIT)
        up = jnp.clip(gu[:, D_FF:], -SWIGLU_LIMIT, SWIGLU_LIMIT)
        a = gate * jax.nn.sigmoid(SWIGLU_ALPHA * gate) * (up + 1)
        y_ref[...] = (_dot(a.astype(BF16), wd_s[...]) + bd_ref[0]).astype(y_ref.dtype)

        @pl.when(i == n_used - 1)
        def _():
            wait_gather(x_bufs[nxt], sems.at[nxt])
            wait_gather(x_bufs[ahead], sems.at[ahead])

    @pl.when(i < n_used)
    def _():
        @pl.when(i == 0)
        def _():
            start_gather(0, x_bufs[0], sems.at[0])
            start_gather(jnp.minimum(1, n_used - 1), x_bufs[1], sems.at[1])

        prev = be_ref[jnp.maximum(i - 1, 0)]

        @pl.when((i == 0) | (be_ref[i] != prev))
        def _():
            wgu_s[...] = wgu_ref[0].astype(BF16)
            wd_s[...] = wd_ref[0].astype(BF16)

        for cur in range(3):
            pl.when(i % 3 == cur)(functools.partial(block, cur))


def _experts(h, tok_sorted, blk_e, blk_off, n_used, n_blk, w_gate_up, b_gate_up, w_down, b_down):
    E = w_gate_up.shape[0]

    def row_map(i, tok, be, off, nu):
        return (jnp.maximum(jnp.minimum(i, nu[0] - 1), 0), 0)

    def w_map(i, tok, be, off, nu):
        return (be[i], 0, 0)

    grid_spec = pltpu.PrefetchScalarGridSpec(
        num_scalar_prefetch=4,
        grid=(n_blk,),
        in_specs=[
            pl.BlockSpec(memory_space=pl.ANY),
            pl.BlockSpec((1, D_MODEL, 2 * D_FF), w_map),
            pl.BlockSpec((1, 1, 2 * D_FF), w_map),
            pl.BlockSpec((1, D_FF, D_MODEL), w_map),
            pl.BlockSpec((1, 1, D_MODEL), w_map),
        ],
        out_specs=pl.BlockSpec((EXPERT_ROWS, D_MODEL), row_map),
        scratch_shapes=[pltpu.VMEM((EXPERT_ROWS, D_MODEL), F32)] * 3 + [
                        pltpu.SemaphoreType.DMA((3,)),
                        pltpu.VMEM((D_MODEL, 2 * D_FF), BF16), pltpu.VMEM((D_FF, D_MODEL), BF16)],
    )
    return pl.pallas_call(
        _expert_kernel,
        grid_spec=grid_spec,
        out_shape=jax.ShapeDtypeStruct((n_blk * EXPERT_ROWS, D_MODEL), BF16),
        compiler_params=_cparams(("arbitrary",)),
        name="experts",
    )(tok_sorted, blk_e, blk_off, n_used, h, w_gate_up, b_gate_up.reshape(E, 1, 2 * D_FF), w_down,
      b_down.reshape(E, 1, D_MODEL))


def _combine_kernel(yg_ref, route_ref, x_ref, mod_ref, *rest):
    o_ref = rest[-1]
    route = route_ref[...]
    y = None
    for k in range(TOP_K):
        term = route[:, ROUTE_W + k:ROUTE_W + k + 1] * yg_ref[k].astype(F32)
        y = term if y is None else y + term
    g2 = mod_ref[0][:, 5 * D_MODEL:6 * D_MODEL]
    o_ref[...] = x_ref[...] + g2 * y


def _combine(yg, route, X, mod, row_batch, tile0, out_prev):
    n_rows = X.shape[0]
    in_specs = [pl.BlockSpec((TOP_K, ROW_TILE, D_MODEL), lambda t: (0, t, 0)),
                pl.BlockSpec((ROW_TILE, ROUTE_ROWS), lambda t: (t + tile0, 0)),
                pl.BlockSpec((ROW_TILE, D_MODEL), lambda t: (t + tile0, 0)),
                pl.BlockSpec((1, 1, N_MOD * D_MODEL), lambda t: (row_batch(t + tile0), 0, 0))]
    args = [yg, route, X, mod]
    aliases = {}
    if out_prev is not None:
        in_specs.append(pl.BlockSpec(memory_space=pl.ANY))
        aliases = {len(args): 0}
        args.append(out_prev)
    return pl.pallas_call(
        _combine_kernel,
        grid=(yg.shape[1] // ROW_TILE,),
        in_specs=in_specs,
        out_specs=pl.BlockSpec((ROW_TILE, D_MODEL), lambda t: (t + tile0, 0)),
        out_shape=jax.ShapeDtypeStruct((n_rows, D_MODEL), F32),
        input_output_aliases=aliases,
        compiler_params=_cparams(("arbitrary",)),
        name="moe_combine",
    )(*args)


def _moe(h, route, counts_f, X, mod, row_batch, layer, w_gate_up, b_gate_up, w_down, b_down):
    n_groups = counts_f.shape[0]
    Tg = h.shape[0] // n_groups
    n = Tg * TOP_K
    n_blk = n // EXPERT_ROWS + N_EXPERTS
    experts = jnp.arange(N_EXPERTS, dtype=jnp.int32)
    out = None
    for g in range(n_groups):
        route_g = lax.slice_in_dim(route, g * Tg, (g + 1) * Tg, axis=0)
        top_e = route_g[:, ROUTE_E:ROUTE_E + TOP_K].astype(jnp.int32)
        rank = route_g[:, ROUTE_RANK:ROUTE_RANK + TOP_K].astype(jnp.int32)
        counts = counts_f[g, :, 0].astype(jnp.int32)
        padded = (counts + EXPERT_ROWS - 1) // EXPERT_ROWS * EXPERT_ROWS
        pad_end = jnp.cumsum(padded)
        pad_start = pad_end - padded
        start_of = jnp.sum(jnp.where(top_e[:, :, None] == experts, pad_start, 0), axis=-1)
        dest = start_of + rank
        tok = jnp.broadcast_to(jnp.arange(g * Tg, (g + 1) * Tg, dtype=jnp.int32)[:, None], (Tg, TOP_K))
        _, tok_sorted = lax.sort_key_val(dest.reshape(-1), tok.reshape(-1))
        tok_sorted = jnp.concatenate([tok_sorted, jnp.zeros((EXPERT_ROWS,), jnp.int32)])
        n_used = (pad_end[-1] // EXPERT_ROWS).astype(jnp.int32)
        blk_row = jnp.minimum(jnp.arange(n_blk, dtype=jnp.int32), n_used - 1) * EXPERT_ROWS
        blk_e = jnp.sum((pad_end[None, :] <= blk_row[:, None]).astype(jnp.int32), axis=1)
        blk_e = jnp.minimum(blk_e, N_EXPERTS - 1)
        pad_before = pad_start - (jnp.cumsum(counts) - counts)
        blk_off = blk_row - jnp.sum(jnp.where(blk_e[:, None] == experts, pad_before, 0), axis=-1)
        y = _experts(h, tok_sorted, blk_e + layer * N_EXPERTS, blk_off, n_used.reshape(1), n_blk,
                     w_gate_up, b_gate_up, w_down, b_down)
        yg = y.at[dest.T.reshape(-1)].get(mode="promise_in_bounds").reshape(TOP_K, Tg, D_MODEL)
        out = _combine(yg, route, X, mod, row_batch, g * (Tg // ROW_TILE), out)
    return out


def _rope_tables(S, dim):
    t = jnp.arange(S, dtype=jnp.int32)
    row = (t // GRID_W).astype(F32)
    col = (t % GRID_W).astype(F32)
    n_freq = dim // 4
    inv = ROPE_THETA ** (-jnp.arange(n_freq, dtype=F32) / n_freq)
    ang = jnp.concatenate([row[:, None] * inv, col[:, None] * inv], axis=-1)
    cos, sin = jnp.cos(ang), jnp.sin(ang)
    reps = LANES // dim
    cos_t = jnp.tile(jnp.concatenate([cos, cos], axis=-1), (1, reps))
    sin_t = jnp.tile(jnp.concatenate([-sin, sin], axis=-1), (1, reps))
    cos_t = jnp.concatenate([cos_t, jnp.ones((ROW_TILE, LANES), F32)], axis=0)
    sin_t = jnp.concatenate([sin_t, jnp.zeros((ROW_TILE, LANES), F32)], axis=0)
    return cos_t, sin_t


def _block_ones(seg):
    i = np.arange(LANES)
    return jnp.asarray((i[:, None] // seg == i[None, :] // seg).astype(np.float32), dtype=BF16)


def _lane_tile(v):
    return jnp.tile(v.astype(F32), LANES // v.shape[0])


def kernel(x, c, ctx, c_ctx, norm1_g, norm2_g, w_ada, b_ada, w_in, b_gate, a_qn, a_kn, b_qn, b_kn, lam_q1, lam_k1, lam_q2, lam_k2, subln_g, c_qn, c_kn, sink, d_qn, d_kn, rpb, w_branch, w_out, router_w, router_b, w_gate_up, b_gate_up, w_down, b_down):
    B, S, D = x.shape
    C = ctx.shape[1]
    L = w_in.shape[0]
    assert D == D_MODEL and C == ROW_TILE and S % ROW_TILE == 0 and B + 1 <= 8
    n_lat = B * S
    per = S // ROW_TILE
    nlat_tiles = B * per

    def row_batch(t):
        return jnp.where(t < nlat_tiles, t // per, B)

    cvec = jnp.zeros((8, D), F32).at[:B].set(c).at[B].set(c_ctx)
    mod_all = _ada(cvec, w_ada, b_ada)

    tables = _rope_tables(S, HEAD_DIM) + _rope_tables(S, B_DK)
    ones64, ones32 = _block_ones(HEAD_DIM), _block_ones(B_DK)

    E = w_gate_up.shape[1]
    w_gu_all = w_gate_up.reshape(L * E, D, 2 * D_FF)
    w_dn_all = w_down.reshape(L * E, D_FF, D)

    X = jnp.concatenate([x.reshape(n_lat, D), ctx.reshape(B * C, D)], axis=0)
    for l in range(L):
        last = l == L - 1
        lam_init = 0.8 - 0.6 * math.exp(-0.3 * l)
        mod = mod_all[l].reshape(8, 1, N_MOD * D)
        qkv, gate = _in_proj(X, norm1_g[l], mod, w_in[l].astype(BF16), b_gate[l], row_batch)
        gains = jnp.stack([_lane_tile(g[l]) for g in (a_qn, a_kn, b_qn, b_kn, c_qn, c_kn, d_qn, d_kn)])
        qTs, ks, vTs = _qkv_post(qkv, tables, ones64, ones32, gains, B, S, C)

        lamvec = jnp.zeros((8, LANES), F32)
        for r, v in enumerate((lam_q1, lam_k1, lam_q2, lam_k2)):
            lamvec = lamvec.at[r, :B_DK].set(v[l])
        subg = subln_g[l].reshape(HEAD_DIM, 1)
        sink_l = sink[l].astype(F32)
        nbr_table = _nbr_bias_table(rpb[l])

        o_a = _flash(qTs[0], ks[0], vTs[0], mode="gqa", S=S, C=C, ctx_only=False)
        o_b = _flash(qTs[1], ks[1], vTs[1], mode="diff", S=S, C=C, ctx_only=False,
                     lamvec=lamvec, subg=subg, lam_init=lam_init)
        o_c = _window(qTs[2], ks[2], vTs[2], sink_l, S, C)
        o_d = _nbr(qTs[3], ks[3], vTs[3], nbr_table, S, C)
        if not last:
            o_a = _flash(qTs[0], ks[0], vTs[0], mode="gqa", S=S, C=C, ctx_only=True, out_prev=o_a)
            o_b = _flash(qTs[1], ks[1], vTs[1], mode="diff", S=S, C=C, ctx_only=True,
                         lamvec=lamvec, subg=subg, lam_init=lam_init, out_prev=o_b)
            o_c = _flash(qTs[2], ks[2], vTs[2], mode="gqa", S=S, C=C, ctx_only=True, sink=sink_l, out_prev=o_c)
            o_d = _flash(qTs[3], ks[3], vTs[3], mode="mha", S=S, C=C, ctx_only=True, out_prev=o_d)

        n_rows = n_lat if last else X.shape[0]
        Xm, h2, route_t, counts = _merge((o_a, o_b, o_c, o_d), gate, X, mod, w_branch[l].astype(BF16),
                                         w_out[l].astype(BF16), norm2_g[l], router_w[l].T.astype(F32),
                                         router_b[l].astype(F32).reshape(N_EXPERTS, 1), n_rows, B, S, C,
                                         row_batch)
        X = _moe(h2, route_t.T, counts, Xm, mod, row_batch, l, w_gu_all, b_gate_up.reshape(L * E, -1),
                 w_dn_all, b_down.reshape(L * E, -1))
    return X[:n_lat].reshape(B, S, D)
```

```python
import functools
import math

import numpy as np
import jax
import jax.numpy as jnp
from jax import lax
from jax.experimental import pallas as pl
from jax.experimental.pallas import tpu as pltpu

F32 = jnp.float32
BF16 = jnp.bfloat16

D_MODEL = 1024
GRID_W = 64
HEAD_DIM = 64
B_DK = 32
WINDOW = 128
NA_ROWS = 8
NA_COLS = 16
ROPE_THETA = 10000.0
N_EXPERTS = 32
TOP_K = 4
D_FF = D_MODEL
SWIGLU_LIMIT = 7.0
SWIGLU_ALPHA = 1.702
N_MOD = 6
EPS = 1e-6
NEG_INF = -1e30
LOG2E = 1.4426950408889634

Q_TOT = 1024
KV_TOT = 1536
QKV_TOT = Q_TOT + KV_TOT
GATE_TOT = 4 * D_MODEL

ROW_TILE = 256
LANES = 128
EXPERT_ROWS = 256
NBR_WIN_ROWS = 10
NBR_TABLE_ROWS = NBR_WIN_ROWS + 8
FLASH_KEYS = 1152
FLASH_QUERIES = 512
SUM_ROWS = 16
VMEM_LIMIT = 52 * 1024 * 1024


def _cparams(sem):
    return pltpu.CompilerParams(dimension_semantics=sem, vmem_limit_bytes=VMEM_LIMIT)


def _dot(a, b):
    return jnp.dot(a, b, preferred_element_type=F32)


def _ada_kernel(c_ref, w_ref, b_ref, o_ref):
    c = c_ref[...]
    s = c * jax.nn.sigmoid(c)
    o_ref[0] = jnp.dot(s, w_ref[0], precision=lax.Precision.HIGHEST,
                       preferred_element_type=F32) + b_ref[0]


def _ada(cvec, w_ada, b_ada):
    L = w_ada.shape[0]
    n_out = w_ada.shape[2]
    tn = 1536
    return pl.pallas_call(
        _ada_kernel,
        grid=(L, n_out // tn),
        in_specs=[
            pl.BlockSpec((8, D_MODEL), lambda l, j: (0, 0)),
            pl.BlockSpec((1, D_MODEL, tn), lambda l, j: (l, 0, j)),
            pl.BlockSpec((1, 1, tn), lambda l, j: (l, 0, j)),
        ],
        out_specs=pl.BlockSpec((1, 8, tn), lambda l, j: (l, 0, j)),
        out_shape=jax.ShapeDtypeStruct((L, 8, n_out), F32),
        compiler_params=_cparams(("arbitrary", "arbitrary")),
        name="ada_mod",
    )(cvec, w_ada, b_ada.reshape(L, 1, n_out))


def _in_kernel(x_ref, g_ref, mod_ref, w_ref, bg_ref, qkv_ref, gate_ref, *, tn):
    x = x_ref[...]
    ms = jnp.mean(x * x, axis=-1, keepdims=True)
    y = x * lax.rsqrt(ms + EPS) * g_ref[...]
    mod = mod_ref[0]
    sh = mod[:, 0:D_MODEL]
    sc = mod[:, D_MODEL:2 * D_MODEL]
    h = (y * (1 + sc) + sh).astype(BF16)
    for j in range(QKV_TOT // tn):
        qkv_ref[:, j * tn:(j + 1) * tn] = _dot(h, w_ref[:, j * tn:(j + 1) * tn]).astype(BF16)
    for j in range(GATE_TOT // tn):
        g = _dot(h, w_ref[:, QKV_TOT + j * tn:QKV_TOT + (j + 1) * tn]) + bg_ref[:, j * tn:(j + 1) * tn]
        gate_ref[:, j * tn:(j + 1) * tn] = jax.nn.sigmoid(g).astype(BF16)


def _in_proj(X, norm_g, mod, w_in_bf, b_gate, row_batch):
    R = X.shape[0]
    nt = R // ROW_TILE
    return pl.pallas_call(
        functools.partial(_in_kernel, tn=512),
        grid=(nt,),
        in_specs=[
            pl.BlockSpec((ROW_TILE, D_MODEL), lambda t: (t, 0)),
            pl.BlockSpec((1, D_MODEL), lambda t: (0, 0)),
            pl.BlockSpec((1, 1, N_MOD * D_MODEL), lambda t: (row_batch(t), 0, 0)),
            pl.BlockSpec((D_MODEL, QKV_TOT + GATE_TOT), lambda t: (0, 0), pipeline_mode=pl.Buffered(1)),
            pl.BlockSpec((1, GATE_TOT), lambda t: (0, 0)),
        ],
        out_specs=[
            pl.BlockSpec((ROW_TILE, QKV_TOT), lambda t: (t, 0)),
            pl.BlockSpec((ROW_TILE, GATE_TOT), lambda t: (t, 0)),
        ],
        out_shape=[
            jax.ShapeDtypeStruct((R, QKV_TOT), BF16),
            jax.ShapeDtypeStruct((R, GATE_TOT), BF16),
        ],
        compiler_params=_cparams(("arbitrary",)),
        name="in_proj",
    )(X, norm_g.reshape(1, D_MODEL), mod, w_in_bf, b_gate.reshape(1, GATE_TOT))


def _seg_rms(x, ones, seg, g):
    sq = x * x
    hi = sq.astype(BF16)
    lo = (sq - hi.astype(F32)).astype(BF16)
    ss = _dot(hi, ones) + _dot(lo, ones)
    return x * lax.rsqrt(ss * (1.0 / seg) + EPS) * g


def _rot_half(y, half):
    lane = lax.broadcasted_iota(jnp.int32, y.shape, 1)
    fwd = pltpu.roll(y, LANES - half, axis=1)
    bwd = pltpu.roll(y, half, axis=1)
    return jnp.where((lane % (2 * half)) < half, fwd, bwd)


def _qkv_post_kernel(qkv_ref, c64_ref, s64_ref, c32_ref, s32_ref, ones64_ref, ones32_ref, gains_ref,
                     qa_ref, qb_ref, qc_ref, qd_ref, ka_ref, kb_ref, kc_ref, kd_ref,
                     va_ref, vb_ref, vc_ref, vd_ref):
    gains = gains_ref[...]
    rope = {64: (c64_ref[...], s64_ref[...]), 32: (c32_ref[...], s32_ref[...])}
    ones = {64: ones64_ref[...], 32: ones32_ref[...]}

    def chunk(col):
        return qkv_ref[:, col:col + LANES].astype(F32)

    def normed(col, seg, gain_row, use_rope, scale):
        y = _seg_rms(chunk(col), ones[seg], seg, gains[gain_row:gain_row + 1, :])
        if use_rope:
            cos, sin = rope[seg]
            y = y * cos + _rot_half(y, seg // 2) * sin
        return y * scale if scale != 1.0 else y

    q_refs = (qa_ref, qb_ref, qc_ref, qd_ref)
    q_seg = (64, 32, 64, 64)
    q_rope = (True, True, True, False)
    for m in range(4):
        scale = float(q_seg[m]) ** -0.5 * LOG2E
        for c in range(2):
            y = normed(m * 256 + c * LANES, q_seg[m], 2 * m, q_rope[m], scale)
            q_refs[m][0, c * LANES:(c + 1) * LANES, :] = y.T.astype(BF16)

    k_refs = (ka_ref, kb_ref, kc_ref, kd_ref)
    v_refs = (va_ref, vb_ref, vc_ref, vd_ref)
    widths = (128, 256, 128, 256)
    kcol = Q_TOT
    vcol = Q_TOT + sum(widths)
    for m in range(4):
        for c in range(widths[m] // LANES):
            y = normed(kcol, q_seg[m], 2 * m + 1, q_rope[m], 1.0)
            k_refs[m][0, :, c * LANES:(c + 1) * LANES] = y.astype(BF16)
            v_refs[m][0, c * LANES:(c + 1) * LANES, :] = chunk(vcol).T.astype(BF16)
            kcol += LANES
            vcol += LANES


def _qkv_post(qkv, tables, ones64, ones32, gains, B, S, C):
    R = qkv.shape[0]
    nt = R // ROW_TILE
    per = S // ROW_TILE
    nlat = B * per
    T = S + C

    def bidx(t):
        return jnp.where(t < nlat, t // per, t - nlat)

    def pidx(t):
        return jnp.where(t < nlat, t % per, per)

    tab_spec = pl.BlockSpec((ROW_TILE, LANES), lambda t: (pidx(t), 0))
    const_spec = pl.BlockSpec((LANES, LANES), lambda t: (0, 0))
    widths = (128, 256, 128, 256)
    q_specs = [pl.BlockSpec((1, 256, ROW_TILE), lambda t: (bidx(t), 0, pidx(t))) for _ in range(4)]
    k_specs = [pl.BlockSpec((1, ROW_TILE, w), lambda t: (bidx(t), pidx(t), 0)) for w in widths]
    v_specs = [pl.BlockSpec((1, w, ROW_TILE), lambda t: (bidx(t), 0, pidx(t))) for w in widths]
    q_shapes = [jax.ShapeDtypeStruct((B, 256, T), BF16) for _ in range(4)]
    k_shapes = [jax.ShapeDtypeStruct((B, T, w), BF16) for w in widths]
    v_shapes = [jax.ShapeDtypeStruct((B, w, T), BF16) for w in widths]
    outs = pl.pallas_call(
        _qkv_post_kernel,
        grid=(nt,),
        in_specs=[pl.BlockSpec((ROW_TILE, QKV_TOT), lambda t: (t, 0)),
                  tab_spec, tab_spec, tab_spec, tab_spec, const_spec, const_spec,
                  pl.BlockSpec((8, LANES), lambda t: (0, 0))],
        out_specs=q_specs + k_specs + v_specs,
        out_shape=q_shapes + k_shapes + v_shapes,
        compiler_params=_cparams(("arbitrary",)),
        name="qkv_post",
    )(qkv, *tables, ones64, ones32, gains)
    return outs[0:4], outs[4:8], outs[8:12]


def _pad_queries(q, pieces, kw, unit):
    tq = q.shape[1]
    r = lax.broadcasted_iota(jnp.int32, (kw, tq), 0)
    blocks = []
    for row0, size, extra in pieces:
        tiled = jnp.concatenate([q[row0:row0 + size, :]] * (kw // size), axis=0)
        off = unit * HEAD_DIM + extra
        blocks.append(jnp.where((r >= off) & (r < off + size), tiled, 0.0))
    out = blocks[0] if len(blocks) == 1 else jnp.concatenate(blocks, axis=1)
    return out.astype(BF16)


_PIECES = {
    "gqa": ((0, 64, 0), (64, 64, 0)),
    "diff": ((0, 32, 0), (32, 32, 32)),
    "mha": ((0, 64, 0),),
}


def _flash_kernel(*refs, mode, kw, tq, max_keys, chunks, use_sink, lam_init, aliased):
    refs = list(refs)
    qT_ref, k_ref, vT_ref = refs[0:3]
    pos = 3
    sink_ref = None
    if use_sink:
        sink_ref = refs[pos]
        pos += 1
    if mode == "diff":
        lamvec_ref, subg_ref = refs[pos:pos + 2]
        pos += 2
    if aliased:
        pos += 1
    o_ref = refs[pos]
    s_bufs = refs[pos + 1:pos + 4]

    unit = pl.program_id(1)
    pieces = _PIECES[mode]
    ng = len(pieces)
    n = ng * tq
    qpad = _pad_queries(qT_ref[0].astype(F32), pieces, kw, unit)

    if use_sink:
        m0 = jnp.concatenate([jnp.full((1, tq), sink_ref[ng * unit + g] * LOG2E, F32) for g in range(ng)],
                             axis=1)
        l0 = jnp.ones((1, n), F32)
    else:
        m0 = jnp.full((1, n), NEG_INF, F32)
        l0 = jnp.zeros((1, n), F32)
    acc0 = jnp.zeros((HEAD_DIM, n), F32)
    ones_rows = jnp.ones((SUM_ROWS, max_keys), BF16)

    def scores(start, size, s_ref):
        s = _dot(k_ref[0, pl.ds(start, size), :], qpad)
        s_ref[0:size, :] = s
        return jnp.max(s, axis=0, keepdims=True)

    def absorb(start, size, s_ref, mc, state):
        m, l, acc = state
        m_new = jnp.maximum(m, mc)
        alpha = jnp.exp2(m - m_new)
        p = jnp.exp2(s_ref[0:size, :] - m_new)
        v_aug = jnp.concatenate([vT_ref[0, :, pl.ds(start, size)], ones_rows[:, 0:size]], axis=0)
        r = _dot(v_aug, p.astype(BF16))
        return m_new, alpha * l + r[HEAD_DIM:HEAD_DIM + 1], alpha * acc + r[0:HEAD_DIM]

    state = (m0, l0, acc0)
    first, main, n_main, tail = chunks

    def main_start(i):
        return pl.multiple_of(main[0] + i * main[1], LANES)

    mc = scores(first[0], first[1], s_bufs[0])
    if n_main == 0:
        state = absorb(first[0], first[1], s_bufs[0], mc, state)
    else:
        mc_next = scores(main_start(0), main[1], s_bufs[1])
        state = absorb(first[0], first[1], s_bufs[0], mc, state)
        mc = mc_next

        def triple(i, carry):
            mc, state = carry[0], carry[1:]
            k = 3 * i
            mc2 = scores(main_start(k + 1), main[1], s_bufs[2])
            state = absorb(main_start(k), main[1], s_bufs[1], mc, state)
            mc0 = scores(main_start(k + 2), main[1], s_bufs[0])
            state = absorb(main_start(k + 1), main[1], s_bufs[2], mc2, state)
            mc1 = scores(main_start(k + 3), main[1], s_bufs[1])
            state = absorb(main_start(k + 2), main[1], s_bufs[0], mc0, state)
            return (mc1,) + tuple(state)

        n_loop = (n_main - 1) // 3
        if n_loop > 0:
            carry = lax.fori_loop(0, n_loop, triple, (mc,) + tuple(state))
            mc, state = carry[0], carry[1:]
        last = main_start(n_main - 1)
        if tail[1] > 0:
            mc_next = scores(tail[0], tail[1], s_bufs[2])
            state = absorb(last, main[1], s_bufs[1], mc, state)
            state = absorb(tail[0], tail[1], s_bufs[2], mc_next, state)
        else:
            state = absorb(last, main[1], s_bufs[1], mc, state)
    m, l, acc = state
    o = acc / l

    if mode == "gqa":
        for g in range(ng):
            o_ref[0, g * HEAD_DIM:(g + 1) * HEAD_DIM, :] = o[:, g * tq:(g + 1) * tq].astype(o_ref.dtype)
    elif mode == "mha":
        o_ref[0] = o.astype(o_ref.dtype)
    else:
        lv = lamvec_ref[...]
        lam = (jnp.exp(jnp.sum(lv[0:1] * lv[1:2], axis=1, keepdims=True))
               - jnp.exp(jnp.sum(lv[2:3] * lv[3:4], axis=1, keepdims=True)) + lam_init)
        d = o[:, 0:tq] - lam * o[:, tq:2 * tq]
        ms = jnp.mean(d * d, axis=0, keepdims=True)
        o_ref[0] = (d * lax.rsqrt(ms + EPS) * subg_ref[...] * (1.0 - lam_init)).astype(o_ref.dtype)


def _flash(qT, k, vT, *, mode, S, C, ctx_only, sink=None, lamvec=None, subg=None, lam_init=0.0, out_prev=None):
    B, _, T = qT.shape
    kw = k.shape[2]
    q_rows = 128 if mode == "gqa" else 64
    units = 256 // q_rows
    if ctx_only:
        tq, nq, q_blk0 = C, 1, S // C
        chunks = ((S, C), (0, 0), 0, (0, 0))
    else:
        tq = FLASH_QUERIES if S % FLASH_QUERIES == 0 else 256
        nq, q_blk0 = S // tq, 0
        tk = FLASH_KEYS if S >= 4 * FLASH_KEYS else 256
        n_main = (S // tk - 1) // 3 * 3 + 1
        chunks = ((S, C), (0, tk), n_main, (n_main * tk, S - n_main * tk))
    max_keys = max(c[1] for c in (chunks[0], chunks[1], chunks[3]))
    in_specs = [
        pl.BlockSpec((1, q_rows, tq), lambda b, u, i: (b, u, i + q_blk0)),
        pl.BlockSpec((1, T, kw), lambda b, u, i: (b, 0, 0)),
        pl.BlockSpec((1, HEAD_DIM, T), lambda b, u, i: (b, u, 0)),
    ]
    args = [qT, k, vT]
    if sink is not None:
        in_specs.append(pl.BlockSpec(memory_space=pltpu.SMEM))
        args.append(sink)
    if mode == "diff":
        in_specs += [pl.BlockSpec((8, LANES), lambda b, u, i: (0, 0)),
                     pl.BlockSpec((HEAD_DIM, 1), lambda b, u, i: (0, 0))]
        args += [lamvec, subg]
    aliases = {}
    if out_prev is not None:
        in_specs.append(pl.BlockSpec(memory_space=pl.ANY))
        aliases = {len(args): 0}
        args.append(out_prev)
    kern = functools.partial(_flash_kernel, mode=mode, kw=kw, tq=tq, max_keys=max_keys, chunks=chunks,
                             use_sink=sink is not None, lam_init=lam_init,
                             aliased=out_prev is not None)
    return pl.pallas_call(
        kern,
        grid=(B, units, nq),
        in_specs=in_specs,
        out_specs=pl.BlockSpec((1, q_rows, tq), lambda b, u, i: (b, u, i + q_blk0)),
        out_shape=jax.ShapeDtypeStruct((B, 256, T), BF16),
        input_output_aliases=aliases,
        scratch_shapes=[pltpu.VMEM((max_keys, len(_PIECES[mode]) * tq), F32)] * 3,
        compiler_params=_cparams(("arbitrary", "arbitrary", "arbitrary")),
        name="flash_" + mode + ("_ctx" if ctx_only else ""),
    )(*args)


WINDOW_BLOCKS = 2


def _window_kernel(qT_ref, k_ref, vT_ref, sink_ref, o_ref, *, S, C):
    step = pl.program_id(1)
    qb = WINDOW
    span = 3 * qb
    n = 2 * qb
    k_ctx = k_ref[0, S:S + C, :]
    for bi in range(WINDOW_BLOCKS):
        i = step * WINDOW_BLOCKS + bi
        cols = slice(bi * qb, (bi + 1) * qb)
        start = pl.multiple_of(jnp.clip((i - 1) * qb, 0, S - span), LANES)
        k_loc = k_ref[0, pl.ds(start, span), :]
        kpos = start + lax.broadcasted_iota(jnp.int32, (span, n), 0)
        qpos = i * qb + lax.broadcasted_iota(jnp.int32, (span, n), 1) % qb
        in_window = jnp.abs(kpos - qpos) <= WINDOW
        for unit in range(2):
            q = qT_ref[0, unit * 128:(unit + 1) * 128, cols].astype(F32)
            qpad = _pad_queries(q, _PIECES["gqa"], LANES, unit)
            s_loc = jnp.where(in_window, _dot(k_loc, qpad), NEG_INF)
            s_ctx = _dot(k_ctx, qpad)
            sink = jnp.concatenate(
                [jnp.full((1, qb), sink_ref[2 * unit + g] * LOG2E, F32) for g in range(2)], axis=1)
            m = jnp.maximum(jnp.maximum(jnp.max(s_loc, axis=0, keepdims=True),
                                        jnp.max(s_ctx, axis=0, keepdims=True)), sink)
            e_loc = jnp.exp2(s_loc - m)
            e_ctx = jnp.exp2(s_ctx - m)
            den = (jnp.sum(e_loc, axis=0, keepdims=True) + jnp.sum(e_ctx, axis=0, keepdims=True)
                   + jnp.exp2(sink - m))
            v_rows = slice(unit * HEAD_DIM, (unit + 1) * HEAD_DIM)
            o = (_dot(vT_ref[0, v_rows, pl.ds(start, span)], e_loc.astype(BF16))
                 + _dot(vT_ref[0, v_rows, S:S + C], e_ctx.astype(BF16))) / den
            for g in range(2):
                head = 2 * unit + g
                o_ref[0, head * HEAD_DIM:(head + 1) * HEAD_DIM, cols] = (
                    o[:, g * qb:(g + 1) * qb].astype(o_ref.dtype))


def _window(qT, k, vT, sink, S, C):
    B, _, T = qT.shape
    tq = WINDOW * WINDOW_BLOCKS
    return pl.pallas_call(
        functools.partial(_window_kernel, S=S, C=C),
        grid=(B, S // tq),
        in_specs=[
            pl.BlockSpec((1, 256, tq), lambda b, i: (b, 0, i)),
            pl.BlockSpec((1, T, LANES), lambda b, i: (b, 0, 0)),
            pl.BlockSpec((1, 2 * HEAD_DIM, T), lambda b, i: (b, 0, 0)),
            pl.BlockSpec(memory_space=pltpu.SMEM),
        ],
        out_specs=pl.BlockSpec((1, 256, tq), lambda b, i: (b, 0, i)),
        out_shape=jax.ShapeDtypeStruct((B, 256, T), BF16),
        compiler_params=_cparams(("arbitrary", "arbitrary")),
        name="window_attn",
    )(qT, k, vT, sink)


def _nbr_kernel(qT_ref, k_ref, vT_ref, tab_ref, o_ref, *, S, C):
    j = pl.program_id(1)
    rows = S // GRID_W
    n = 2 * GRID_W
    span = NBR_WIN_ROWS * GRID_W
    w0 = jnp.clip(2 * j - NA_ROWS // 2, 0, rows - NBR_WIN_ROWS)
    shift = w0 - 2 * j + NA_ROWS // 2 + 4
    start = pl.multiple_of(w0 * GRID_W, LANES)
    k_loc = k_ref[0, pl.ds(start, span), :]
    k_ctx = k_ref[0, S:S + C, :]
    kr = w0 + lax.broadcasted_iota(jnp.int32, (span, n), 0) // GRID_W
    qr = 2 * j + lax.broadcasted_iota(jnp.int32, (span, n), 1) // GRID_W
    r0 = jnp.clip(qr - NA_ROWS // 2, 0, rows - NA_ROWS)
    in_rows = (kr >= r0) & (kr < r0 + NA_ROWS)
    for head in range(4):
        h_rows = slice(head * HEAD_DIM, (head + 1) * HEAD_DIM)
        qpad = _pad_queries(qT_ref[0, h_rows, :].astype(F32), _PIECES["mha"], 256, head)
        bias = tab_ref[head, pl.ds(pl.multiple_of(shift * GRID_W, GRID_W), span), :]
        s_loc = jnp.where(in_rows, _dot(k_loc, qpad) + bias, NEG_INF)
        s_ctx = _dot(k_ctx, qpad)
        m = jnp.maximum(jnp.max(s_loc, axis=0, keepdims=True), jnp.max(s_ctx, axis=0, keepdims=True))
        e_loc = jnp.exp2(s_loc - m)
        e_ctx = jnp.exp2(s_ctx - m)
        den = jnp.sum(e_loc, axis=0, keepdims=True) + jnp.sum(e_ctx, axis=0, keepdims=True)
        o = (_dot(vT_ref[0, h_rows, pl.ds(start, span)], e_loc.astype(BF16))
             + _dot(vT_ref[0, h_rows, S:S + C], e_ctx.astype(BF16))) / den
        o_ref[0, h_rows, :] = o.astype(o_ref.dtype)


def _nbr_bias_table(rpb):
    u = np.arange(NBR_TABLE_ROWS)[:, None, None, None]
    kc = np.arange(GRID_W)[None, :, None, None]
    e = np.arange(2)[None, None, :, None]
    qc = np.arange(GRID_W)[None, None, None, :]
    dr = u - e - 1
    row_ok = (dr >= 0) & (dr < 2 * NA_ROWS - 1)
    dc = np.clip(kc - qc, -(NA_COLS - 1), NA_COLS - 1) + (NA_COLS - 1)
    c0 = np.clip(qc - NA_COLS // 2, 0, GRID_W - NA_COLS)
    col_ok = (kc >= c0) & (kc < c0 + NA_COLS)
    shape = (NBR_TABLE_ROWS, GRID_W, 2, GRID_W)
    n_dr, n_dc = 2 * NA_ROWS - 1, 2 * NA_COLS - 1
    sel_r = ((dr[:, 0, :, 0, None] == np.arange(n_dr)) & row_ok[:, 0, :, 0, None]).astype(np.float32)
    sel_c = (dc[0, :, 0, :, None] == np.arange(n_dc)).astype(np.float32)
    vals = jnp.einsum("uer,hrc,kqc->hukeq", sel_r, rpb.astype(F32) * LOG2E, sel_c,
                      precision=lax.Precision.HIGHEST)
    vals = jnp.where(np.broadcast_to(col_ok, shape), vals, NEG_INF)
    return vals.reshape(rpb.shape[0], NBR_TABLE_ROWS * GRID_W, 2 * GRID_W)


def _nbr(qT, k, vT, table, S, C):
    B, _, T = qT.shape
    return pl.pallas_call(
        functools.partial(_nbr_kernel, S=S, C=C),
        grid=(B, S // (2 * GRID_W)),
        in_specs=[
            pl.BlockSpec((1, 256, 2 * GRID_W), lambda b, j: (b, 0, j)),
            pl.BlockSpec((1, T, 256), lambda b, j: (b, 0, 0)),
            pl.BlockSpec((1, 256, T), lambda b, j: (b, 0, 0)),
            pl.BlockSpec((4, NBR_TABLE_ROWS * GRID_W, 2 * GRID_W), lambda b, j: (0, 0, 0)),
        ],
        out_specs=pl.BlockSpec((1, 256, 2 * GRID_W), lambda b, j: (b, 0, j)),
        out_shape=jax.ShapeDtypeStruct((B, 256, T), BF16),
        compiler_params=_cparams(("arbitrary", "arbitrary")),
        name="nbr_attn",
    )(qT, k, vT, table)


ROUTE_W, ROUTE_E, ROUTE_RANK = 0, 4, 8
ROUTE_ROWS = 16
MOE_GROUPS = 1


def _merge_kernel(oa_ref, ob_ref, oc_ref, od_ref, gate_ref, x_ref, mod_ref, wb_ref, wo_ref, n2_ref,
                  rw_ref, rb_ref, tri_ref, xo_ref, h_ref, route_ref, cnt_ref, base_ref, *, group_tiles):
    @pl.when(pl.program_id(0) % group_tiles == 0)
    def _():
        base_ref[...] = jnp.zeros_like(base_ref)

    acc = None
    for nbr, o_ref in enumerate((oa_ref, ob_ref, oc_ref, od_ref)):
        proj = lax.dot_general(o_ref[0], wb_ref[nbr], (((0,), (0,)), ((), ())),
                               preferred_element_type=F32)
        term = gate_ref[:, nbr * D_MODEL:(nbr + 1) * D_MODEL].astype(F32) * proj
        acc = term if acc is None else acc + term
    mix = _dot(acc.astype(BF16), wo_ref[...])
    mod = mod_ref[0]
    g1 = mod[:, 2 * D_MODEL:3 * D_MODEL]
    sh2 = mod[:, 3 * D_MODEL:4 * D_MODEL]
    sc2 = mod[:, 4 * D_MODEL:5 * D_MODEL]
    xn = x_ref[...] + g1 * mix
    xo_ref[...] = xn
    ms = jnp.mean(xn * xn, axis=-1, keepdims=True)
    h = (xn * lax.rsqrt(ms + EPS) * n2_ref[...]) * (1 + sc2) + sh2
    h_ref[...] = h
    logits = lax.dot_general(rw_ref[...], h.astype(BF16), (((1,), (1,)), ((), ())),
                             preferred_element_type=F32) + rb_ref[...]
    row_f = lax.broadcasted_iota(jnp.int32, logits.shape, 0).astype(F32)
    work = logits
    picks = []
    for _ in range(TOP_K):
        top = jnp.max(work, axis=0, keepdims=True)
        idx = jnp.min(jnp.where(work == top, row_f, float(N_EXPERTS)), axis=0, keepdims=True)
        hit = row_f == idx
        picks.append((top, idx, hit))
        work = jnp.where(hit, -jnp.inf, work)
    ex = [jnp.exp(top - picks[0][0]) for top, _, _ in picks]
    den = ex[0] + ex[1] + ex[2] + ex[3]
    chosen = jnp.zeros(logits.shape, F32)
    for _, _, hit in picks:
        chosen = chosen + hit.astype(F32)
    base = base_ref[...][:, 0:1]
    before = _dot(chosen.astype(BF16), tri_ref[...]) + base
    fields = ([ex[k] / den for k in range(TOP_K)] + [idx for _, idx, _ in picks]
              + [jnp.sum(jnp.where(hit, before, 0.0), axis=0, keepdims=True) for _, _, hit in picks])
    fields.append(jnp.zeros((ROUTE_ROWS - len(fields), logits.shape[1]), F32))
    route_ref[...] = jnp.concatenate(fields, axis=0)
    base_ref[...] = base_ref[...] + jnp.sum(chosen, axis=1, keepdims=True)
    cnt_ref[0] = base_ref[...]


def _merge(oTs, gate, X, mod, wb_bf, wo_bf, norm2_g, rw_t, rb_col, n_rows, B, S, C, row_batch):
    nt = n_rows // ROW_TILE
    per = S // ROW_TILE
    nlat = B * per

    def bidx(t):
        return jnp.where(t < nlat, t // per, t - nlat)

    def pidx(t):
        return jnp.where(t < nlat, t % per, per)

    o_spec = pl.BlockSpec((1, 256, ROW_TILE), lambda t: (bidx(t), 0, pidx(t)))
    r = np.arange(ROW_TILE)
    strict_upper = jnp.asarray((r[:, None] < r[None, :]).astype(np.float32), dtype=BF16)
    group_tiles = nt // MOE_GROUPS
    assert nt % MOE_GROUPS == 0
    return pl.pallas_call(
        functools.partial(_merge_kernel, group_tiles=group_tiles),
        grid=(nt,),
        in_specs=[o_spec, o_spec, o_spec, o_spec,
                  pl.BlockSpec((ROW_TILE, GATE_TOT), lambda t: (t, 0)),
                  pl.BlockSpec((ROW_TILE, D_MODEL), lambda t: (t, 0)),
                  pl.BlockSpec((1, 1, N_MOD * D_MODEL), lambda t: (row_batch(t), 0, 0)),
                  pl.BlockSpec((4, 256, D_MODEL), lambda t: (0, 0, 0)),
                  pl.BlockSpec((D_MODEL, D_MODEL), lambda t: (0, 0)),
                  pl.BlockSpec((1, D_MODEL), lambda t: (0, 0)),
                  pl.BlockSpec((N_EXPERTS, D_MODEL), lambda t: (0, 0)),
                  pl.BlockSpec((N_EXPERTS, 1), lambda t: (0, 0)),
                  pl.BlockSpec((ROW_TILE, ROW_TILE), lambda t: (0, 0))],
        out_specs=[pl.BlockSpec((ROW_TILE, D_MODEL), lambda t: (t, 0)),
                   pl.BlockSpec((ROW_TILE, D_MODEL), lambda t: (t, 0)),
                   pl.BlockSpec((ROUTE_ROWS, ROW_TILE), lambda t: (0, t)),
                   pl.BlockSpec((1, N_EXPERTS, LANES), lambda t: (t // group_tiles, 0, 0))],
        out_shape=[jax.ShapeDtypeStruct((n_rows, D_MODEL), F32),
                   jax.ShapeDtypeStruct((n_rows, D_MODEL), F32),
                   jax.ShapeDtypeStruct((ROUTE_ROWS, n_rows), F32),
                   jax.ShapeDtypeStruct((MOE_GROUPS, N_EXPERTS, LANES), F32)],
        scratch_shapes=[pltpu.VMEM((N_EXPERTS, LANES), F32)],
        compiler_params=_cparams(("arbitrary",)),
        name="merge",
    )(*oTs, gate, X, mod, wb_bf, wo_bf, norm2_g.reshape(1, D_MODEL), rw_t, rb_col, strict_upper)


def _expert_kernel(tok_ref, be_ref, off_ref, nu_ref, h_hbm, wgu_ref, bgu_ref, wd_ref, bd_ref, y_ref,
                   x_0, x_1, x_2, sems, wgu_s, wd_s):
    i = pl.program_id(0)
    n_used = nu_ref[0]
    x_bufs = (x_0, x_1, x_2)

    def start_gather(block, x_dst, sem):
        base = off_ref[block]
        for r in range(EXPERT_ROWS):
            pltpu.make_async_copy(h_hbm.at[pl.ds(tok_ref[base + r], 1), :], x_dst.at[pl.ds(r, 1), :],
                                  sem).start()

    def wait_gather(x_dst, sem):
        pltpu.make_async_copy(h_hbm.at[pl.ds(0, EXPERT_ROWS), :], x_dst, sem).wait()

    def block(cur):
        nxt, ahead = (cur + 1) % 3, (cur + 2) % 3
        wait_gather(x_bufs[cur], sems.at[cur])
        start_gather(jnp.minimum(i + 2, n_used - 1), x_bufs[ahead], sems.at[ahead])
        x = x_bufs[cur][...].astype(BF16)
        gu = _dot(x, wgu_s[...]) + bgu_ref[0]
        gate = jnp.minimum(gu[:, :D_FF], SWIGLU_LIMIT)
        up = jnp.clip(gu[:, D_FF:], -SWIGLU_LIMIT, SWIGLU_LIMIT)
        a = gate * jax.nn.sigmoid(SWIGLU_ALPHA * gate) * (up + 1)
        y_ref[...] = (_dot(a.astype(BF16), wd_s[...]) + bd_ref[0]).astype(y_ref.dtype)

        @pl.when(i == n_used - 1)
        def _():
            wait_gather(x_bufs[nxt], sems.at[nxt])
            wait_gather(x_bufs[ahead], sems.at[ahead])

    @pl.when(i < n_used)
    def _():
        @pl.when(i == 0)
        def _():
            start_gather(0, x_bufs[0], sems.at[0])
            start_gather(jnp.minimum(1, n_used - 1), x_bufs[1], sems.at[1])

        prev = be_ref[jnp.maximum(i - 1, 0)]

        @pl.when((i == 0) | (be_ref[i] != prev))
        def _():
            wgu_s[...] = wgu_ref[0].astype(BF16)
            wd_s[...] = wd_ref[0].astype(BF16)

        for cur in range(3):
            pl.when(i % 3 == cur)(functools.partial(block, cur))


def _experts(h, tok_sorted, blk_e, blk_off, n_used, n_blk, w_gate_up, b_gate_up, w_down, b_down):
    E = w_gate_up.shape[0]

    def row_map(i, tok, be, off, nu):
        return (jnp.maximum(jnp.minimum(i, nu[0] - 1), 0), 0)

    def w_map(i, tok, be, off, nu):
        return (be[i], 0, 0)

    grid_spec = pltpu.PrefetchScalarGridSpec(
        num_scalar_prefetch=4,
        grid=(n_blk,),
        in_specs=[
            pl.BlockSpec(memory_space=pl.ANY),
            pl.BlockSpec((1, D_MODEL, 2 * D_FF), w_map),
            pl.BlockSpec((1, 1, 2 * D_FF), w_map),
            pl.BlockSpec((1, D_FF, D_MODEL), w_map),
            pl.BlockSpec((1, 1, D_MODEL), w_map),
        ],
        out_specs=pl.BlockSpec((EXPERT_ROWS, D_MODEL), row_map),
        scratch_shapes=[pltpu.VMEM((EXPERT_ROWS, D_MODEL), F32)] * 3 + [
                        pltpu.SemaphoreType.DMA((3,)),
                        pltpu.VMEM((D_MODEL, 2 * D_FF), BF16), pltpu.VMEM((D_FF, D_MODEL), BF16)],
    )
    return pl.pallas_call(
        _expert_kernel,
        grid_spec=grid_spec,
        out_shape=jax.ShapeDtypeStruct((n_blk * EXPERT_ROWS, D_MODEL), BF16),
        compiler_params=_cparams(("arbitrary",)),
        name="experts",
    )(tok_sorted, blk_e, blk_off, n_used, h, w_gate_up, b_gate_up.reshape(E, 1, 2 * D_FF), w_down,
      b_down.reshape(E, 1, D_MODEL))


def _combine_kernel(yg_ref, route_ref, x_ref, mod_ref, *rest):
    o_ref = rest[-1]
    route = route_ref[...]
    y = None
    for k in range(TOP_K):
        term = route[:, ROUTE_W + k:ROUTE_W + k + 1] * yg_ref[k].astype(F32)
        y = term if y is None else y + term
    g2 = mod_ref[0][:, 5 * D_MODEL:6 * D_MODEL]
    o_ref[...] = x_ref[...] + g2 * y


def _combine(yg, route, X, mod, row_batch, tile0, out_prev):
    n_rows = X.shape[0]
    in_specs = [pl.BlockSpec((TOP_K, ROW_TILE, D_MODEL), lambda t: (0, t, 0)),
                pl.BlockSpec((ROW_TILE, ROUTE_ROWS), lambda t: (t + tile0, 0)),
                pl.BlockSpec((ROW_TILE, D_MODEL), lambda t: (t + tile0, 0)),
                pl.BlockSpec((1, 1, N_MOD * D_MODEL), lambda t: (row_batch(t + tile0), 0, 0))]
    args = [yg, route, X, mod]
    aliases = {}
    if out_prev is not None:
        in_specs.append(pl.BlockSpec(memory_space=pl.ANY))
        aliases = {len(args): 0}
        args.append(out_prev)
    return pl.pallas_call(
        _combine_kernel,
        grid=(yg.shape[1] // ROW_TILE,),
        in_specs=in_specs,
        out_specs=pl.BlockSpec((ROW_TILE, D_MODEL), lambda t: (t + tile0, 0)),
        out_shape=jax.ShapeDtypeStruct((n_rows, D_MODEL), F32),
        input_output_aliases=aliases,
        compiler_params=_cparams(("arbitrary",)),
        name="moe_combine",
    )(*args)


def _moe(h, route, counts_f, X, mod, row_batch, layer, w_gate_up, b_gate_up, w_down, b_down):
    n_groups = counts_f.shape[0]
    Tg = h.shape[0] // n_groups
    n = Tg * TOP_K
    n_blk = n // EXPERT_ROWS + N_EXPERTS
    experts = jnp.arange(N_EXPERTS, dtype=jnp.int32)
    out = None
    for g in range(n_groups):
        route_g = lax.slice_in_dim(route, g * Tg, (g + 1) * Tg, axis=0)
        top_e = route_g[:, ROUTE_E:ROUTE_E + TOP_K].astype(jnp.int32)
        rank = route_g[:, ROUTE_RANK:ROUTE_RANK + TOP_K].astype(jnp.int32)
        counts = counts_f[g, :, 0].astype(jnp.int32)
        padded = (counts + EXPERT_ROWS - 1) // EXPERT_ROWS * EXPERT_ROWS
        pad_end = jnp.cumsum(padded)
        pad_start = pad_end - padded
        start_of = jnp.sum(jnp.where(top_e[:, :, None] == experts, pad_start, 0), axis=-1)
        dest = start_of + rank
        tok = jnp.broadcast_to(jnp.arange(g * Tg, (g + 1) * Tg, dtype=jnp.int32)[:, None], (Tg, TOP_K))
        _, tok_sorted = lax.sort_key_val(dest.reshape(-1), tok.reshape(-1))
        tok_sorted = jnp.concatenate([tok_sorted, jnp.zeros((EXPERT_ROWS,), jnp.int32)])
        n_used = (pad_end[-1] // EXPERT_ROWS).astype(jnp.int32)
        blk_row = jnp.minimum(jnp.arange(n_blk, dtype=jnp.int32), n_used - 1) * EXPERT_ROWS
        blk_e = jnp.sum((pad_end[None, :] <= blk_row[:, None]).astype(jnp.int32), axis=1)
        blk_e = jnp.minimum(blk_e, N_EXPERTS - 1)
        pad_before = pad_start - (jnp.cumsum(counts) - counts)
        blk_off = blk_row - jnp.sum(jnp.where(blk_e[:, None] == experts, pad_before, 0), axis=-1)
        y = _experts(h, tok_sorted, blk_e + layer * N_EXPERTS, blk_off, n_used.reshape(1), n_blk,
                     w_gate_up, b_gate_up, w_down, b_down)
        yg = y.at[dest.T.reshape(-1)].get(mode="promise_in_bounds").reshape(TOP_K, Tg, D_MODEL)
        out = _combine(yg, route, X, mod, row_batch, g * (Tg // ROW_TILE), out)
    return out


def _rope_tables(S, dim):
    t = jnp.arange(S, dtype=jnp.int32)
    row = (t // GRID_W).astype(F32)
    col = (t % GRID_W).astype(F32)
    n_freq = dim // 4
    inv = ROPE_THETA ** (-jnp.arange(n_freq, dtype=F32) / n_freq)
    ang = jnp.concatenate([row[:, None] * inv, col[:, None] * inv], axis=-1)
    cos, sin = jnp.cos(ang), jnp.sin(ang)
    reps = LANES // dim
    cos_t = jnp.tile(jnp.concatenate([cos, cos], axis=-1), (1, reps))
    sin_t = jnp.tile(jnp.concatenate([-sin, sin], axis=-1), (1, reps))
    cos_t = jnp.concatenate([cos_t, jnp.ones((ROW_TILE, LANES), F32)], axis=0)
    sin_t = jnp.concatenate([sin_t, jnp.zeros((ROW_TILE, LANES), F32)], axis=0)
    return cos_t, sin_t


def _block_ones(seg):
    i = np.arange(LANES)
    return jnp.asarray((i[:, None] // seg == i[None, :] // seg).astype(np.float32), dtype=BF16)


def _lane_tile(v):
    return jnp.tile(v.astype(F32), LANES // v.shape[0])


def kernel(x, c, ctx, c_ctx, norm1_g, norm2_g, w_ada, b_ada, w_in, b_gate, a_qn, a_kn, b_qn, b_kn, lam_q1, lam_k1, lam_q2, lam_k2, subln_g, c_qn, c_kn, sink, d_qn, d_kn, rpb, w_branch, w_out, router_w, router_b, w_gate_up, b_gate_up, w_down, b_down):
    B, S, D = x.shape
    C = ctx.shape[1]
    L = w_in.shape[0]
    assert D == D_MODEL and C == ROW_TILE and S % ROW_TILE == 0 and B + 1 <= 8
    n_lat = B * S
    per = S // ROW_TILE
    nlat_tiles = B * per

    def row_batch(t):
        return jnp.where(t < nlat_tiles, t // per, B)

    cvec = jnp.zeros((8, D), F32).at[:B].set(c).at[B].set(c_ctx)
    mod_all = _ada(cvec, w_ada, b_ada)

    tables = _rope_tables(S, HEAD_DIM) + _rope_tables(S, B_DK)
    ones64, ones32 = _block_ones(HEAD_DIM), _block_ones(B_DK)

    E = w_gate_up.shape[1]
    w_gu_all = w_gate_up.reshape(L * E, D, 2 * D_FF)
    w_dn_all = w_down.reshape(L * E, D_FF, D)

    X = jnp.concatenate([x.reshape(n_lat, D), ctx.reshape(B * C, D)], axis=0)
    for l in range(L):
        last = l == L - 1
        lam_init = 0.8 - 0.6 * math.exp(-0.3 * l)
        mod = mod_all[l].reshape(8, 1, N_MOD * D)
        qkv, gate = _in_proj(X, norm1_g[l], mod, w_in[l].astype(BF16), b_gate[l], row_batch)
        gains = jnp.stack([_lane_tile(g[l]) for g in (a_qn, a_kn, b_qn, b_kn, c_qn, c_kn, d_qn, d_kn)])
        qTs, ks, vTs = _qkv_post(qkv, tables, ones64, ones32, gains, B, S, C)

        lamvec = jnp.zeros((8, LANES), F32)
        for r, v in enumerate((lam_q1, lam_k1, lam_q2, lam_k2)):
            lamvec = lamvec.at[r, :B_DK].set(v[l])
        subg = subln_g[l].reshape(HEAD_DIM, 1)
        sink_l = sink[l].astype(F32)
        nbr_table = _nbr_bias_table(rpb[l])

        o_a = _flash(qTs[0], ks[0], vTs[0], mode="gqa", S=S, C=C, ctx_only=False)
        o_b = _flash(qTs[1], ks[1], vTs[1], mode="diff", S=S, C=C, ctx_only=False,
                     lamvec=lamvec, subg=subg, lam_init=lam_init)
        o_c = _window(qTs[2], ks[2], vTs[2], sink_l, S, C)
        o_d = _nbr(qTs[3], ks[3], vTs[3], nbr_table, S, C)
        if not last:
            o_a = _flash(qTs[0], ks[0], vTs[0], mode="gqa", S=S, C=C, ctx_only=True, out_prev=o_a)
            o_b = _flash(qTs[1], ks[1], vTs[1], mode="diff", S=S, C=C, ctx_only=True,
                         lamvec=lamvec, subg=subg, lam_init=lam_init, out_prev=o_b)
            o_c = _flash(qTs[2], ks[2], vTs[2], mode="gqa", S=S, C=C, ctx_only=True, sink=sink_l, out_prev=o_c)
            o_d = _flash(qTs[3], ks[3], vTs[3], mode="mha", S=S, C=C, ctx_only=True, out_prev=o_d)

        n_rows = n_lat if last else X.shape[0]
        Xm, h2, route_t, counts = _merge((o_a, o_b, o_c, o_d), gate, X, mod, w_branch[l].astype(BF16),
                                         w_out[l].astype(BF16), norm2_g[l], router_w[l].T.astype(BF16),
                                         router_b[l].astype(F32).reshape(N_EXPERTS, 1), n_rows, B, S, C,
                                         row_batch)
        X = _moe(h2, route_t.T, counts, Xm, mod, row_batch, l, w_gu_all, b_gate_up.reshape(L * E, -1),
                 w_dn_all, b_down.reshape(L * E, -1))
    return X[:n_lat].reshape(B, S, D)
```

```python
import functools
import math

import numpy as np
import jax
import jax.numpy as jnp
from jax import lax
from jax.experimental import pallas as pl
from jax.experimental.pallas import tpu as pltpu

F32 = jnp.float32
BF16 = jnp.bfloat16

D_MODEL = 1024
GRID_W = 64
HEAD_DIM = 64
B_DK = 32
WINDOW = 128
NA_ROWS = 8
NA_COLS = 16
ROPE_THETA = 10000.0
N_EXPERTS = 32
TOP_K = 4
D_FF = D_MODEL
SWIGLU_LIMIT = 7.0
SWIGLU_ALPHA = 1.702
N_MOD = 6
EPS = 1e-6
NEG_INF = -1e30
LOG2E = 1.4426950408889634

Q_TOT = 1024
KV_TOT = 1536
QKV_TOT = Q_TOT + KV_TOT
GATE_TOT = 4 * D_MODEL

ROW_TILE = 256
LANES = 128
EXPERT_ROWS = 256
NBR_Q_ROWS = 4
NBR_WIN_ROWS = NA_ROWS + NBR_Q_ROWS
NBR_TABLE_ROWS = NBR_WIN_ROWS + 8
FLASH_KEYS = 1152
FLASH_QUERIES = 512
SUM_ROWS = 16
VMEM_LIMIT = 52 * 1024 * 1024


def _cparams(sem):
    return pltpu.CompilerParams(dimension_semantics=sem, vmem_limit_bytes=VMEM_LIMIT)


def _dot(a, b):
    return jnp.dot(a, b, preferred_element_type=F32)


def _ada_kernel(c_ref, w_ref, b_ref, o_ref):
    c = c_ref[...]
    s = c * jax.nn.sigmoid(c)
    o_ref[0] = jnp.dot(s, w_ref[0], precision=lax.Precision.HIGHEST,
                       preferred_element_type=F32) + b_ref[0]


def _ada(cvec, w_ada, b_ada):
    L = w_ada.shape[0]
    n_out = w_ada.shape[2]
    tn = 1536
    return pl.pallas_call(
        _ada_kernel,
        grid=(L, n_out // tn),
        in_specs=[
            pl.BlockSpec((8, D_MODEL), lambda l, j: (0, 0)),
            pl.BlockSpec((1, D_MODEL, tn), lambda l, j: (l, 0, j)),
            pl.BlockSpec((1, 1, tn), lambda l, j: (l, 0, j)),
        ],
        out_specs=pl.BlockSpec((1, 8, tn), lambda l, j: (l, 0, j)),
        out_shape=jax.ShapeDtypeStruct((L, 8, n_out), F32),
        compiler_params=_cparams(("arbitrary", "arbitrary")),
        name="ada_mod",
    )(cvec, w_ada, b_ada.reshape(L, 1, n_out))


def _seg_rms(x, ones, seg, g):
    sq = x * x
    hi = sq.astype(BF16)
    lo = (sq - hi.astype(F32)).astype(BF16)
    ss = _dot(hi, ones) + _dot(lo, ones)
    return x * lax.rsqrt(ss * (1.0 / seg) + EPS) * g


def _rot_half(y, half):
    lane = lax.broadcasted_iota(jnp.int32, y.shape, 1)
    fwd = pltpu.roll(y, LANES - half, axis=1)
    bwd = pltpu.roll(y, half, axis=1)
    return jnp.where((lane % (2 * half)) < half, fwd, bwd)


def _in_kernel(x_ref, g_ref, mod_ref, w_ref, bg_ref, c64_ref, s64_ref, c32_ref, s32_ref, ones64_ref,
               ones32_ref, gains_ref, gate_ref, qa_ref, qb_ref, qc_ref, qd_ref, ka_ref, kb_ref, kc_ref, kd_ref,
               va_ref, vb_ref, vc_ref, vd_ref, *, tn):
    x = x_ref[...]
    ms = jnp.mean(x * x, axis=-1, keepdims=True)
    y = x * lax.rsqrt(ms + EPS) * g_ref[...]
    mod = mod_ref[0]
    sh = mod[:, 0:D_MODEL]
    sc = mod[:, D_MODEL:2 * D_MODEL]
    h = (y * (1 + sc) + sh).astype(BF16)
    qkv = [_dot(h, w_ref[:, j * tn:(j + 1) * tn]) for j in range(QKV_TOT // tn)]

    gains = gains_ref[...]
    rope = {64: (c64_ref, s64_ref), 32: (c32_ref, s32_ref)}
    ones = {64: ones64_ref, 32: ones32_ref}

    def chunk(col):
        return qkv[col // tn][:, col % tn:col % tn + LANES]

    def normed(col, seg, gain_row, use_rope, scale):
        y = _seg_rms(chunk(col), ones[seg][...], seg, gains[gain_row:gain_row + 1, :])
        if use_rope:
            y = y * rope[seg][0][...] + _rot_half(y, seg // 2) * rope[seg][1][...]
        return y * scale if scale != 1.0 else y

    units = []
    q_refs = (qa_ref, qb_ref, qc_ref, qd_ref)
    q_seg = (64, 32, 64, 64)
    q_rope = (True, True, True, False)
    for m in range(4):
        scale = float(q_seg[m]) ** -0.5 * LOG2E
        for c in range(2):
            def q_unit(m=m, c=c, scale=scale):
                y = normed(m * 256 + c * LANES, q_seg[m], 2 * m, q_rope[m], scale)
                q_refs[m][0, c * LANES:(c + 1) * LANES, :] = y.T.astype(BF16)
            units.append(q_unit)
    k_refs = (ka_ref, kb_ref, kc_ref, kd_ref)
    v_refs = (va_ref, vb_ref, vc_ref, vd_ref)
    widths = (128, 256, 128, 256)
    kcol = Q_TOT
    vcol = Q_TOT + sum(widths)
    for m in range(4):
        for c in range(widths[m] // LANES):
            def k_unit(m=m, c=c, kcol=kcol):
                y = normed(kcol, q_seg[m], 2 * m + 1, q_rope[m], 1.0)
                k_refs[m][0, :, c * LANES:(c + 1) * LANES] = y.astype(BF16)

            def v_unit(m=m, c=c, vcol=vcol):
                v_refs[m][0, c * LANES:(c + 1) * LANES, :] = chunk(vcol).T.astype(BF16)
            units += [k_unit, v_unit]
            kcol += LANES
            vcol += LANES

    n_gate = GATE_TOT // tn
    per_gate = -(-len(units) // n_gate)
    for j in range(n_gate):
        g = _dot(h, w_ref[:, QKV_TOT + j * tn:QKV_TOT + (j + 1) * tn]) + bg_ref[:, j * tn:(j + 1) * tn]
        gate_ref[:, j * tn:(j + 1) * tn] = jax.nn.sigmoid(g).astype(BF16)
        for unit in units[j * per_gate:(j + 1) * per_gate]:
            unit()


def _in_proj(X, norm_g, mod, w_in_bf, b_gate, row_batch, tables, ones64, ones32, gains, B, S, C):
    R = X.shape[0]
    nt = R // ROW_TILE
    per = S // ROW_TILE
    nlat = B * per
    T = S + C

    def bidx(t):
        return jnp.where(t < nlat, t // per, t - nlat)

    def pidx(t):
        return jnp.where(t < nlat, t % per, per)

    tab_spec = pl.BlockSpec((ROW_TILE, LANES), lambda t: (pidx(t), 0))
    const_spec = pl.BlockSpec((LANES, LANES), lambda t: (0, 0))
    widths = (128, 256, 128, 256)
    q_specs = [pl.BlockSpec((1, 256, ROW_TILE), lambda t: (bidx(t), 0, pidx(t))) for _ in range(4)]
    k_specs = [pl.BlockSpec((1, ROW_TILE, w), lambda t: (bidx(t), pidx(t), 0)) for w in widths]
    v_specs = [pl.BlockSpec((1, w, ROW_TILE), lambda t: (bidx(t), 0, pidx(t))) for w in widths]
    q_shapes = [jax.ShapeDtypeStruct((B, 256, T), BF16) for _ in range(4)]
    k_shapes = [jax.ShapeDtypeStruct((B, T, w), BF16) for w in widths]
    v_shapes = [jax.ShapeDtypeStruct((B, w, T), BF16) for w in widths]
    outs = pl.pallas_call(
        functools.partial(_in_kernel, tn=512),
        grid=(nt,),
        in_specs=[
            pl.BlockSpec((ROW_TILE, D_MODEL), lambda t: (t, 0)),
            pl.BlockSpec((1, D_MODEL), lambda t: (0, 0)),
            pl.BlockSpec((1, 1, N_MOD * D_MODEL), lambda t: (row_batch(t), 0, 0)),
            pl.BlockSpec((D_MODEL, QKV_TOT + GATE_TOT), lambda t: (0, 0), pipeline_mode=pl.Buffered(1)),
            pl.BlockSpec((1, GATE_TOT), lambda t: (0, 0)),
            tab_spec, tab_spec, tab_spec, tab_spec, const_spec, const_spec,
            pl.BlockSpec((8, LANES), lambda t: (0, 0)),
        ],
        out_specs=[pl.BlockSpec((ROW_TILE, GATE_TOT), lambda t: (t, 0))] + q_specs + k_specs + v_specs,
        out_shape=[jax.ShapeDtypeStruct((R, GATE_TOT), BF16)] + q_shapes + k_shapes + v_shapes,
        compiler_params=_cparams(("arbitrary",)),
        name="in_proj",
    )(X, norm_g.reshape(1, D_MODEL), mod, w_in_bf, b_gate.reshape(1, GATE_TOT), *tables, ones64, ones32, gains)
    return outs[0], outs[1:5], outs[5:9], outs[9:13]


def _pad_queries(q, pieces, kw, unit):
    tq = q.shape[1]
    r = lax.broadcasted_iota(jnp.int32, (kw, tq), 0)
    blocks = []
    for row0, size, extra in pieces:
        tiled = jnp.concatenate([q[row0:row0 + size, :]] * (kw // size), axis=0)
        off = unit * HEAD_DIM + extra
        blocks.append(jnp.where((r >= off) & (r < off + size), tiled, 0.0))
    out = blocks[0] if len(blocks) == 1 else jnp.concatenate(blocks, axis=1)
    return out.astype(BF16)


_PIECES = {
    "gqa": ((0, 64, 0), (64, 64, 0)),
    "diff": ((0, 32, 0), (32, 32, 32)),
    "mha": ((0, 64, 0),),
}


def _flash_kernel(*refs, mode, kw, tq, max_keys, chunks, use_sink, lam_init, aliased):
    refs = list(refs)
    qT_ref, k_ref, vT_ref = refs[0:3]
    pos = 3
    sink_ref = None
    if use_sink:
        sink_ref = refs[pos]
        pos += 1
    if mode == "diff":
        lamvec_ref, subg_ref = refs[pos:pos + 2]
        pos += 2
    if aliased:
        pos += 1
    o_ref = refs[pos]
    s_bufs = refs[pos + 1:pos + 4]

    unit = pl.program_id(1)
    pieces = _PIECES[mode]
    ng = len(pieces)
    n = ng * tq
    qpad = _pad_queries(qT_ref[0].astype(F32), pieces, kw, unit)

    if use_sink:
        m0 = jnp.concatenate([jnp.full((1, tq), sink_ref[ng * unit + g] * LOG2E, F32) for g in range(ng)],
                             axis=1)
        l0 = jnp.ones((1, n), F32)
    else:
        m0 = jnp.full((1, n), NEG_INF, F32)
        l0 = jnp.zeros((1, n), F32)
    acc0 = jnp.zeros((HEAD_DIM, n), F32)
    ones_rows = jnp.ones((SUM_ROWS, max_keys), BF16)

    def scores(start, size, s_ref):
        s = _dot(k_ref[0, pl.ds(start, size), :], qpad)
        s_ref[0:size, :] = s
        return jnp.max(s, axis=0, keepdims=True)

    def absorb(start, size, s_ref, mc, state):
        m, l, acc = state
        m_new = jnp.maximum(m, mc)
        alpha = jnp.exp2(m - m_new)
        p = jnp.exp2(s_ref[0:size, :] - m_new)
        v_aug = jnp.concatenate([vT_ref[0, :, pl.ds(start, size)], ones_rows[:, 0:size]], axis=0)
        r = _dot(v_aug, p.astype(BF16))
        return m_new, alpha * l + r[HEAD_DIM:HEAD_DIM + 1], alpha * acc + r[0:HEAD_DIM]

    state = (m0, l0, acc0)
    first, main, n_main, tail = chunks

    def main_start(i):
        return pl.multiple_of(main[0] + i * main[1], LANES)

    mc = scores(first[0], first[1], s_bufs[0])
    if n_main == 0:
        state = absorb(first[0], first[1], s_bufs[0], mc, state)
    else:
        mc_next = scores(main_start(0), main[1], s_bufs[1])
        state = absorb(first[0], first[1], s_bufs[0], mc, state)
        mc = mc_next

        def triple(i, carry):
            mc, state = carry[0], carry[1:]
            k = 3 * i
            mc2 = scores(main_start(k + 1), main[1], s_bufs[2])
            state = absorb(main_start(k), main[1], s_bufs[1], mc, state)
            mc0 = scores(main_start(k + 2), main[1], s_bufs[0])
            state = absorb(main_start(k + 1), main[1], s_bufs[2], mc2, state)
            mc1 = scores(main_start(k + 3), main[1], s_bufs[1])
            state = absorb(main_start(k + 2), main[1], s_bufs[0], mc0, state)
            return (mc1,) + tuple(state)

        n_loop = (n_main - 1) // 3
        if n_loop > 0:
            carry = lax.fori_loop(0, n_loop, triple, (mc,) + tuple(state))
            mc, state = carry[0], carry[1:]
        last = main_start(n_main - 1)
        if tail[1] > 0:
            mc_next = scores(tail[0], tail[1], s_bufs[2])
            state = absorb(last, main[1], s_bufs[1], mc, state)
            state = absorb(tail[0], tail[1], s_bufs[2], mc_next, state)
        else:
            state = absorb(last, main[1], s_bufs[1], mc, state)
    m, l, acc = state
    o = acc / l

    if mode == "gqa":
        for g in range(ng):
            o_ref[0, g * HEAD_DIM:(g + 1) * HEAD_DIM, :] = o[:, g * tq:(g + 1) * tq].astype(o_ref.dtype)
    elif mode == "mha":
        o_ref[0] = o.astype(o_ref.dtype)
    else:
        lv = lamvec_ref[...]
        lam = (jnp.exp(jnp.sum(lv[0:1] * lv[1:2], axis=1, keepdims=True))
               - jnp.exp(jnp.sum(lv[2:3] * lv[3:4], axis=1, keepdims=True)) + lam_init)
        d = o[:, 0:tq] - lam * o[:, tq:2 * tq]
        ms = jnp.mean(d * d, axis=0, keepdims=True)
        o_ref[0] = (d * lax.rsqrt(ms + EPS) * subg_ref[...] * (1.0 - lam_init)).astype(o_ref.dtype)


def _flash(qT, k, vT, *, mode, S, C, ctx_only, sink=None, lamvec=None, subg=None, lam_init=0.0, out_prev=None):
    B, _, T = qT.shape
    kw = k.shape[2]
    q_rows = 128 if mode == "gqa" else 64
    units = 256 // q_rows
    if ctx_only:
        tq, nq, q_blk0 = C, 1, S // C
        chunks = ((S, C), (0, 0), 0, (0, 0))
    else:
        tq = FLASH_QUERIES if S % FLASH_QUERIES == 0 else 256
        nq, q_blk0 = S // tq, 0
        tk = FLASH_KEYS if S >= 4 * FLASH_KEYS else 256
        n_main = (S // tk - 1) // 3 * 3 + 1
        chunks = ((S, C), (0, tk), n_main, (n_main * tk, S - n_main * tk))
    max_keys = max(c[1] for c in (chunks[0], chunks[1], chunks[3]))
    in_specs = [
        pl.BlockSpec((1, q_rows, tq), lambda b, u, i: (b, u, i + q_blk0)),
        pl.BlockSpec((1, T, kw), lambda b, u, i: (b, 0, 0)),
        pl.BlockSpec((1, HEAD_DIM, T), lambda b, u, i: (b, u, 0)),
    ]
    args = [qT, k, vT]
    if sink is not None:
        in_specs.append(pl.BlockSpec(memory_space=pltpu.SMEM))
        args.append(sink)
    if mode == "diff":
        in_specs += [pl.BlockSpec((8, LANES), lambda b, u, i: (0, 0)),
                     pl.BlockSpec((HEAD_DIM, 1), lambda b, u, i: (0, 0))]
        args += [lamvec, subg]
    aliases = {}
    if out_prev is not None:
        in_specs.append(pl.BlockSpec(memory_space=pl.ANY))
        aliases = {len(args): 0}
        args.append(out_prev)
    kern = functools.partial(_flash_kernel, mode=mode, kw=kw, tq=tq, max_keys=max_keys, chunks=chunks,
                             use_sink=sink is not None, lam_init=lam_init,
                             aliased=out_prev is not None)
    return pl.pallas_call(
        kern,
        grid=(B, units, nq),
        in_specs=in_specs,
        out_specs=pl.BlockSpec((1, q_rows, tq), lambda b, u, i: (b, u, i + q_blk0)),
        out_shape=jax.ShapeDtypeStruct((B, 256, T), BF16),
        input_output_aliases=aliases,
        scratch_shapes=[pltpu.VMEM((max_keys, len(_PIECES[mode]) * tq), F32)] * 3,
        compiler_params=_cparams(("arbitrary", "arbitrary", "arbitrary")),
        name="flash_" + mode + ("_ctx" if ctx_only else ""),
    )(*args)


WINDOW_BLOCKS = 2


def _window_kernel(qT_ref, k_ref, vT_ref, sink_ref, o_ref, *, S, C):
    step = pl.program_id(1)
    qb = WINDOW
    span = 3 * qb
    n = 2 * qb
    k_ctx = k_ref[0, S:S + C, :]
    for bi in range(WINDOW_BLOCKS):
        i = step * WINDOW_BLOCKS + bi
        cols = slice(bi * qb, (bi + 1) * qb)
        start = pl.multiple_of(jnp.clip((i - 1) * qb, 0, S - span), LANES)
        k_loc = k_ref[0, pl.ds(start, span), :]
        kpos = start + lax.broadcasted_iota(jnp.int32, (span, n), 0)
        qpos = i * qb + lax.broadcasted_iota(jnp.int32, (span, n), 1) % qb
        in_window = jnp.abs(kpos - qpos) <= WINDOW
        for unit in range(2):
            q = qT_ref[0, unit * 128:(unit + 1) * 128, cols].astype(F32)
            qpad = _pad_queries(q, _PIECES["gqa"], LANES, unit)
            s_loc = jnp.where(in_window, _dot(k_loc, qpad), NEG_INF)
            s_ctx = _dot(k_ctx, qpad)
            sink = jnp.concatenate(
                [jnp.full((1, qb), sink_ref[2 * unit + g] * LOG2E, F32) for g in range(2)], axis=1)
            m = jnp.maximum(jnp.maximum(jnp.max(s_loc, axis=0, keepdims=True),
                                        jnp.max(s_ctx, axis=0, keepdims=True)), sink)
            e_loc = jnp.exp2(s_loc - m)
            e_ctx = jnp.exp2(s_ctx - m)
            den = (jnp.sum(e_loc, axis=0, keepdims=True) + jnp.sum(e_ctx, axis=0, keepdims=True)
                   + jnp.exp2(sink - m))
            v_rows = slice(unit * HEAD_DIM, (unit + 1) * HEAD_DIM)
            o = (_dot(vT_ref[0, v_rows, pl.ds(start, span)], e_loc.astype(BF16))
                 + _dot(vT_ref[0, v_rows, S:S + C], e_ctx.astype(BF16))) / den
            for g in range(2):
                head = 2 * unit + g
                o_ref[0, head * HEAD_DIM:(head + 1) * HEAD_DIM, cols] = (
                    o[:, g * qb:(g + 1) * qb].astype(o_ref.dtype))


def _window(qT, k, vT, sink, S, C):
    B, _, T = qT.shape
    tq = WINDOW * WINDOW_BLOCKS
    return pl.pallas_call(
        functools.partial(_window_kernel, S=S, C=C),
        grid=(B, S // tq),
        in_specs=[
            pl.BlockSpec((1, 256, tq), lambda b, i: (b, 0, i)),
            pl.BlockSpec((1, T, LANES), lambda b, i: (b, 0, 0)),
            pl.BlockSpec((1, 2 * HEAD_DIM, T), lambda b, i: (b, 0, 0)),
            pl.BlockSpec(memory_space=pltpu.SMEM),
        ],
        out_specs=pl.BlockSpec((1, 256, tq), lambda b, i: (b, 0, i)),
        out_shape=jax.ShapeDtypeStruct((B, 256, T), BF16),
        compiler_params=_cparams(("arbitrary", "arbitrary")),
        name="window_attn",
    )(qT, k, vT, sink)


def _nbr_kernel(qT_ref, k_ref, vT_ref, tab_ref, o_ref, *, S, C):
    j = pl.program_id(1)
    rows = S // GRID_W
    n = NBR_Q_ROWS * GRID_W
    span = NBR_WIN_ROWS * GRID_W
    w0 = jnp.clip(NBR_Q_ROWS * j - NA_ROWS // 2, 0, rows - NBR_WIN_ROWS)
    shift = w0 - NBR_Q_ROWS * j + NA_ROWS // 2 + 4
    start = pl.multiple_of(w0 * GRID_W, LANES)
    k_loc = k_ref[0, pl.ds(start, span), :]
    k_ctx = k_ref[0, S:S + C, :]
    kr = w0 + lax.broadcasted_iota(jnp.int32, (span, n), 0) // GRID_W
    qr = NBR_Q_ROWS * j + lax.broadcasted_iota(jnp.int32, (span, n), 1) // GRID_W
    r0 = jnp.clip(qr - NA_ROWS // 2, 0, rows - NA_ROWS)
    in_rows = (kr >= r0) & (kr < r0 + NA_ROWS)
    for head in range(4):
        h_rows = slice(head * HEAD_DIM, (head + 1) * HEAD_DIM)
        qpad = _pad_queries(qT_ref[0, h_rows, :].astype(F32), _PIECES["mha"], 256, head)
        bias = tab_ref[head, pl.ds(pl.multiple_of(shift * GRID_W, GRID_W), span), :]
        s_loc = jnp.where(in_rows, _dot(k_loc, qpad) + bias, NEG_INF)
        s_ctx = _dot(k_ctx, qpad)
        m = jnp.maximum(jnp.max(s_loc, axis=0, keepdims=True), jnp.max(s_ctx, axis=0, keepdims=True))
        e_loc = jnp.exp2(s_loc - m)
        e_ctx = jnp.exp2(s_ctx - m)
        den = jnp.sum(e_loc, axis=0, keepdims=True) + jnp.sum(e_ctx, axis=0, keepdims=True)
        o = (_dot(vT_ref[0, h_rows, pl.ds(start, span)], e_loc.astype(BF16))
             + _dot(vT_ref[0, h_rows, S:S + C], e_ctx.astype(BF16))) / den
        o_ref[0, h_rows, :] = o.astype(o_ref.dtype)


def _nbr_bias_table(rpb):
    u = np.arange(NBR_TABLE_ROWS)[:, None, None, None]
    kc = np.arange(GRID_W)[None, :, None, None]
    e = np.arange(NBR_Q_ROWS)[None, None, :, None]
    qc = np.arange(GRID_W)[None, None, None, :]
    dr = u - e - 1
    row_ok = (dr >= 0) & (dr < 2 * NA_ROWS - 1)
    dc = np.clip(kc - qc, -(NA_COLS - 1), NA_COLS - 1) + (NA_COLS - 1)
    c0 = np.clip(qc - NA_COLS // 2, 0, GRID_W - NA_COLS)
    col_ok = (kc >= c0) & (kc < c0 + NA_COLS)
    shape = (NBR_TABLE_ROWS, GRID_W, NBR_Q_ROWS, GRID_W)
    n_dr, n_dc = 2 * NA_ROWS - 1, 2 * NA_COLS - 1
    sel_r = ((dr[:, 0, :, 0, None] == np.arange(n_dr)) & row_ok[:, 0, :, 0, None]).astype(np.float32)
    sel_c = (dc[0, :, 0, :, None] == np.arange(n_dc)).astype(np.float32)
    vals = jnp.einsum("uer,hrc,kqc->hukeq", sel_r, rpb.astype(F32) * LOG2E, sel_c,
                      precision=lax.Precision.HIGHEST)
    vals = jnp.where(np.broadcast_to(col_ok, shape), vals, NEG_INF)
    return vals.reshape(rpb.shape[0], NBR_TABLE_ROWS * GRID_W, NBR_Q_ROWS * GRID_W)


def _nbr(qT, k, vT, table, S, C):
    B, _, T = qT.shape
    tq = NBR_Q_ROWS * GRID_W
    return pl.pallas_call(
        functools.partial(_nbr_kernel, S=S, C=C),
        grid=(B, S // tq),
        in_specs=[
            pl.BlockSpec((1, 256, tq), lambda b, j: (b, 0, j)),
            pl.BlockSpec((1, T, 256), lambda b, j: (b, 0, 0)),
            pl.BlockSpec((1, 256, T), lambda b, j: (b, 0, 0)),
            pl.BlockSpec((4, NBR_TABLE_ROWS * GRID_W, tq), lambda b, j: (0, 0, 0)),
        ],
        out_specs=pl.BlockSpec((1, 256, tq), lambda b, j: (b, 0, j)),
        out_shape=jax.ShapeDtypeStruct((B, 256, T), BF16),
        compiler_params=_cparams(("arbitrary", "arbitrary")),
        name="nbr_attn",
    )(qT, k, vT, table)


ROUTE_W, ROUTE_E, ROUTE_RANK = 0, 4, 8
ROUTE_ROWS = 16
MOE_GROUPS = 1


def _merge_kernel(oa_ref, ob_ref, oc_ref, od_ref, gate_ref, x_ref, mod_ref, wb_ref, wo_ref, n2_ref,
                  rw_ref, rb_ref, tri_ref, xo_ref, h_ref, route_ref, cnt_ref, base_ref, *, group_tiles):
    @pl.when(pl.program_id(0) % group_tiles == 0)
    def _():
        base_ref[...] = jnp.zeros_like(base_ref)

    acc = None
    for nbr, o_ref in enumerate((oa_ref, ob_ref, oc_ref, od_ref)):
        proj = lax.dot_general(o_ref[0], wb_ref[nbr], (((0,), (0,)), ((), ())),
                               preferred_element_type=F32)
        term = gate_ref[:, nbr * D_MODEL:(nbr + 1) * D_MODEL].astype(F32) * proj
        acc = term if acc is None else acc + term
    mix = _dot(acc.astype(BF16), wo_ref[...])
    mod = mod_ref[0]
    g1 = mod[:, 2 * D_MODEL:3 * D_MODEL]
    sh2 = mod[:, 3 * D_MODEL:4 * D_MODEL]
    sc2 = mod[:, 4 * D_MODEL:5 * D_MODEL]
    xn = x_ref[...] + g1 * mix
    xo_ref[...] = xn
    ms = jnp.mean(xn * xn, axis=-1, keepdims=True)
    h = (xn * lax.rsqrt(ms + EPS) * n2_ref[...]) * (1 + sc2) + sh2
    h_ref[...] = h
    logits = lax.dot_general(rw_ref[...], h.astype(BF16), (((1,), (1,)), ((), ())),
                             preferred_element_type=F32) + rb_ref[...]
    row_f = lax.broadcasted_iota(jnp.int32, logits.shape, 0).astype(F32)
    work = logits
    picks = []
    for _ in range(TOP_K):
        top = jnp.max(work, axis=0, keepdims=True)
        idx = jnp.min(jnp.where(work == top, row_f, float(N_EXPERTS)), axis=0, keepdims=True)
        hit = row_f == idx
        picks.append((top, idx, hit))
        work = jnp.where(hit, -jnp.inf, work)
    ex = [jnp.exp(top - picks[0][0]) for top, _, _ in picks]
    den = ex[0] + ex[1] + ex[2] + ex[3]
    chosen = jnp.zeros(logits.shape, F32)
    for _, _, hit in picks:
        chosen = chosen + hit.astype(F32)
    base = base_ref[...][:, 0:1]
    before = _dot(chosen.astype(BF16), tri_ref[...]) + base
    fields = ([ex[k] / den for k in range(TOP_K)] + [idx for _, idx, _ in picks]
              + [jnp.sum(jnp.where(hit, before, 0.0), axis=0, keepdims=True) for _, _, hit in picks])
    fields.append(jnp.zeros((ROUTE_ROWS - len(fields), logits.shape[1]), F32))
    route_ref[...] = jnp.concatenate(fields, axis=0)
    base_ref[...] = base_ref[...] + jnp.sum(chosen, axis=1, keepdims=True)
    cnt_ref[0] = base_ref[...]


def _merge(oTs, gate, X, mod, wb_bf, wo_bf, norm2_g, rw_t, rb_col, n_rows, B, S, C, row_batch):
    nt = n_rows // ROW_TILE
    per = S // ROW_TILE
    nlat = B * per

    def bidx(t):
        return jnp.where(t < nlat, t // per, t - nlat)

    def pidx(t):
        return jnp.where(t < nlat, t % per, per)

    o_spec = pl.BlockSpec((1, 256, ROW_TILE), lambda t: (bidx(t), 0, pidx(t)))
    r = np.arange(ROW_TILE)
    strict_upper = jnp.asarray((r[:, None] < r[None, :]).astype(np.float32), dtype=BF16)
    group_tiles = nt // MOE_GROUPS
    assert nt % MOE_GROUPS == 0
    return pl.pallas_call(
        functools.partial(_merge_kernel, group_tiles=group_tiles),
        grid=(nt,),
        in_specs=[o_spec, o_spec, o_spec, o_spec,
                  pl.BlockSpec((ROW_TILE, GATE_TOT), lambda t: (t, 0)),
                  pl.BlockSpec((ROW_TILE, D_MODEL), lambda t: (t, 0)),
                  pl.BlockSpec((1, 1, N_MOD * D_MODEL), lambda t: (row_batch(t), 0, 0)),
                  pl.BlockSpec((4, 256, D_MODEL), lambda t: (0, 0, 0)),
                  pl.BlockSpec((D_MODEL, D_MODEL), lambda t: (0, 0)),
                  pl.BlockSpec((1, D_MODEL), lambda t: (0, 0)),
                  pl.BlockSpec((N_EXPERTS, D_MODEL), lambda t: (0, 0)),
                  pl.BlockSpec((N_EXPERTS, 1), lambda t: (0, 0)),
                  pl.BlockSpec((ROW_TILE, ROW_TILE), lambda t: (0, 0))],
        out_specs=[pl.BlockSpec((ROW_TILE, D_MODEL), lambda t: (t, 0)),
                   pl.BlockSpec((ROW_TILE, D_MODEL), lambda t: (t, 0)),
                   pl.BlockSpec((ROUTE_ROWS, ROW_TILE), lambda t: (0, t)),
                   pl.BlockSpec((1, N_EXPERTS, LANES), lambda t: (t // group_tiles, 0, 0))],
        out_shape=[jax.ShapeDtypeStruct((n_rows, D_MODEL), F32),
                   jax.ShapeDtypeStruct((n_rows, D_MODEL), F32),
                   jax.ShapeDtypeStruct((ROUTE_ROWS, n_rows), F32),
                   jax.ShapeDtypeStruct((MOE_GROUPS, N_EXPERTS, LANES), F32)],
        scratch_shapes=[pltpu.VMEM((N_EXPERTS, LANES), F32)],
        compiler_params=_cparams(("arbitrary",)),
        name="merge",
    )(*oTs, gate, X, mod, wb_bf, wo_bf, norm2_g.reshape(1, D_MODEL), rw_t, rb_col, strict_upper)


def _expert_kernel(tok_ref, be_ref, off_ref, nu_ref, h_hbm, wgu_ref, bgu_ref, wd_ref, bd_ref, y_ref,
                   x_0, x_1, x_2, sems, wgu_s, wd_s):
    i = pl.program_id(0)
    n_used = nu_ref[0]
    x_bufs = (x_0, x_1, x_2)

    def start_gather(block, x_dst, sem):
        base = off_ref[block]
        for r in range(EXPERT_ROWS):
            pltpu.make_async_copy(h_hbm.at[pl.ds(tok_ref[base + r], 1), :], x_dst.at[pl.ds(r, 1), :],
                                  sem).start()

    def wait_gather(x_dst, sem):
        pltpu.make_async_copy(h_hbm.at[pl.ds(0, EXPERT_ROWS), :], x_dst, sem).wait()

    def block(cur):
        nxt, ahead = (cur + 1) % 3, (cur + 2) % 3
        wait_gather(x_bufs[cur], sems.at[cur])
        start_gather(jnp.minimum(i + 2, n_used - 1), x_bufs[ahead], sems.at[ahead])
        x = x_bufs[cur][...].astype(BF16)
        gu = _dot(x, wgu_s[...]) + bgu_ref[0]
        gate = jnp.minimum(gu[:, :D_FF], SWIGLU_LIMIT)
        up = jnp.clip(gu[:, D_FF:], -SWIGLU_LIMIT, SWIGLU_LIMIT)
        a = gate * jax.nn.sigmoid(SWIGLU_ALPHA * gate) * (up + 1)
        y_ref[...] = (_dot(a.astype(BF16), wd_s[...]) + bd_ref[0]).astype(y_ref.dtype)

        @pl.when(i == n_used - 1)
        def _():
            wait_gather(x_bufs[nxt], sems.at[nxt])
            wait_gather(x_bufs[ahead], sems.at[ahead])

    @pl.when(i < n_used)
    def _():
        @pl.when(i == 0)
        def _():
            start_gather(0, x_bufs[0], sems.at[0])
            start_gather(jnp.minimum(1, n_used - 1), x_bufs[1], sems.at[1])

        prev = be_ref[jnp.maximum(i - 1, 0)]

        @pl.when((i == 0) | (be_ref[i] != prev))
        def _():
            wgu_s[...] = wgu_ref[0].astype(BF16)
            wd_s[...] = wd_ref[0].astype(BF16)

        for cur in range(3):
            pl.when(i % 3 == cur)(functools.partial(block, cur))


def _experts(h, tok_sorted, blk_e, blk_off, n_used, n_blk, w_gate_up, b_gate_up, w_down, b_down):
    E = w_gate_up.shape[0]

    def row_map(i, tok, be, off, nu):
        return (jnp.maximum(jnp.minimum(i, nu[0] - 1), 0), 0)

    def w_map(i, tok, be, off, nu):
        return (be[i], 0, 0)

    grid_spec = pltpu.PrefetchScalarGridSpec(
        num_scalar_prefetch=4,
        grid=(n_blk,),
        in_specs=[
            pl.BlockSpec(memory_space=pl.ANY),
            pl.BlockSpec((1, D_MODEL, 2 * D_FF), w_map),
            pl.BlockSpec((1, 1, 2 * D_FF), w_map),
            pl.BlockSpec((1, D_FF, D_MODEL), w_map),
            pl.BlockSpec((1, 1, D_MODEL), w_map),
        ],
        out_specs=pl.BlockSpec((EXPERT_ROWS, D_MODEL), row_map),
        scratch_shapes=[pltpu.VMEM((EXPERT_ROWS, D_MODEL), F32)] * 3 + [
                        pltpu.SemaphoreType.DMA((3,)),
                        pltpu.VMEM((D_MODEL, 2 * D_FF), BF16), pltpu.VMEM((D_FF, D_MODEL), BF16)],
    )
    return pl.pallas_call(
        _expert_kernel,
        grid_spec=grid_spec,
        out_shape=jax.ShapeDtypeStruct((n_blk * EXPERT_ROWS, D_MODEL), BF16),
        compiler_params=_cparams(("arbitrary",)),
        name="experts",
    )(tok_sorted, blk_e, blk_off, n_used, h, w_gate_up, b_gate_up.reshape(E, 1, 2 * D_FF), w_down,
      b_down.reshape(E, 1, D_MODEL))


def _combine_kernel(yg_ref, route_ref, x_ref, mod_ref, *rest):
    o_ref = rest[-1]
    route = route_ref[...]
    y = None
    for k in range(TOP_K):
        term = route[:, ROUTE_W + k:ROUTE_W + k + 1] * yg_ref[k].astype(F32)
        y = term if y is None else y + term
    g2 = mod_ref[0][:, 5 * D_MODEL:6 * D_MODEL]
    o_ref[...] = x_ref[...] + g2 * y


def _combine(yg, route, X, mod, row_batch, tile0, out_prev):
    n_rows = X.shape[0]
    in_specs = [pl.BlockSpec((TOP_K, ROW_TILE, D_MODEL), lambda t: (0, t, 0)),
                pl.BlockSpec((ROW_TILE, ROUTE_ROWS), lambda t: (t + tile0, 0)),
                pl.BlockSpec((ROW_TILE, D_MODEL), lambda t: (t + tile0, 0)),
                pl.BlockSpec((1, 1, N_MOD * D_MODEL), lambda t: (row_batch(t + tile0), 0, 0))]
    args = [yg, route, X, mod]
    aliases = {}
    if out_prev is not None:
        in_specs.append(pl.BlockSpec(memory_space=pl.ANY))
        aliases = {len(args): 0}
        args.append(out_prev)
    return pl.pallas_call(
        _combine_kernel,
        grid=(yg.shape[1] // ROW_TILE,),
        in_specs=in_specs,
        out_specs=pl.BlockSpec((ROW_TILE, D_MODEL), lambda t: (t + tile0, 0)),
        out_shape=jax.ShapeDtypeStruct((n_rows, D_MODEL), F32),
        input_output_aliases=aliases,
        compiler_params=_cparams(("arbitrary",)),
        name="moe_combine",
    )(*args)


def _moe(h, route, counts_f, X, mod, row_batch, layer, w_gate_up, b_gate_up, w_down, b_down):
    n_groups = counts_f.shape[0]
    Tg = h.shape[0] // n_groups
    n = Tg * TOP_K
    n_blk = n // EXPERT_ROWS + N_EXPERTS
    experts = jnp.arange(N_EXPERTS, dtype=jnp.int32)
    out = None
    for g in range(n_groups):
        route_g = lax.slice_in_dim(route, g * Tg, (g + 1) * Tg, axis=0)
        top_e = route_g[:, ROUTE_E:ROUTE_E + TOP_K].astype(jnp.int32)
        rank = route_g[:, ROUTE_RANK:ROUTE_RANK + TOP_K].astype(jnp.int32)
        counts = counts_f[g, :, 0].astype(jnp.int32)
        padded = (counts + EXPERT_ROWS - 1) // EXPERT_ROWS * EXPERT_ROWS
        pad_end = jnp.cumsum(padded)
        pad_start = pad_end - padded
        start_of = jnp.sum(jnp.where(top_e[:, :, None] == experts, pad_start, 0), axis=-1)
        dest = start_of + rank
        tok = jnp.broadcast_to(jnp.arange(g * Tg, (g + 1) * Tg, dtype=jnp.int32)[:, None], (Tg, TOP_K))
        _, tok_sorted = lax.sort_key_val(dest.reshape(-1), tok.reshape(-1))
        tok_sorted = jnp.concatenate([tok_sorted, jnp.zeros((EXPERT_ROWS,), jnp.int32)])
        n_used = (pad_end[-1] // EXPERT_ROWS).astype(jnp.int32)
        blk_row = jnp.minimum(jnp.arange(n_blk, dtype=jnp.int32), n_used - 1) * EXPERT_ROWS
        blk_e = jnp.sum((pad_end[None, :] <= blk_row[:, None]).astype(jnp.int32), axis=1)
        blk_e = jnp.minimum(blk_e, N_EXPERTS - 1)
        pad_before = pad_start - (jnp.cumsum(counts) - counts)
        blk_off = blk_row - jnp.sum(jnp.where(blk_e[:, None] == experts, pad_before, 0), axis=-1)
        y = _experts(h, tok_sorted, blk_e + layer * N_EXPERTS, blk_off, n_used.reshape(1), n_blk,
                     w_gate_up, b_gate_up, w_down, b_down)
        yg = y.at[dest.T.reshape(-1)].get(mode="promise_in_bounds").reshape(TOP_K, Tg, D_MODEL)
        out = _combine(yg, route, X, mod, row_batch, g * (Tg // ROW_TILE), out)
    return out


def _rope_tables(S, dim):
    t = jnp.arange(S, dtype=jnp.int32)
    row = (t // GRID_W).astype(F32)
    col = (t % GRID_W).astype(F32)
    n_freq = dim // 4
    inv = ROPE_THETA ** (-jnp.arange(n_freq, dtype=F32) / n_freq)
    ang = jnp.concatenate([row[:, None] * inv, col[:, None] * inv], axis=-1)
    cos, sin = jnp.cos(ang), jnp.sin(ang)
    reps = LANES // dim
    cos_t = jnp.tile(jnp.concatenate([cos, cos], axis=-1), (1, reps))
    sin_t = jnp.tile(jnp.concatenate([-sin, sin], axis=-1), (1, reps))
    cos_t = jnp.concatenate([cos_t, jnp.ones((ROW_TILE, LANES), F32)], axis=0)
    sin_t = jnp.concatenate([sin_t, jnp.zeros((ROW_TILE, LANES), F32)], axis=0)
    return cos_t, sin_t


def _block_ones(seg):
    i = np.arange(LANES)
    return jnp.asarray((i[:, None] // seg == i[None, :] // seg).astype(np.float32), dtype=BF16)


def _lane_tile(v):
    return jnp.tile(v.astype(F32), LANES // v.shape[0])


def kernel(x, c, ctx, c_ctx, norm1_g, norm2_g, w_ada, b_ada, w_in, b_gate, a_qn, a_kn, b_qn, b_kn, lam_q1, lam_k1, lam_q2, lam_k2, subln_g, c_qn, c_kn, sink, d_qn, d_kn, rpb, w_branch, w_out, router_w, router_b, w_gate_up, b_gate_up, w_down, b_down):
    B, S, D = x.shape
    C = ctx.shape[1]
    L = w_in.shape[0]
    assert D == D_MODEL and C == ROW_TILE and S % ROW_TILE == 0 and B + 1 <= 8
    n_lat = B * S
    per = S // ROW_TILE
    nlat_tiles = B * per

    def row_batch(t):
        return jnp.where(t < nlat_tiles, t // per, B)

    cvec = jnp.zeros((8, D), F32).at[:B].set(c).at[B].set(c_ctx)
    mod_all = _ada(cvec, w_ada, b_ada)

    tables = _rope_tables(S, HEAD_DIM) + _rope_tables(S, B_DK)
    ones64, ones32 = _block_ones(HEAD_DIM), _block_ones(B_DK)

    E = w_gate_up.shape[1]
    w_gu_all = w_gate_up.reshape(L * E, D, 2 * D_FF)
    w_dn_all = w_down.reshape(L * E, D_FF, D)

    X = jnp.concatenate([x.reshape(n_lat, D), ctx.reshape(B * C, D)], axis=0)
    for l in range(L):
        last = l == L - 1
        lam_init = 0.8 - 0.6 * math.exp(-0.3 * l)
        mod = mod_all[l].reshape(8, 1, N_MOD * D)
        gains = jnp.stack([_lane_tile(g[l]) for g in (a_qn, a_kn, b_qn, b_kn, c_qn, c_kn, d_qn, d_kn)])
        gate, qTs, ks, vTs = _in_proj(X, norm1_g[l], mod, w_in[l].astype(BF16), b_gate[l], row_batch,
                                      tables, ones64, ones32, gains, B, S, C)

        lamvec = jnp.zeros((8, LANES), F32)
        for r, v in enumerate((lam_q1, lam_k1, lam_q2, lam_k2)):
            lamvec = lamvec.at[r, :B_DK].set(v[l])
        subg = subln_g[l].reshape(HEAD_DIM, 1)
        sink_l = sink[l].astype(F32)
        nbr_table = _nbr_bias_table(rpb[l])

        o_a = _flash(qTs[0], ks[0], vTs[0], mode="gqa", S=S, C=C, ctx_only=False)
        o_b = _flash(qTs[1], ks[1], vTs[1], mode="diff", S=S, C=C, ctx_only=False,
                     lamvec=lamvec, subg=subg, lam_init=lam_init)
        o_c = _window(qTs[2], ks[2], vTs[2], sink_l, S, C)
        o_d = _nbr(qTs[3], ks[3], vTs[3], nbr_table, S, C)
        if not last:
            o_a = _flash(qTs[0], ks[0], vTs[0], mode="gqa", S=S, C=C, ctx_only=True, out_prev=o_a)
            o_b = _flash(qTs[1], ks[1], vTs[1], mode="diff", S=S, C=C, ctx_only=True,
                         lamvec=lamvec, subg=subg, lam_init=lam_init, out_prev=o_b)
            o_c = _flash(qTs[2], ks[2], vTs[2], mode="gqa", S=S, C=C, ctx_only=True, sink=sink_l, out_prev=o_c)
            o_d = _flash(qTs[3], ks[3], vTs[3], mode="mha", S=S, C=C, ctx_only=True, out_prev=o_d)

        n_rows = n_lat if last else X.shape[0]
        Xm, h2, route_t, counts = _merge((o_a, o_b, o_c, o_d), gate, X, mod, w_branch[l].astype(BF16),
                                         w_out[l].astype(BF16), norm2_g[l], router_w[l].T.astype(BF16),
                                         router_b[l].astype(F32).reshape(N_EXPERTS, 1), n_rows, B, S, C,
                                         row_batch)
        X = _moe(h2, route_t.T, counts, Xm, mod, row_batch, l, w_gu_all, b_gate_up.reshape(L * E, -1),
                 w_dn_all, b_down.reshape(L * E, -1))
    return X[:n_lat].reshape(B, S, D)
```

```python
import functools
import math

import numpy as np
import jax
import jax.numpy as jnp
from jax import lax
from jax.experimental import pallas as pl
from jax.experimental.pallas import tpu as pltpu

F32 = jnp.float32
BF16 = jnp.bfloat16

D_MODEL = 1024
GRID_W = 64
HEAD_DIM = 64
B_DK = 32
WINDOW = 128
NA_ROWS = 8
NA_COLS = 16
ROPE_THETA = 10000.0
N_EXPERTS = 32
TOP_K = 4
D_FF = D_MODEL
SWIGLU_LIMIT = 7.0
SWIGLU_ALPHA = 1.702
N_MOD = 6
EPS = 1e-6
NEG_INF = -1e30
LOG2E = 1.4426950408889634

Q_TOT = 1024
KV_TOT = 1536
QKV_TOT = Q_TOT + KV_TOT
GATE_TOT = 4 * D_MODEL

ROW_TILE = 256
LANES = 128
EXPERT_ROWS = 256
NBR_Q_ROWS = 4
NBR_WIN_ROWS = NA_ROWS + NBR_Q_ROWS
NBR_TABLE_ROWS = NBR_WIN_ROWS + 8
FLASH_KEYS = 1152
FLASH_QUERIES = 512
SUM_ROWS = 16
VMEM_LIMIT = 52 * 1024 * 1024


def _cparams(sem):
    return pltpu.CompilerParams(dimension_semantics=sem, vmem_limit_bytes=VMEM_LIMIT)


def _dot(a, b):
    return jnp.dot(a, b, preferred_element_type=F32)


def _ada_kernel(c_ref, w_ref, b_ref, o_ref):
    c = c_ref[...]
    s = c * jax.nn.sigmoid(c)
    o_ref[0] = jnp.dot(s, w_ref[0], precision=lax.Precision.HIGHEST,
                       preferred_element_type=F32) + b_ref[0]


def _ada(cvec, w_ada, b_ada):
    L = w_ada.shape[0]
    n_out = w_ada.shape[2]
    tn = 1536
    return pl.pallas_call(
        _ada_kernel,
        grid=(L, n_out // tn),
        in_specs=[
            pl.BlockSpec((8, D_MODEL), lambda l, j: (0, 0)),
            pl.BlockSpec((1, D_MODEL, tn), lambda l, j: (l, 0, j)),
            pl.BlockSpec((1, 1, tn), lambda l, j: (l, 0, j)),
        ],
        out_specs=pl.BlockSpec((1, 8, tn), lambda l, j: (l, 0, j)),
        out_shape=jax.ShapeDtypeStruct((L, 8, n_out), F32),
        compiler_params=_cparams(("arbitrary", "arbitrary")),
        name="ada_mod",
    )(cvec, w_ada, b_ada.reshape(L, 1, n_out))


def _seg_rms(x, ones, seg, g):
    sq = x * x
    hi = sq.astype(BF16)
    lo = (sq - hi.astype(F32)).astype(BF16)
    ss = _dot(hi, ones) + _dot(lo, ones)
    return x * lax.rsqrt(ss * (1.0 / seg) + EPS) * g


def _rot_half(y, half):
    lane = lax.broadcasted_iota(jnp.int32, y.shape, 1)
    fwd = pltpu.roll(y, LANES - half, axis=1)
    bwd = pltpu.roll(y, half, axis=1)
    return jnp.where((lane % (2 * half)) < half, fwd, bwd)


def _stream_specs(X, n_lat_tiles):
    lat, ctx_rows, ctx_tile0 = (X[0], X[1], 0) if isinstance(X, tuple) else (X, X, n_lat_tiles)
    specs = [pl.BlockSpec((ROW_TILE, D_MODEL), lambda t: (jnp.minimum(t, n_lat_tiles - 1), 0)),
             pl.BlockSpec((ROW_TILE, D_MODEL), lambda t: (ctx_tile0 + jnp.maximum(t - n_lat_tiles, 0), 0))]
    return specs, [lat, ctx_rows]


def _stream_rows(X):
    return X[0].shape[0] + X[1].shape[0] if isinstance(X, tuple) else X.shape[0]


def _in_kernel(xa_ref, xb_ref, g_ref, mod_ref, w_ref, bg_ref, c64_ref, s64_ref, c32_ref, s32_ref, ones64_ref,
               ones32_ref, gains_ref, gate_ref, qa_ref, qb_ref, qc_ref, qd_ref, ka_ref, kb_ref, kc_ref, kd_ref,
               va_ref, vb_ref, vc_ref, vd_ref, *, tn, n_lat_tiles):
    x = jnp.where(pl.program_id(0) < n_lat_tiles, xa_ref[...], xb_ref[...])
    ms = jnp.mean(x * x, axis=-1, keepdims=True)
    y = x * lax.rsqrt(ms + EPS) * g_ref[...]
    mod = mod_ref[0]
    sh = mod[:, 0:D_MODEL]
    sc = mod[:, D_MODEL:2 * D_MODEL]
    h = (y * (1 + sc) + sh).astype(BF16)
    qkv = [_dot(h, w_ref[:, j * tn:(j + 1) * tn]) for j in range(QKV_TOT // tn)]

    gains = gains_ref[...]
    rope = {64: (c64_ref, s64_ref), 32: (c32_ref, s32_ref)}
    ones = {64: ones64_ref, 32: ones32_ref}

    def chunk(col):
        return qkv[col // tn][:, col % tn:col % tn + LANES]

    def normed(col, seg, gain_row, use_rope, scale):
        y = _seg_rms(chunk(col), ones[seg][...], seg, gains[gain_row:gain_row + 1, :])
        if use_rope:
            y = y * rope[seg][0][...] + _rot_half(y, seg // 2) * rope[seg][1][...]
        return y * scale if scale != 1.0 else y

    units = []
    q_refs = (qa_ref, qb_ref, qc_ref, qd_ref)
    q_seg = (64, 32, 64, 64)
    q_rope = (True, True, True, False)
    for m in range(4):
        scale = float(q_seg[m]) ** -0.5 * LOG2E
        for c in range(2):
            def q_unit(m=m, c=c, scale=scale):
                y = normed(m * 256 + c * LANES, q_seg[m], 2 * m, q_rope[m], scale)
                q_refs[m][0, c * LANES:(c + 1) * LANES, :] = y.T.astype(BF16)
            units.append(q_unit)
    k_refs = (ka_ref, kb_ref, kc_ref, kd_ref)
    v_refs = (va_ref, vb_ref, vc_ref, vd_ref)
    widths = (128, 256, 128, 256)
    kcol = Q_TOT
    vcol = Q_TOT + sum(widths)
    for m in range(4):
        for c in range(widths[m] // LANES):
            def k_unit(m=m, c=c, kcol=kcol):
                y = normed(kcol, q_seg[m], 2 * m + 1, q_rope[m], 1.0)
                k_refs[m][0, :, c * LANES:(c + 1) * LANES] = y.astype(BF16)

            def v_unit(m=m, c=c, vcol=vcol):
                v_refs[m][0, c * LANES:(c + 1) * LANES, :] = chunk(vcol).T.astype(BF16)
            units += [k_unit, v_unit]
            kcol += LANES
            vcol += LANES

    n_gate = GATE_TOT // tn
    per_gate = -(-len(units) // n_gate)
    for j in range(n_gate):
        g = _dot(h, w_ref[:, QKV_TOT + j * tn:QKV_TOT + (j + 1) * tn]) + bg_ref[:, j * tn:(j + 1) * tn]
        gate_ref[:, j * tn:(j + 1) * tn] = jax.nn.sigmoid(g).astype(BF16)
        for unit in units[j * per_gate:(j + 1) * per_gate]:
            unit()


def _in_proj(X, norm_g, mod, w_in_bf, b_gate, row_batch, tables, ones64, ones32, gains, B, S, C):
    R = _stream_rows(X)
    nt = R // ROW_TILE
    per = S // ROW_TILE
    nlat = B * per
    T = S + C
    x_specs, x_args = _stream_specs(X, nlat)

    def bidx(t):
        return jnp.where(t < nlat, t // per, t - nlat)

    def pidx(t):
        return jnp.where(t < nlat, t % per, per)

    tab_spec = pl.BlockSpec((ROW_TILE, LANES), lambda t: (pidx(t), 0))
    const_spec = pl.BlockSpec((LANES, LANES), lambda t: (0, 0))
    widths = (128, 256, 128, 256)
    q_specs = [pl.BlockSpec((1, 256, ROW_TILE), lambda t: (bidx(t), 0, pidx(t))) for _ in range(4)]
    k_specs = [pl.BlockSpec((1, ROW_TILE, w), lambda t: (bidx(t), pidx(t), 0)) for w in widths]
    v_specs = [pl.BlockSpec((1, w, ROW_TILE), lambda t: (bidx(t), 0, pidx(t))) for w in widths]
    q_shapes = [jax.ShapeDtypeStruct((B, 256, T), BF16) for _ in range(4)]
    k_shapes = [jax.ShapeDtypeStruct((B, T, w), BF16) for w in widths]
    v_shapes = [jax.ShapeDtypeStruct((B, w, T), BF16) for w in widths]
    outs = pl.pallas_call(
        functools.partial(_in_kernel, tn=512, n_lat_tiles=nlat),
        grid=(nt,),
        in_specs=x_specs + [
            pl.BlockSpec((1, D_MODEL), lambda t: (0, 0)),
            pl.BlockSpec((1, 1, N_MOD * D_MODEL), lambda t: (row_batch(t), 0, 0)),
            pl.BlockSpec((D_MODEL, QKV_TOT + GATE_TOT), lambda t: (0, 0), pipeline_mode=pl.Buffered(1)),
            pl.BlockSpec((1, GATE_TOT), lambda t: (0, 0)),
            tab_spec, tab_spec, tab_spec, tab_spec, const_spec, const_spec,
            pl.BlockSpec((8, LANES), lambda t: (0, 0)),
        ],
        out_specs=[pl.BlockSpec((ROW_TILE, GATE_TOT), lambda t: (t, 0))] + q_specs + k_specs + v_specs,
        out_shape=[jax.ShapeDtypeStruct((R, GATE_TOT), BF16)] + q_shapes + k_shapes + v_shapes,
        compiler_params=_cparams(("arbitrary",)),
        name="in_proj",
    )(*x_args, norm_g.reshape(1, D_MODEL), mod, w_in_bf, b_gate.reshape(1, GATE_TOT), *tables, ones64, ones32,
      gains)
    return outs[0], outs[1:5], outs[5:9], outs[9:13]


def _pad_queries(q, pieces, kw, unit):
    tq = q.shape[1]
    r = lax.broadcasted_iota(jnp.int32, (kw, tq), 0)
    blocks = []
    for row0, size, extra in pieces:
        tiled = jnp.concatenate([q[row0:row0 + size, :]] * (kw // size), axis=0)
        off = unit * HEAD_DIM + extra
        blocks.append(jnp.where((r >= off) & (r < off + size), tiled, 0.0))
    out = blocks[0] if len(blocks) == 1 else jnp.concatenate(blocks, axis=1)
    return out.astype(BF16)


_PIECES = {
    "gqa": ((0, 64, 0), (64, 64, 0)),
    "diff": ((0, 32, 0), (32, 32, 32)),
    "mha": ((0, 64, 0),),
}


def _flash_kernel(*refs, mode, kw, tq, max_keys, chunks, use_sink, lam_init, aliased):
    refs = list(refs)
    qT_ref, k_ref, vT_ref = refs[0:3]
    pos = 3
    sink_ref = None
    if use_sink:
        sink_ref = refs[pos]
        pos += 1
    if mode == "diff":
        lamvec_ref, subg_ref = refs[pos:pos + 2]
        pos += 2
    if aliased:
        pos += 1
    o_ref = refs[pos]
    s_bufs = refs[pos + 1:pos + 4]

    unit = pl.program_id(1)
    pieces = _PIECES[mode]
    ng = len(pieces)
    n = ng * tq
    qpad = _pad_queries(qT_ref[0].astype(F32), pieces, kw, unit)

    if use_sink:
        m0 = jnp.concatenate([jnp.full((1, tq), sink_ref[ng * unit + g] * LOG2E, F32) for g in range(ng)],
                             axis=1)
        l0 = jnp.ones((1, n), F32)
    else:
        m0 = jnp.full((1, n), NEG_INF, F32)
        l0 = jnp.zeros((1, n), F32)
    acc0 = jnp.zeros((HEAD_DIM, n), F32)
    ones_rows = jnp.ones((SUM_ROWS, max_keys), BF16)

    def scores(start, size, s_ref):
        s = _dot(k_ref[0, pl.ds(start, size), :], qpad)
        s_ref[0:size, :] = s
        return jnp.max(s, axis=0, keepdims=True)

    def absorb(start, size, s_ref, mc, state):
        m, l, acc = state
        m_new = jnp.maximum(m, mc)
        alpha = jnp.exp2(m - m_new)
        p = jnp.exp2(s_ref[0:size, :] - m_new)
        v_aug = jnp.concatenate([vT_ref[0, :, pl.ds(start, size)], ones_rows[:, 0:size]], axis=0)
        r = _dot(v_aug, p.astype(BF16))
        return m_new, alpha * l + r[HEAD_DIM:HEAD_DIM + 1], alpha * acc + r[0:HEAD_DIM]

    state = (m0, l0, acc0)
    first, main, n_main, tail = chunks

    def main_start(i):
        return pl.multiple_of(main[0] + i * main[1], LANES)

    mc = scores(first[0], first[1], s_bufs[0])
    if n_main == 0:
        state = absorb(first[0], first[1], s_bufs[0], mc, state)
    else:
        mc_next = scores(main_start(0), main[1], s_bufs[1])
        state = absorb(first[0], first[1], s_bufs[0], mc, state)
        mc = mc_next

        def triple(i, carry):
            mc, state = carry[0], carry[1:]
            k = 3 * i
            mc2 = scores(main_start(k + 1), main[1], s_bufs[2])
            state = absorb(main_start(k), main[1], s_bufs[1], mc, state)
            mc0 = scores(main_start(k + 2), main[1], s_bufs[0])
            state = absorb(main_start(k + 1), main[1], s_bufs[2], mc2, state)
            mc1 = scores(main_start(k + 3), main[1], s_bufs[1])
            state = absorb(main_start(k + 2), main[1], s_bufs[0], mc0, state)
            return (mc1,) + tuple(state)

        n_loop = (n_main - 1) // 3
        if n_loop > 0:
            carry = lax.fori_loop(0, n_loop, triple, (mc,) + tuple(state))
            mc, state = carry[0], carry[1:]
        last = main_start(n_main - 1)
        if tail[1] > 0:
            mc_next = scores(tail[0], tail[1], s_bufs[2])
            state = absorb(last, main[1], s_bufs[1], mc, state)
            state = absorb(tail[0], tail[1], s_bufs[2], mc_next, state)
        else:
            state = absorb(last, main[1], s_bufs[1], mc, state)
    m, l, acc = state
    o = acc / l

    if mode == "gqa":
        for g in range(ng):
            o_ref[0, g * HEAD_DIM:(g + 1) * HEAD_DIM, :] = o[:, g * tq:(g + 1) * tq].astype(o_ref.dtype)
    elif mode == "mha":
        o_ref[0] = o.astype(o_ref.dtype)
    else:
        lv = lamvec_ref[...]
        lam = (jnp.exp(jnp.sum(lv[0:1] * lv[1:2], axis=1, keepdims=True))
               - jnp.exp(jnp.sum(lv[2:3] * lv[3:4], axis=1, keepdims=True)) + lam_init)
        d = o[:, 0:tq] - lam * o[:, tq:2 * tq]
        ms = jnp.mean(d * d, axis=0, keepdims=True)
        o_ref[0] = (d * lax.rsqrt(ms + EPS) * subg_ref[...] * (1.0 - lam_init)).astype(o_ref.dtype)


def _flash(qT, k, vT, *, mode, S, C, ctx_only, sink=None, lamvec=None, subg=None, lam_init=0.0, out_prev=None):
    B, _, T = qT.shape
    kw = k.shape[2]
    q_rows = 128 if mode == "gqa" else 64
    units = 256 // q_rows
    if ctx_only:
        tq, nq, q_blk0 = C, 1, S // C
        chunks = ((S, C), (0, 0), 0, (0, 0))
    else:
        tq = FLASH_QUERIES if S % FLASH_QUERIES == 0 else 256
        nq, q_blk0 = S // tq, 0
        tk = FLASH_KEYS if S >= 4 * FLASH_KEYS else 256
        n_main = (S // tk - 1) // 3 * 3 + 1
        chunks = ((S, C), (0, tk), n_main, (n_main * tk, S - n_main * tk))
    max_keys = max(c[1] for c in (chunks[0], chunks[1], chunks[3]))
    in_specs = [
        pl.BlockSpec((1, q_rows, tq), lambda b, u, i: (b, u, i + q_blk0)),
        pl.BlockSpec((1, T, kw), lambda b, u, i: (b, 0, 0)),
        pl.BlockSpec((1, HEAD_DIM, T), lambda b, u, i: (b, u, 0)),
    ]
    args = [qT, k, vT]
    if sink is not None:
        in_specs.append(pl.BlockSpec(memory_space=pltpu.SMEM))
        args.append(sink)
    if mode == "diff":
        in_specs += [pl.BlockSpec((8, LANES), lambda b, u, i: (0, 0)),
                     pl.BlockSpec((HEAD_DIM, 1), lambda b, u, i: (0, 0))]
        args += [lamvec, subg]
    aliases = {}
    if out_prev is not None:
        in_specs.append(pl.BlockSpec(memory_space=pl.ANY))
        aliases = {len(args): 0}
        args.append(out_prev)
    kern = functools.partial(_flash_kernel, mode=mode, kw=kw, tq=tq, max_keys=max_keys, chunks=chunks,
                             use_sink=sink is not None, lam_init=lam_init,
                             aliased=out_prev is not None)
    return pl.pallas_call(
        kern,
        grid=(B, units, nq),
        in_specs=in_specs,
        out_specs=pl.BlockSpec((1, q_rows, tq), lambda b, u, i: (b, u, i + q_blk0)),
        out_shape=jax.ShapeDtypeStruct((B, 256, T), BF16),
        input_output_aliases=aliases,
        scratch_shapes=[pltpu.VMEM((max_keys, len(_PIECES[mode]) * tq), F32)] * 3,
        compiler_params=_cparams(("arbitrary", "arbitrary", "arbitrary")),
        name="flash_" + mode + ("_ctx" if ctx_only else ""),
    )(*args)


WINDOW_BLOCKS = 2


def _window_kernel(qT_ref, k_ref, vT_ref, sink_ref, o_ref, *, S, C):
    step = pl.program_id(1)
    qb = WINDOW
    span = 3 * qb
    n = 2 * qb
    k_ctx = k_ref[0, S:S + C, :]
    for bi in range(WINDOW_BLOCKS):
        i = step * WINDOW_BLOCKS + bi
        cols = slice(bi * qb, (bi + 1) * qb)
        start = pl.multiple_of(jnp.clip((i - 1) * qb, 0, S - span), LANES)
        k_loc = k_ref[0, pl.ds(start, span), :]
        kpos = start + lax.broadcasted_iota(jnp.int32, (span, n), 0)
        qpos = i * qb + lax.broadcasted_iota(jnp.int32, (span, n), 1) % qb
        in_window = jnp.abs(kpos - qpos) <= WINDOW
        for unit in range(2):
            q = qT_ref[0, unit * 128:(unit + 1) * 128, cols].astype(F32)
            qpad = _pad_queries(q, _PIECES["gqa"], LANES, unit)
            s_loc = jnp.where(in_window, _dot(k_loc, qpad), NEG_INF)
            s_ctx = _dot(k_ctx, qpad)
            sink = jnp.concatenate(
                [jnp.full((1, qb), sink_ref[2 * unit + g] * LOG2E, F32) for g in range(2)], axis=1)
            m = jnp.maximum(jnp.maximum(jnp.max(s_loc, axis=0, keepdims=True),
                                        jnp.max(s_ctx, axis=0, keepdims=True)), sink)
            e_loc = jnp.exp2(s_loc - m)
            e_ctx = jnp.exp2(s_ctx - m)
            den = (jnp.sum(e_loc, axis=0, keepdims=True) + jnp.sum(e_ctx, axis=0, keepdims=True)
                   + jnp.exp2(sink - m))
            v_rows = slice(unit * HEAD_DIM, (unit + 1) * HEAD_DIM)
            o = (_dot(vT_ref[0, v_rows, pl.ds(start, span)], e_loc.astype(BF16))
                 + _dot(vT_ref[0, v_rows, S:S + C], e_ctx.astype(BF16))) / den
            for g in range(2):
                head = 2 * unit + g
                o_ref[0, head * HEAD_DIM:(head + 1) * HEAD_DIM, cols] = (
                    o[:, g * qb:(g + 1) * qb].astype(o_ref.dtype))


def _window(qT, k, vT, sink, S, C):
    B, _, T = qT.shape
    tq = WINDOW * WINDOW_BLOCKS
    return pl.pallas_call(
        functools.partial(_window_kernel, S=S, C=C),
        grid=(B, S // tq),
        in_specs=[
            pl.BlockSpec((1, 256, tq), lambda b, i: (b, 0, i)),
            pl.BlockSpec((1, T, LANES), lambda b, i: (b, 0, 0)),
            pl.BlockSpec((1, 2 * HEAD_DIM, T), lambda b, i: (b, 0, 0)),
            pl.BlockSpec(memory_space=pltpu.SMEM),
        ],
        out_specs=pl.BlockSpec((1, 256, tq), lambda b, i: (b, 0, i)),
        out_shape=jax.ShapeDtypeStruct((B, 256, T), BF16),
        compiler_params=_cparams(("arbitrary", "arbitrary")),
        name="window_attn",
    )(qT, k, vT, sink)


def _nbr_kernel(qT_ref, k_ref, vT_ref, tab_ref, o_ref, *, S, C):
    j = pl.program_id(1)
    rows = S // GRID_W
    n = NBR_Q_ROWS * GRID_W
    span = NBR_WIN_ROWS * GRID_W
    w0 = jnp.clip(NBR_Q_ROWS * j - NA_ROWS // 2, 0, rows - NBR_WIN_ROWS)
    shift = w0 - NBR_Q_ROWS * j + NA_ROWS // 2 + 4
    start = pl.multiple_of(w0 * GRID_W, LANES)
    k_loc = k_ref[0, pl.ds(start, span), :]
    k_ctx = k_ref[0, S:S + C, :]
    kr = w0 + lax.broadcasted_iota(jnp.int32, (span, n), 0) // GRID_W
    qr = NBR_Q_ROWS * j + lax.broadcasted_iota(jnp.int32, (span, n), 1) // GRID_W
    r0 = jnp.clip(qr - NA_ROWS // 2, 0, rows - NA_ROWS)
    in_rows = (kr >= r0) & (kr < r0 + NA_ROWS)
    for head in range(4):
        h_rows = slice(head * HEAD_DIM, (head + 1) * HEAD_DIM)
        qpad = _pad_queries(qT_ref[0, h_rows, :].astype(F32), _PIECES["mha"], 256, head)
        bias = tab_ref[head, pl.ds(pl.multiple_of(shift * GRID_W, GRID_W), span), :]
        s_loc = jnp.where(in_rows, _dot(k_loc, qpad) + bias, NEG_INF)
        s_ctx = _dot(k_ctx, qpad)
        m = jnp.maximum(jnp.max(s_loc, axis=0, keepdims=True), jnp.max(s_ctx, axis=0, keepdims=True))
        e_loc = jnp.exp2(s_loc - m)
        e_ctx = jnp.exp2(s_ctx - m)
        den = jnp.sum(e_loc, axis=0, keepdims=True) + jnp.sum(e_ctx, axis=0, keepdims=True)
        o = (_dot(vT_ref[0, h_rows, pl.ds(start, span)], e_loc.astype(BF16))
             + _dot(vT_ref[0, h_rows, S:S + C], e_ctx.astype(BF16))) / den
        o_ref[0, h_rows, :] = o.astype(o_ref.dtype)


def _nbr_bias_table(rpb):
    u = np.arange(NBR_TABLE_ROWS)[:, None, None, None]
    kc = np.arange(GRID_W)[None, :, None, None]
    e = np.arange(NBR_Q_ROWS)[None, None, :, None]
    qc = np.arange(GRID_W)[None, None, None, :]
    dr = u - e - 1
    row_ok = (dr >= 0) & (dr < 2 * NA_ROWS - 1)
    dc = np.clip(kc - qc, -(NA_COLS - 1), NA_COLS - 1) + (NA_COLS - 1)
    c0 = np.clip(qc - NA_COLS // 2, 0, GRID_W - NA_COLS)
    col_ok = (kc >= c0) & (kc < c0 + NA_COLS)
    shape = (NBR_TABLE_ROWS, GRID_W, NBR_Q_ROWS, GRID_W)
    n_dr, n_dc = 2 * NA_ROWS - 1, 2 * NA_COLS - 1
    sel_r = ((dr[:, 0, :, 0, None] == np.arange(n_dr)) & row_ok[:, 0, :, 0, None]).astype(np.float32)
    sel_c = (dc[0, :, 0, :, None] == np.arange(n_dc)).astype(np.float32)
    vals = jnp.einsum("uer,hrc,kqc->hukeq", sel_r, rpb.astype(F32) * LOG2E, sel_c,
                      precision=lax.Precision.HIGHEST)
    vals = jnp.where(np.broadcast_to(col_ok, shape), vals, NEG_INF)
    return vals.reshape(rpb.shape[0], NBR_TABLE_ROWS * GRID_W, NBR_Q_ROWS * GRID_W)


def _nbr(qT, k, vT, table, S, C):
    B, _, T = qT.shape
    tq = NBR_Q_ROWS * GRID_W
    return pl.pallas_call(
        functools.partial(_nbr_kernel, S=S, C=C),
        grid=(B, S // tq),
        in_specs=[
            pl.BlockSpec((1, 256, tq), lambda b, j: (b, 0, j)),
            pl.BlockSpec((1, T, 256), lambda b, j: (b, 0, 0)),
            pl.BlockSpec((1, 256, T), lambda b, j: (b, 0, 0)),
            pl.BlockSpec((4, NBR_TABLE_ROWS * GRID_W, tq), lambda b, j: (0, 0, 0)),
        ],
        out_specs=pl.BlockSpec((1, 256, tq), lambda b, j: (b, 0, j)),
        out_shape=jax.ShapeDtypeStruct((B, 256, T), BF16),
        compiler_params=_cparams(("arbitrary", "arbitrary")),
        name="nbr_attn",
    )(qT, k, vT, table)


ROUTE_W, ROUTE_E, ROUTE_RANK = 0, 4, 8
ROUTE_ROWS = 16
MOE_GROUPS = 1


def _merge_kernel(oa_ref, ob_ref, oc_ref, od_ref, gate_ref, xa_ref, xb_ref, mod_ref, wb_ref, wo_ref, n2_ref,
                  rw_ref, rb_ref, tri_ref, xo_ref, h_ref, route_ref, cnt_ref, base_ref, *, group_tiles,
                  n_lat_tiles):
    @pl.when(pl.program_id(0) % group_tiles == 0)
    def _():
        base_ref[...] = jnp.zeros_like(base_ref)

    acc = None
    for nbr, o_ref in enumerate((oa_ref, ob_ref, oc_ref, od_ref)):
        proj = lax.dot_general(o_ref[0], wb_ref[nbr], (((0,), (0,)), ((), ())),
                               preferred_element_type=F32)
        term = gate_ref[:, nbr * D_MODEL:(nbr + 1) * D_MODEL].astype(F32) * proj
        acc = term if acc is None else acc + term
    mix = _dot(acc.astype(BF16), wo_ref[...])
    mod = mod_ref[0]
    g1 = mod[:, 2 * D_MODEL:3 * D_MODEL]
    sh2 = mod[:, 3 * D_MODEL:4 * D_MODEL]
    sc2 = mod[:, 4 * D_MODEL:5 * D_MODEL]
    xn = jnp.where(pl.program_id(0) < n_lat_tiles, xa_ref[...], xb_ref[...]) + g1 * mix
    xo_ref[...] = xn
    ms = jnp.mean(xn * xn, axis=-1, keepdims=True)
    h = (xn * lax.rsqrt(ms + EPS) * n2_ref[...]) * (1 + sc2) + sh2
    h_ref[...] = h
    logits = lax.dot_general(rw_ref[...], h.astype(BF16), (((1,), (1,)), ((), ())),
                             preferred_element_type=F32) + rb_ref[...]
    row_f = lax.broadcasted_iota(jnp.int32, logits.shape, 0).astype(F32)
    work = logits
    picks = []
    for _ in range(TOP_K):
        top = jnp.max(work, axis=0, keepdims=True)
        idx = jnp.min(jnp.where(work == top, row_f, float(N_EXPERTS)), axis=0, keepdims=True)
        hit = row_f == idx
        picks.append((top, idx, hit))
        work = jnp.where(hit, -jnp.inf, work)
    ex = [jnp.exp(top - picks[0][0]) for top, _, _ in picks]
    den = ex[0] + ex[1] + ex[2] + ex[3]
    chosen = jnp.zeros(logits.shape, F32)
    for _, _, hit in picks:
        chosen = chosen + hit.astype(F32)
    base = base_ref[...][:, 0:1]
    before = _dot(chosen.astype(BF16), tri_ref[...]) + base
    fields = ([ex[k] / den for k in range(TOP_K)] + [idx for _, idx, _ in picks]
              + [jnp.sum(jnp.where(hit, before, 0.0), axis=0, keepdims=True) for _, _, hit in picks])
    fields.append(jnp.zeros((ROUTE_ROWS - len(fields), logits.shape[1]), F32))
    route_ref[...] = jnp.concatenate(fields, axis=0)
    base_ref[...] = base_ref[...] + jnp.sum(chosen, axis=1, keepdims=True)
    cnt_ref[0] = base_ref[...]


def _merge(oTs, gate, X, mod, wb_bf, wo_bf, norm2_g, rw_t, rb_col, n_rows, B, S, C, row_batch):
    nt = n_rows // ROW_TILE
    per = S // ROW_TILE
    nlat = B * per

    def bidx(t):
        return jnp.where(t < nlat, t // per, t - nlat)

    def pidx(t):
        return jnp.where(t < nlat, t % per, per)

    o_spec = pl.BlockSpec((1, 256, ROW_TILE), lambda t: (bidx(t), 0, pidx(t)))
    x_specs, x_args = _stream_specs(X, nlat)
    r = np.arange(ROW_TILE)
    strict_upper = jnp.asarray((r[:, None] < r[None, :]).astype(np.float32), dtype=BF16)
    group_tiles = nt // MOE_GROUPS
    assert nt % MOE_GROUPS == 0
    return pl.pallas_call(
        functools.partial(_merge_kernel, group_tiles=group_tiles, n_lat_tiles=nlat),
        grid=(nt,),
        in_specs=[o_spec, o_spec, o_spec, o_spec,
                  pl.BlockSpec((ROW_TILE, GATE_TOT), lambda t: (t, 0))] + x_specs + [
                  pl.BlockSpec((1, 1, N_MOD * D_MODEL), lambda t: (row_batch(t), 0, 0)),
                  pl.BlockSpec((4, 256, D_MODEL), lambda t: (0, 0, 0)),
                  pl.BlockSpec((D_MODEL, D_MODEL), lambda t: (0, 0)),
                  pl.BlockSpec((1, D_MODEL), lambda t: (0, 0)),
                  pl.BlockSpec((N_EXPERTS, D_MODEL), lambda t: (0, 0)),
                  pl.BlockSpec((N_EXPERTS, 1), lambda t: (0, 0)),
                  pl.BlockSpec((ROW_TILE, ROW_TILE), lambda t: (0, 0))],
        out_specs=[pl.BlockSpec((ROW_TILE, D_MODEL), lambda t: (t, 0)),
                   pl.BlockSpec((ROW_TILE, D_MODEL), lambda t: (t, 0)),
                   pl.BlockSpec((ROUTE_ROWS, ROW_TILE), lambda t: (0, t)),
                   pl.BlockSpec((1, N_EXPERTS, LANES), lambda t: (t // group_tiles, 0, 0))],
        out_shape=[jax.ShapeDtypeStruct((n_rows, D_MODEL), F32),
                   jax.ShapeDtypeStruct((n_rows, D_MODEL), F32),
                   jax.ShapeDtypeStruct((ROUTE_ROWS, n_rows), F32),
                   jax.ShapeDtypeStruct((MOE_GROUPS, N_EXPERTS, LANES), F32)],
        scratch_shapes=[pltpu.VMEM((N_EXPERTS, LANES), F32)],
        compiler_params=_cparams(("arbitrary",)),
        name="merge",
    )(*oTs, gate, *x_args, mod, wb_bf, wo_bf, norm2_g.reshape(1, D_MODEL), rw_t, rb_col, strict_upper)


def _expert_kernel(tok_ref, be_ref, off_ref, nu_ref, h_hbm, wgu_ref, bgu_ref, wd_ref, bd_ref, y_ref,
                   x_0, x_1, x_2, sems, wgu_s, wd_s):
    i = pl.program_id(0)
    n_used = nu_ref[0]
    x_bufs = (x_0, x_1, x_2)

    def start_gather(block, x_dst, sem):
        base = off_ref[block]
        for r in range(EXPERT_ROWS):
            pltpu.make_async_copy(h_hbm.at[pl.ds(tok_ref[base + r], 1), :], x_dst.at[pl.ds(r, 1), :],
                                  sem).start()

    def wait_gather(x_dst, sem):
        pltpu.make_async_copy(h_hbm.at[pl.ds(0, EXPERT_ROWS), :], x_dst, sem).wait()

    def block(cur):
        nxt, ahead = (cur + 1) % 3, (cur + 2) % 3
        wait_gather(x_bufs[cur], sems.at[cur])
        start_gather(jnp.minimum(i + 2, n_used - 1), x_bufs[ahead], sems.at[ahead])
        x = x_bufs[cur][...].astype(BF16)
        gu = _dot(x, wgu_s[...]) + bgu_ref[0]
        gate = jnp.minimum(gu[:, :D_FF], SWIGLU_LIMIT)
        up = jnp.clip(gu[:, D_FF:], -SWIGLU_LIMIT, SWIGLU_LIMIT)
        a = gate * jax.nn.sigmoid(SWIGLU_ALPHA * gate) * (up + 1)
        y_ref[...] = (_dot(a.astype(BF16), wd_s[...]) + bd_ref[0]).astype(y_ref.dtype)

        @pl.when(i == n_used - 1)
        def _():
            wait_gather(x_bufs[nxt], sems.at[nxt])
            wait_gather(x_bufs[ahead], sems.at[ahead])

    @pl.when(i < n_used)
    def _():
        @pl.when(i == 0)
        def _():
            start_gather(0, x_bufs[0], sems.at[0])
            start_gather(jnp.minimum(1, n_used - 1), x_bufs[1], sems.at[1])

        prev = be_ref[jnp.maximum(i - 1, 0)]

        @pl.when((i == 0) | (be_ref[i] != prev))
        def _():
            wgu_s[...] = wgu_ref[0].astype(BF16)
            wd_s[...] = wd_ref[0].astype(BF16)

        for cur in range(3):
            pl.when(i % 3 == cur)(functools.partial(block, cur))


def _experts(h, tok_sorted, blk_e, blk_off, n_used, n_blk, w_gate_up, b_gate_up, w_down, b_down):
    E = w_gate_up.shape[0]

    def row_map(i, tok, be, off, nu):
        return (jnp.maximum(jnp.minimum(i, nu[0] - 1), 0), 0)

    def w_map(i, tok, be, off, nu):
        return (be[i], 0, 0)

    grid_spec = pltpu.PrefetchScalarGridSpec(
        num_scalar_prefetch=4,
        grid=(n_blk,),
        in_specs=[
            pl.BlockSpec(memory_space=pl.ANY),
            pl.BlockSpec((1, D_MODEL, 2 * D_FF), w_map),
            pl.BlockSpec((1, 1, 2 * D_FF), w_map),
            pl.BlockSpec((1, D_FF, D_MODEL), w_map),
            pl.BlockSpec((1, 1, D_MODEL), w_map),
        ],
        out_specs=pl.BlockSpec((EXPERT_ROWS, D_MODEL), row_map),
        scratch_shapes=[pltpu.VMEM((EXPERT_ROWS, D_MODEL), F32)] * 3 + [
                        pltpu.SemaphoreType.DMA((3,)),
                        pltpu.VMEM((D_MODEL, 2 * D_FF), BF16), pltpu.VMEM((D_FF, D_MODEL), BF16)],
    )
    return pl.pallas_call(
        _expert_kernel,
        grid_spec=grid_spec,
        out_shape=jax.ShapeDtypeStruct((n_blk * EXPERT_ROWS, D_MODEL), BF16),
        compiler_params=_cparams(("arbitrary",)),
        name="experts",
    )(tok_sorted, blk_e, blk_off, n_used, h, w_gate_up, b_gate_up.reshape(E, 1, 2 * D_FF), w_down,
      b_down.reshape(E, 1, D_MODEL))


def _combine_kernel(yg_ref, route_ref, x_ref, mod_ref, *rest):
    o_ref = rest[-1]
    route = route_ref[...]
    y = None
    for k in range(TOP_K):
        term = route[:, ROUTE_W + k:ROUTE_W + k + 1] * yg_ref[k].astype(F32)
        y = term if y is None else y + term
    g2 = mod_ref[0][:, 5 * D_MODEL:6 * D_MODEL]
    o_ref[...] = x_ref[...] + g2 * y


def _combine(yg, route, X, mod, row_batch, tile0, out_prev):
    n_rows = X.shape[0]
    in_specs = [pl.BlockSpec((TOP_K, ROW_TILE, D_MODEL), lambda t: (0, t, 0)),
                pl.BlockSpec((ROW_TILE, ROUTE_ROWS), lambda t: (t + tile0, 0)),
                pl.BlockSpec((ROW_TILE, D_MODEL), lambda t: (t + tile0, 0)),
                pl.BlockSpec((1, 1, N_MOD * D_MODEL), lambda t: (row_batch(t + tile0), 0, 0))]
    args = [yg, route, X, mod]
    aliases = {}
    if out_prev is not None:
        in_specs.append(pl.BlockSpec(memory_space=pl.ANY))
        aliases = {len(args): 0}
        args.append(out_prev)
    return pl.pallas_call(
        _combine_kernel,
        grid=(yg.shape[1] // ROW_TILE,),
        in_specs=in_specs,
        out_specs=pl.BlockSpec((ROW_TILE, D_MODEL), lambda t: (t + tile0, 0)),
        out_shape=jax.ShapeDtypeStruct((n_rows, D_MODEL), F32),
        input_output_aliases=aliases,
        compiler_params=_cparams(("arbitrary",)),
        name="moe_combine",
    )(*args)


def _moe(h, route, counts_f, X, mod, row_batch, layer, w_gate_up, b_gate_up, w_down, b_down):
    n_groups = counts_f.shape[0]
    Tg = h.shape[0] // n_groups
    n = Tg * TOP_K
    n_blk = n // EXPERT_ROWS + N_EXPERTS
    experts = jnp.arange(N_EXPERTS, dtype=jnp.int32)
    out = None
    for g in range(n_groups):
        route_g = lax.slice_in_dim(route, g * Tg, (g + 1) * Tg, axis=0)
        top_e = route_g[:, ROUTE_E:ROUTE_E + TOP_K].astype(jnp.int32)
        rank = route_g[:, ROUTE_RANK:ROUTE_RANK + TOP_K].astype(jnp.int32)
        counts = counts_f[g, :, 0].astype(jnp.int32)
        padded = (counts + EXPERT_ROWS - 1) // EXPERT_ROWS * EXPERT_ROWS
        pad_end = jnp.cumsum(padded)
        pad_start = pad_end - padded
        start_of = jnp.sum(jnp.where(top_e[:, :, None] == experts, pad_start, 0), axis=-1)
        dest = start_of + rank
        tok = jnp.broadcast_to(jnp.arange(g * Tg, (g + 1) * Tg, dtype=jnp.int32)[:, None], (Tg, TOP_K))
        _, tok_sorted = lax.sort_key_val(dest.reshape(-1), tok.reshape(-1))
        tok_sorted = jnp.concatenate([tok_sorted, jnp.zeros((EXPERT_ROWS,), jnp.int32)])
        n_used = (pad_end[-1] // EXPERT_ROWS).astype(jnp.int32)
        blk_row = jnp.minimum(jnp.arange(n_blk, dtype=jnp.int32), n_used - 1) * EXPERT_ROWS
        blk_e = jnp.sum((pad_end[None, :] <= blk_row[:, None]).astype(jnp.int32), axis=1)
        blk_e = jnp.minimum(blk_e, N_EXPERTS - 1)
        pad_before = pad_start - (jnp.cumsum(counts) - counts)
        blk_off = blk_row - jnp.sum(jnp.where(blk_e[:, None] == experts, pad_before, 0), axis=-1)
        y = _experts(h, tok_sorted, blk_e + layer * N_EXPERTS, blk_off, n_used.reshape(1), n_blk,
                     w_gate_up, b_gate_up, w_down, b_down)
        yg = y.at[dest.T.reshape(-1)].get(mode="promise_in_bounds").reshape(TOP_K, Tg, D_MODEL)
        out = _combine(yg, route, X, mod, row_batch, g * (Tg // ROW_TILE), out)
    return out


def _rope_tables(S, dim):
    t = jnp.arange(S, dtype=jnp.int32)
    row = (t // GRID_W).astype(F32)
    col = (t % GRID_W).astype(F32)
    n_freq = dim // 4
    inv = ROPE_THETA ** (-jnp.arange(n_freq, dtype=F32) / n_freq)
    ang = jnp.concatenate([row[:, None] * inv, col[:, None] * inv], axis=-1)
    cos, sin = jnp.cos(ang), jnp.sin(ang)
    reps = LANES // dim
    cos_t = jnp.tile(jnp.concatenate([cos, cos], axis=-1), (1, reps))
    sin_t = jnp.tile(jnp.concatenate([-sin, sin], axis=-1), (1, reps))
    cos_t = jnp.concatenate([cos_t, jnp.ones((ROW_TILE, LANES), F32)], axis=0)
    sin_t = jnp.concatenate([sin_t, jnp.zeros((ROW_TILE, LANES), F32)], axis=0)
    return cos_t, sin_t


def _block_ones(seg):
    i = np.arange(LANES)
    return jnp.asarray((i[:, None] // seg == i[None, :] // seg).astype(np.float32), dtype=BF16)


def _lane_tile(v):
    return jnp.tile(v.astype(F32), LANES // v.shape[0])


def kernel(x, c, ctx, c_ctx, norm1_g, norm2_g, w_ada, b_ada, w_in, b_gate, a_qn, a_kn, b_qn, b_kn, lam_q1, lam_k1, lam_q2, lam_k2, subln_g, c_qn, c_kn, sink, d_qn, d_kn, rpb, w_branch, w_out, router_w, router_b, w_gate_up, b_gate_up, w_down, b_down):
    B, S, D = x.shape
    C = ctx.shape[1]
    L = w_in.shape[0]
    assert D == D_MODEL and C == ROW_TILE and S % ROW_TILE == 0 and B + 1 <= 8
    n_lat = B * S
    per = S // ROW_TILE
    nlat_tiles = B * per

    def row_batch(t):
        return jnp.where(t < nlat_tiles, t // per, B)

    cvec = jnp.zeros((8, D), F32).at[:B].set(c).at[B].set(c_ctx)
    mod_all = _ada(cvec, w_ada, b_ada)

    tables = _rope_tables(S, HEAD_DIM) + _rope_tables(S, B_DK)
    ones64, ones32 = _block_ones(HEAD_DIM), _block_ones(B_DK)

    E = w_gate_up.shape[1]
    w_gu_all = w_gate_up.reshape(L * E, D, 2 * D_FF)
    w_dn_all = w_down.reshape(L * E, D_FF, D)

    X = (x.reshape(n_lat, D), ctx.reshape(B * C, D))
    for l in range(L):
        last = l == L - 1
        lam_init = 0.8 - 0.6 * math.exp(-0.3 * l)
        mod = mod_all[l].reshape(8, 1, N_MOD * D)
        gains = jnp.stack([_lane_tile(g[l]) for g in (a_qn, a_kn, b_qn, b_kn, c_qn, c_kn, d_qn, d_kn)])
        gate, qTs, ks, vTs = _in_proj(X, norm1_g[l], mod, w_in[l].astype(BF16), b_gate[l], row_batch,
                                      tables, ones64, ones32, gains, B, S, C)

        lamvec = jnp.zeros((8, LANES), F32)
        for r, v in enumerate((lam_q1, lam_k1, lam_q2, lam_k2)):
            lamvec = lamvec.at[r, :B_DK].set(v[l])
        subg = subln_g[l].reshape(HEAD_DIM, 1)
        sink_l = sink[l].astype(F32)
        nbr_table = _nbr_bias_table(rpb[l])

        o_a = _flash(qTs[0], ks[0], vTs[0], mode="gqa", S=S, C=C, ctx_only=False)
        o_b = _flash(qTs[1], ks[1], vTs[1], mode="diff", S=S, C=C, ctx_only=False,
                     lamvec=lamvec, subg=subg, lam_init=lam_init)
        o_c = _window(qTs[2], ks[2], vTs[2], sink_l, S, C)
        o_d = _nbr(qTs[3], ks[3], vTs[3], nbr_table, S, C)
        if not last:
            o_a = _flash(qTs[0], ks[0], vTs[0], mode="gqa", S=S, C=C, ctx_only=True, out_prev=o_a)
            o_b = _flash(qTs[1], ks[1], vTs[1], mode="diff", S=S, C=C, ctx_only=True,
                         lamvec=lamvec, subg=subg, lam_init=lam_init, out_prev=o_b)
            o_c = _flash(qTs[2], ks[2], vTs[2], mode="gqa", S=S, C=C, ctx_only=True, sink=sink_l, out_prev=o_c)
            o_d = _flash(qTs[3], ks[3], vTs[3], mode="mha", S=S, C=C, ctx_only=True, out_prev=o_d)

        n_rows = n_lat if last else _stream_rows(X)
        Xm, h2, route_t, counts = _merge((o_a, o_b, o_c, o_d), gate, X, mod, w_branch[l].astype(BF16),
                                         w_out[l].astype(BF16), norm2_g[l], router_w[l].T.astype(BF16),
                                         router_b[l].astype(F32).reshape(N_EXPERTS, 1), n_rows, B, S, C,
                                         row_batch)
        X = _moe(h2, route_t.T, counts, Xm, mod, row_batch, l, w_gu_all, b_gate_up.reshape(L * E, -1),
                 w_dn_all, b_down.reshape(L * E, -1))
    return X[:n_lat].reshape(B, S, D)
```

```python
import functools
import math

import numpy as np
import jax
import jax.numpy as jnp
from jax import lax
from jax.experimental import pallas as pl
from jax.experimental.pallas import tpu as pltpu

F32 = jnp.float32
BF16 = jnp.bfloat16

D_MODEL = 1024
GRID_W = 64
HEAD_DIM = 64
B_DK = 32
WINDOW = 128
NA_ROWS = 8
NA_COLS = 16
ROPE_THETA = 10000.0
N_EXPERTS = 32
TOP_K = 4
D_FF = D_MODEL
SWIGLU_LIMIT = 7.0
SWIGLU_ALPHA = 1.702
N_MOD = 6
EPS = 1e-6
NEG_INF = -1e30
LOG2E = 1.4426950408889634

Q_TOT = 1024
KV_TOT = 1536
QKV_TOT = Q_TOT + KV_TOT
GATE_TOT = 4 * D_MODEL

ROW_TILE = 256
LANES = 128
EXPERT_ROWS = 256
NBR_Q_ROWS = 4
NBR_WIN_ROWS = NA_ROWS + NBR_Q_ROWS
NBR_TABLE_ROWS = NBR_WIN_ROWS + 8
FLASH_KEYS = 1152
FLASH_QUERIES = 512
SUM_ROWS = 16
VMEM_LIMIT = 52 * 1024 * 1024


def _cparams(sem):
    return pltpu.CompilerParams(dimension_semantics=sem, vmem_limit_bytes=VMEM_LIMIT)


def _dot(a, b):
    return jnp.dot(a, b, preferred_element_type=F32)


def _ada_kernel(c_ref, w_ref, b_ref, o_ref):
    c = c_ref[...]
    s = c * jax.nn.sigmoid(c)
    o_ref[0] = jnp.dot(s, w_ref[0], precision=lax.Precision.HIGHEST,
                       preferred_element_type=F32) + b_ref[0]


def _ada(cvec, w_ada, b_ada):
    L = w_ada.shape[0]
    n_out = w_ada.shape[2]
    tn = 1536
    return pl.pallas_call(
        _ada_kernel,
        grid=(L, n_out // tn),
        in_specs=[
            pl.BlockSpec((8, D_MODEL), lambda l, j: (0, 0)),
            pl.BlockSpec((1, D_MODEL, tn), lambda l, j: (l, 0, j)),
            pl.BlockSpec((1, 1, tn), lambda l, j: (l, 0, j)),
        ],
        out_specs=pl.BlockSpec((1, 8, tn), lambda l, j: (l, 0, j)),
        out_shape=jax.ShapeDtypeStruct((L, 8, n_out), F32),
        compiler_params=_cparams(("arbitrary", "arbitrary")),
        name="ada_mod",
    )(cvec, w_ada, b_ada.reshape(L, 1, n_out))


def _seg_rms(x, ones, seg, g):
    sq = x * x
    hi = sq.astype(BF16)
    lo = (sq - hi.astype(F32)).astype(BF16)
    ss = _dot(hi, ones) + _dot(lo, ones)
    return x * lax.rsqrt(ss * (1.0 / seg) + EPS) * g


def _rot_half(y, half):
    lane = lax.broadcasted_iota(jnp.int32, y.shape, 1)
    fwd = pltpu.roll(y, LANES - half, axis=1)
    bwd = pltpu.roll(y, half, axis=1)
    return jnp.where((lane % (2 * half)) < half, fwd, bwd)


def _stream_specs(X, n_lat_tiles):
    lat, ctx_rows, ctx_tile0 = (X[0], X[1], 0) if isinstance(X, tuple) else (X, X, n_lat_tiles)
    specs = [pl.BlockSpec((ROW_TILE, D_MODEL), lambda t: (jnp.minimum(t, n_lat_tiles - 1), 0)),
             pl.BlockSpec((ROW_TILE, D_MODEL), lambda t: (ctx_tile0 + jnp.maximum(t - n_lat_tiles, 0), 0))]
    return specs, [lat, ctx_rows]


def _stream_rows(X):
    return X[0].shape[0] + X[1].shape[0] if isinstance(X, tuple) else X.shape[0]


def _in_kernel(xa_ref, xb_ref, g_ref, mod_ref, w_ref, bg_ref, c64_ref, s64_ref, c32_ref, s32_ref, ones64_ref,
               ones32_ref, gains_ref, gate_ref, qa_ref, qb_ref, qc_ref, qd_ref, ka_ref, kb_ref, kc_ref, kd_ref,
               va_ref, vb_ref, vc_ref, vd_ref, *, tn, n_lat_tiles):
    x = jnp.where(pl.program_id(0) < n_lat_tiles, xa_ref[...], xb_ref[...])
    ms = jnp.mean(x * x, axis=-1, keepdims=True)
    y = x * lax.rsqrt(ms + EPS) * g_ref[...]
    mod = mod_ref[0]
    sh = mod[:, 0:D_MODEL]
    sc = mod[:, D_MODEL:2 * D_MODEL]
    h = (y * (1 + sc) + sh).astype(BF16)
    qkv = [_dot(h, w_ref[:, j * tn:(j + 1) * tn]) for j in range(QKV_TOT // tn)]

    gains = gains_ref[...]
    rope = {64: (c64_ref, s64_ref), 32: (c32_ref, s32_ref)}
    ones = {64: ones64_ref, 32: ones32_ref}

    def chunk(col):
        return qkv[col // tn][:, col % tn:col % tn + LANES]

    def normed(col, seg, gain_row, use_rope, scale):
        y = _seg_rms(chunk(col), ones[seg][...], seg, gains[gain_row:gain_row + 1, :])
        if use_rope:
            y = y * rope[seg][0][...] + _rot_half(y, seg // 2) * rope[seg][1][...]
        return y * scale if scale != 1.0 else y

    units = []
    q_refs = (qa_ref, qb_ref, qc_ref, qd_ref)
    q_seg = (64, 32, 64, 64)
    q_rope = (True, True, True, False)
    for m in range(4):
        scale = float(q_seg[m]) ** -0.5 * LOG2E
        for c in range(2):
            def q_unit(m=m, c=c, scale=scale):
                y = normed(m * 256 + c * LANES, q_seg[m], 2 * m, q_rope[m], scale)
                q_refs[m][0, c * LANES:(c + 1) * LANES, :] = y.T.astype(BF16)
            units.append(q_unit)
    k_refs = (ka_ref, kb_ref, kc_ref, kd_ref)
    v_refs = (va_ref, vb_ref, vc_ref, vd_ref)
    widths = (128, 256, 128, 256)
    kcol = Q_TOT
    vcol = Q_TOT + sum(widths)
    for m in range(4):
        for c in range(widths[m] // LANES):
            def k_unit(m=m, c=c, kcol=kcol):
                y = normed(kcol, q_seg[m], 2 * m + 1, q_rope[m], 1.0)
                k_refs[m][0, :, c * LANES:(c + 1) * LANES] = y.astype(BF16)

            def v_unit(m=m, c=c, vcol=vcol):
                v_refs[m][0, c * LANES:(c + 1) * LANES, :] = chunk(vcol).T.astype(BF16)
            units += [k_unit, v_unit]
            kcol += LANES
            vcol += LANES

    n_gate = GATE_TOT // tn
    per_gate = -(-len(units) // n_gate)
    for j in range(n_gate):
        g = _dot(h, w_ref[:, QKV_TOT + j * tn:QKV_TOT + (j + 1) * tn]) + bg_ref[:, j * tn:(j + 1) * tn]
        gate_ref[:, j * tn:(j + 1) * tn] = jax.nn.sigmoid(g).astype(BF16)
        for unit in units[j * per_gate:(j + 1) * per_gate]:
            unit()


def _in_proj(X, norm_g, mod, w_in_bf, b_gate, row_batch, tables, ones64, ones32, gains, B, S, C):
    R = _stream_rows(X)
    nt = R // ROW_TILE
    per = S // ROW_TILE
    nlat = B * per
    T = S + C
    x_specs, x_args = _stream_specs(X, nlat)

    def bidx(t):
        return jnp.where(t < nlat, t // per, t - nlat)

    def pidx(t):
        return jnp.where(t < nlat, t % per, per)

    tab_spec = pl.BlockSpec((ROW_TILE, LANES), lambda t: (pidx(t), 0))
    const_spec = pl.BlockSpec((LANES, LANES), lambda t: (0, 0))
    widths = (128, 256, 128, 256)
    q_specs = [pl.BlockSpec((1, 256, ROW_TILE), lambda t: (bidx(t), 0, pidx(t))) for _ in range(4)]
    k_specs = [pl.BlockSpec((1, ROW_TILE, w), lambda t: (bidx(t), pidx(t), 0)) for w in widths]
    v_specs = [pl.BlockSpec((1, w, ROW_TILE), lambda t: (bidx(t), 0, pidx(t))) for w in widths]
    q_shapes = [jax.ShapeDtypeStruct((B, 256, T), BF16) for _ in range(4)]
    k_shapes = [jax.ShapeDtypeStruct((B, T, w), BF16) for w in widths]
    v_shapes = [jax.ShapeDtypeStruct((B, w, T), BF16) for w in widths]
    outs = pl.pallas_call(
        functools.partial(_in_kernel, tn=512, n_lat_tiles=nlat),
        grid=(nt,),
        in_specs=x_specs + [
            pl.BlockSpec((1, D_MODEL), lambda t: (0, 0)),
            pl.BlockSpec((1, 1, N_MOD * D_MODEL), lambda t: (row_batch(t), 0, 0)),
            pl.BlockSpec((D_MODEL, QKV_TOT + GATE_TOT), lambda t: (0, 0), pipeline_mode=pl.Buffered(1)),
            pl.BlockSpec((1, GATE_TOT), lambda t: (0, 0)),
            tab_spec, tab_spec, tab_spec, tab_spec, const_spec, const_spec,
            pl.BlockSpec((8, LANES), lambda t: (0, 0)),
        ],
        out_specs=[pl.BlockSpec((ROW_TILE, GATE_TOT), lambda t: (t, 0))] + q_specs + k_specs + v_specs,
        out_shape=[jax.ShapeDtypeStruct((R, GATE_TOT), BF16)] + q_shapes + k_shapes + v_shapes,
        compiler_params=_cparams(("arbitrary",)),
        name="in_proj",
    )(*x_args, norm_g.reshape(1, D_MODEL), mod, w_in_bf, b_gate.reshape(1, GATE_TOT), *tables, ones64, ones32,
      gains)
    return outs[0], outs[1:5], outs[5:9], outs[9:13]


def _pad_queries(q, pieces, kw, unit):
    tq = q.shape[1]
    r = lax.broadcasted_iota(jnp.int32, (kw, tq), 0)
    blocks = []
    for row0, size, extra in pieces:
        tiled = jnp.concatenate([q[row0:row0 + size, :]] * (kw // size), axis=0)
        off = unit * HEAD_DIM + extra
        blocks.append(jnp.where((r >= off) & (r < off + size), tiled, 0.0))
    out = blocks[0] if len(blocks) == 1 else jnp.concatenate(blocks, axis=1)
    return out.astype(BF16)


_PIECES = {
    "gqa": ((0, 64, 0), (64, 64, 0)),
    "diff": ((0, 32, 0), (32, 32, 32)),
    "mha": ((0, 64, 0),),
}


def _flash_kernel(*refs, mode, kw, tq, max_keys, chunks, use_sink, lam_init, aliased):
    refs = list(refs)
    qT_ref, k_ref, vT_ref = refs[0:3]
    pos = 3
    sink_ref = None
    if use_sink:
        sink_ref = refs[pos]
        pos += 1
    if mode == "diff":
        lamvec_ref, subg_ref = refs[pos:pos + 2]
        pos += 2
    if aliased:
        pos += 1
    o_ref = refs[pos]
    s_bufs = refs[pos + 1:pos + 4]

    unit = pl.program_id(1)
    pieces = _PIECES[mode]
    ng = len(pieces)
    n = ng * tq
    qpad = _pad_queries(qT_ref[0].astype(F32), pieces, kw, unit)

    if use_sink:
        m0 = jnp.concatenate([jnp.full((1, tq), sink_ref[ng * unit + g] * LOG2E, F32) for g in range(ng)],
                             axis=1)
        l0 = jnp.ones((1, n), F32)
    else:
        m0 = jnp.full((1, n), NEG_INF, F32)
        l0 = jnp.zeros((1, n), F32)
    acc0 = jnp.zeros((HEAD_DIM, n), F32)
    ones_rows = jnp.ones((SUM_ROWS, max_keys), BF16)

    def scores(start, size, s_ref):
        s = _dot(k_ref[0, pl.ds(start, size), :], qpad)
        s_ref[0:size, :] = s
        return jnp.max(s, axis=0, keepdims=True)

    def absorb(start, size, s_ref, mc, state):
        m, l, acc = state
        m_new = jnp.maximum(m, mc)
        alpha = jnp.exp2(m - m_new)
        p = jnp.exp2(s_ref[0:size, :] - m_new)
        v_aug = jnp.concatenate([vT_ref[0, :, pl.ds(start, size)], ones_rows[:, 0:size]], axis=0)
        r = _dot(v_aug, p.astype(BF16))
        return m_new, alpha * l + r[HEAD_DIM:HEAD_DIM + 1], alpha * acc + r[0:HEAD_DIM]

    state = (m0, l0, acc0)
    first, main, n_main, tail = chunks

    def main_start(i):
        return pl.multiple_of(main[0] + i * main[1], LANES)

    mc = scores(first[0], first[1], s_bufs[0])
    if n_main == 0:
        state = absorb(first[0], first[1], s_bufs[0], mc, state)
    else:
        mc_next = scores(main_start(0), main[1], s_bufs[1])
        state = absorb(first[0], first[1], s_bufs[0], mc, state)
        mc = mc_next

        def triple(i, carry):
            mc, state = carry[0], carry[1:]
            k = 3 * i
            mc2 = scores(main_start(k + 1), main[1], s_bufs[2])
            state = absorb(main_start(k), main[1], s_bufs[1], mc, state)
            mc0 = scores(main_start(k + 2), main[1], s_bufs[0])
            state = absorb(main_start(k + 1), main[1], s_bufs[2], mc2, state)
            mc1 = scores(main_start(k + 3), main[1], s_bufs[1])
            state = absorb(main_start(k + 2), main[1], s_bufs[0], mc0, state)
            return (mc1,) + tuple(state)

        n_loop = (n_main - 1) // 3
        if n_loop > 0:
            carry = lax.fori_loop(0, n_loop, triple, (mc,) + tuple(state))
            mc, state = carry[0], carry[1:]
        last = main_start(n_main - 1)
        if tail[1] > 0:
            mc_next = scores(tail[0], tail[1], s_bufs[2])
            state = absorb(last, main[1], s_bufs[1], mc, state)
            state = absorb(tail[0], tail[1], s_bufs[2], mc_next, state)
        else:
            state = absorb(last, main[1], s_bufs[1], mc, state)
    m, l, acc = state
    o = acc / l

    if mode == "gqa":
        for g in range(ng):
            o_ref[0, g * HEAD_DIM:(g + 1) * HEAD_DIM, :] = o[:, g * tq:(g + 1) * tq].astype(o_ref.dtype)
    elif mode == "mha":
        o_ref[0] = o.astype(o_ref.dtype)
    else:
        lv = lamvec_ref[...]
        lam = (jnp.exp(jnp.sum(lv[0:1] * lv[1:2], axis=1, keepdims=True))
               - jnp.exp(jnp.sum(lv[2:3] * lv[3:4], axis=1, keepdims=True)) + lam_init)
        d = o[:, 0:tq] - lam * o[:, tq:2 * tq]
        ms = jnp.mean(d * d, axis=0, keepdims=True)
        o_ref[0] = (d * lax.rsqrt(ms + EPS) * subg_ref[...] * (1.0 - lam_init)).astype(o_ref.dtype)


def _flash(qT, k, vT, *, mode, S, C, ctx_only, sink=None, lamvec=None, subg=None, lam_init=0.0, out_prev=None):
    B, _, T = qT.shape
    kw = k.shape[2]
    q_rows = 128 if mode == "gqa" else 64
    units = 256 // q_rows
    if ctx_only:
        tq, nq, q_blk0 = C, 1, S // C
        chunks = ((S, C), (0, 0), 0, (0, 0))
    else:
        tq = FLASH_QUERIES if S % FLASH_QUERIES == 0 else 256
        nq, q_blk0 = S // tq, 0
        tk = FLASH_KEYS if S >= 4 * FLASH_KEYS else 256
        n_main = (S // tk - 1) // 3 * 3 + 1
        chunks = ((S, C), (0, tk), n_main, (n_main * tk, S - n_main * tk))
    max_keys = max(c[1] for c in (chunks[0], chunks[1], chunks[3]))
    in_specs = [
        pl.BlockSpec((1, q_rows, tq), lambda b, u, i: (b, u, i + q_blk0)),
        pl.BlockSpec((1, T, kw), lambda b, u, i: (b, 0, 0)),
        pl.BlockSpec((1, HEAD_DIM, T), lambda b, u, i: (b, u, 0)),
    ]
    args = [qT, k, vT]
    if sink is not None:
        in_specs.append(pl.BlockSpec(memory_space=pltpu.SMEM))
        args.append(sink)
    if mode == "diff":
        in_specs += [pl.BlockSpec((8, LANES), lambda b, u, i: (0, 0)),
                     pl.BlockSpec((HEAD_DIM, 1), lambda b, u, i: (0, 0))]
        args += [lamvec, subg]
    aliases = {}
    if out_prev is not None:
        in_specs.append(pl.BlockSpec(memory_space=pl.ANY))
        aliases = {len(args): 0}
        args.append(out_prev)
    kern = functools.partial(_flash_kernel, mode=mode, kw=kw, tq=tq, max_keys=max_keys, chunks=chunks,
                             use_sink=sink is not None, lam_init=lam_init,
                             aliased=out_prev is not None)
    return pl.pallas_call(
        kern,
        grid=(B, units, nq),
        in_specs=in_specs,
        out_specs=pl.BlockSpec((1, q_rows, tq), lambda b, u, i: (b, u, i + q_blk0)),
        out_shape=jax.ShapeDtypeStruct((B, 256, T), BF16),
        input_output_aliases=aliases,
        scratch_shapes=[pltpu.VMEM((max_keys, len(_PIECES[mode]) * tq), F32)] * 3,
        compiler_params=_cparams(("arbitrary", "arbitrary", "arbitrary")),
        name="flash_" + mode + ("_ctx" if ctx_only else ""),
    )(*args)


WINDOW_BLOCKS = 2


def _window_kernel(qT_ref, k_ref, vT_ref, sink_ref, o_ref, *, S, C):
    step = pl.program_id(1)
    qb = WINDOW
    span = 3 * qb
    n = 2 * qb
    k_ctx = k_ref[0, S:S + C, :]
    for bi in range(WINDOW_BLOCKS):
        i = step * WINDOW_BLOCKS + bi
        cols = slice(bi * qb, (bi + 1) * qb)
        start = pl.multiple_of(jnp.clip((i - 1) * qb, 0, S - span), LANES)
        k_loc = k_ref[0, pl.ds(start, span), :]
        kpos = start + lax.broadcasted_iota(jnp.int32, (span, n), 0)
        qpos = i * qb + lax.broadcasted_iota(jnp.int32, (span, n), 1) % qb
        in_window = jnp.abs(kpos - qpos) <= WINDOW
        for unit in range(2):
            q = qT_ref[0, unit * 128:(unit + 1) * 128, cols].astype(F32)
            qpad = _pad_queries(q, _PIECES["gqa"], LANES, unit)
            s_loc = jnp.where(in_window, _dot(k_loc, qpad), NEG_INF)
            s_ctx = _dot(k_ctx, qpad)
            sink = jnp.concatenate(
                [jnp.full((1, qb), sink_ref[2 * unit + g] * LOG2E, F32) for g in range(2)], axis=1)
            m = jnp.maximum(jnp.maximum(jnp.max(s_loc, axis=0, keepdims=True),
                                        jnp.max(s_ctx, axis=0, keepdims=True)), sink)
            e_loc = jnp.exp2(s_loc - m)
            e_ctx = jnp.exp2(s_ctx - m)
            den = (jnp.sum(e_loc, axis=0, keepdims=True) + jnp.sum(e_ctx, axis=0, keepdims=True)
                   + jnp.exp2(sink - m))
            v_rows = slice(unit * HEAD_DIM, (unit + 1) * HEAD_DIM)
            o = (_dot(vT_ref[0, v_rows, pl.ds(start, span)], e_loc.astype(BF16))
                 + _dot(vT_ref[0, v_rows, S:S + C], e_ctx.astype(BF16))) / den
            for g in range(2):
                head = 2 * unit + g
                o_ref[0, head * HEAD_DIM:(head + 1) * HEAD_DIM, cols] = (
                    o[:, g * qb:(g + 1) * qb].astype(o_ref.dtype))


def _window(qT, k, vT, sink, S, C):
    B, _, T = qT.shape
    tq = WINDOW * WINDOW_BLOCKS
    return pl.pallas_call(
        functools.partial(_window_kernel, S=S, C=C),
        grid=(B, S // tq),
        in_specs=[
            pl.BlockSpec((1, 256, tq), lambda b, i: (b, 0, i)),
            pl.BlockSpec((1, T, LANES), lambda b, i: (b, 0, 0)),
            pl.BlockSpec((1, 2 * HEAD_DIM, T), lambda b, i: (b, 0, 0)),
            pl.BlockSpec(memory_space=pltpu.SMEM),
        ],
        out_specs=pl.BlockSpec((1, 256, tq), lambda b, i: (b, 0, i)),
        out_shape=jax.ShapeDtypeStruct((B, 256, T), BF16),
        compiler_params=_cparams(("arbitrary", "arbitrary")),
        name="window_attn",
    )(qT, k, vT, sink)


def _nbr_kernel(qT_ref, k_ref, vT_ref, tab_ref, o_ref, *, S, C):
    j = pl.program_id(1)
    rows = S // GRID_W
    n = NBR_Q_ROWS * GRID_W
    span = NBR_WIN_ROWS * GRID_W
    w0 = jnp.clip(NBR_Q_ROWS * j - NA_ROWS // 2, 0, rows - NBR_WIN_ROWS)
    shift = w0 - NBR_Q_ROWS * j + NA_ROWS // 2 + 4
    start = pl.multiple_of(w0 * GRID_W, LANES)
    k_loc = k_ref[0, pl.ds(start, span), :]
    k_ctx = k_ref[0, S:S + C, :]
    kr = w0 + lax.broadcasted_iota(jnp.int32, (span, n), 0) // GRID_W
    qr = NBR_Q_ROWS * j + lax.broadcasted_iota(jnp.int32, (span, n), 1) // GRID_W
    r0 = jnp.clip(qr - NA_ROWS // 2, 0, rows - NA_ROWS)
    in_rows = (kr >= r0) & (kr < r0 + NA_ROWS)
    for head in range(4):
        h_rows = slice(head * HEAD_DIM, (head + 1) * HEAD_DIM)
        qpad = _pad_queries(qT_ref[0, h_rows, :].astype(F32), _PIECES["mha"], 256, head)
        bias = tab_ref[head, pl.ds(pl.multiple_of(shift * GRID_W, GRID_W), span), :]
        s_loc = jnp.where(in_rows, _dot(k_loc, qpad) + bias, NEG_INF)
        s_ctx = _dot(k_ctx, qpad)
        m = jnp.maximum(jnp.max(s_loc, axis=0, keepdims=True), jnp.max(s_ctx, axis=0, keepdims=True))
        e_loc = jnp.exp2(s_loc - m)
        e_ctx = jnp.exp2(s_ctx - m)
        den = jnp.sum(e_loc, axis=0, keepdims=True) + jnp.sum(e_ctx, axis=0, keepdims=True)
        o = (_dot(vT_ref[0, h_rows, pl.ds(start, span)], e_loc.astype(BF16))
             + _dot(vT_ref[0, h_rows, S:S + C], e_ctx.astype(BF16))) / den
        o_ref[0, h_rows, :] = o.astype(o_ref.dtype)


def _nbr_bias_table(rpb):
    u = np.arange(NBR_TABLE_ROWS)[:, None, None, None]
    kc = np.arange(GRID_W)[None, :, None, None]
    e = np.arange(NBR_Q_ROWS)[None, None, :, None]
    qc = np.arange(GRID_W)[None, None, None, :]
    dr = u - e - 1
    row_ok = (dr >= 0) & (dr < 2 * NA_ROWS - 1)
    dc = np.clip(kc - qc, -(NA_COLS - 1), NA_COLS - 1) + (NA_COLS - 1)
    c0 = np.clip(qc - NA_COLS // 2, 0, GRID_W - NA_COLS)
    col_ok = (kc >= c0) & (kc < c0 + NA_COLS)
    shape = (NBR_TABLE_ROWS, GRID_W, NBR_Q_ROWS, GRID_W)
    n_dr, n_dc = 2 * NA_ROWS - 1, 2 * NA_COLS - 1
    sel_r = ((dr[:, 0, :, 0, None] == np.arange(n_dr)) & row_ok[:, 0, :, 0, None]).astype(np.float32)
    sel_c = (dc[0, :, 0, :, None] == np.arange(n_dc)).astype(np.float32)
    vals = jnp.einsum("uer,hrc,kqc->hukeq", sel_r, rpb.astype(F32) * LOG2E, sel_c,
                      precision=lax.Precision.HIGHEST)
    vals = jnp.where(np.broadcast_to(col_ok, shape), vals, NEG_INF)
    return vals.reshape(rpb.shape[0], NBR_TABLE_ROWS * GRID_W, NBR_Q_ROWS * GRID_W)


def _nbr(qT, k, vT, table, S, C):
    B, _, T = qT.shape
    tq = NBR_Q_ROWS * GRID_W
    return pl.pallas_call(
        functools.partial(_nbr_kernel, S=S, C=C),
        grid=(B, S // tq),
        in_specs=[
            pl.BlockSpec((1, 256, tq), lambda b, j: (b, 0, j)),
            pl.BlockSpec((1, T, 256), lambda b, j: (b, 0, 0)),
            pl.BlockSpec((1, 256, T), lambda b, j: (b, 0, 0)),
            pl.BlockSpec((4, NBR_TABLE_ROWS * GRID_W, tq), lambda b, j: (0, 0, 0)),
        ],
        out_specs=pl.BlockSpec((1, 256, tq), lambda b, j: (b, 0, j)),
        out_shape=jax.ShapeDtypeStruct((B, 256, T), BF16),
        compiler_params=_cparams(("arbitrary", "arbitrary")),
        name="nbr_attn",
    )(qT, k, vT, table)


ROUTE_W, ROUTE_E, ROUTE_RANK = 0, 4, 8
ROUTE_ROWS = 16
MOE_GROUPS = 1


def _merge_kernel(oa_ref, ob_ref, oc_ref, od_ref, gate_ref, xa_ref, xb_ref, mod_ref, wb_ref, wo_ref, n2_ref,
                  rw_ref, rb_ref, tri_ref, xo_ref, h_ref, route_ref, cnt_ref, base_ref, *, group_tiles,
                  n_lat_tiles):
    @pl.when(pl.program_id(0) % group_tiles == 0)
    def _():
        base_ref[...] = jnp.zeros_like(base_ref)

    acc = None
    for nbr, o_ref in enumerate((oa_ref, ob_ref, oc_ref, od_ref)):
        proj = lax.dot_general(o_ref[0], wb_ref[nbr], (((0,), (0,)), ((), ())),
                               preferred_element_type=F32)
        term = gate_ref[:, nbr * D_MODEL:(nbr + 1) * D_MODEL].astype(F32) * proj
        acc = term if acc is None else acc + term
    mix = _dot(acc.astype(BF16), wo_ref[...])
    mod = mod_ref[0]
    g1 = mod[:, 2 * D_MODEL:3 * D_MODEL]
    sh2 = mod[:, 3 * D_MODEL:4 * D_MODEL]
    sc2 = mod[:, 4 * D_MODEL:5 * D_MODEL]
    xn = jnp.where(pl.program_id(0) < n_lat_tiles, xa_ref[...], xb_ref[...]) + g1 * mix
    xo_ref[...] = xn
    ms = jnp.mean(xn * xn, axis=-1, keepdims=True)
    h = (xn * lax.rsqrt(ms + EPS) * n2_ref[...]) * (1 + sc2) + sh2
    h_ref[...] = h
    logits = lax.dot_general(rw_ref[...], h.astype(BF16), (((1,), (1,)), ((), ())),
                             preferred_element_type=F32) + rb_ref[...]
    row_f = lax.broadcasted_iota(jnp.int32, logits.shape, 0).astype(F32)
    work = logits
    picks = []
    for _ in range(TOP_K):
        top = jnp.max(work, axis=0, keepdims=True)
        idx = jnp.min(jnp.where(work == top, row_f, float(N_EXPERTS)), axis=0, keepdims=True)
        hit = row_f == idx
        picks.append((top, idx, hit))
        work = jnp.where(hit, -jnp.inf, work)
    ex = [jnp.exp(top - picks[0][0]) for top, _, _ in picks]
    den = ex[0] + ex[1] + ex[2] + ex[3]
    chosen = jnp.zeros(logits.shape, F32)
    for _, _, hit in picks:
        chosen = chosen + hit.astype(F32)
    base = base_ref[...][:, 0:1]
    before = _dot(chosen.astype(BF16), tri_ref[...]) + base
    fields = ([ex[k] / den for k in range(TOP_K)] + [idx for _, idx, _ in picks]
              + [jnp.sum(jnp.where(hit, before, 0.0), axis=0, keepdims=True) for _, _, hit in picks])
    fields.append(jnp.zeros((ROUTE_ROWS - len(fields), logits.shape[1]), F32))
    route_ref[...] = jnp.concatenate(fields, axis=0)
    base_ref[...] = base_ref[...] + jnp.sum(chosen, axis=1, keepdims=True)
    cnt_ref[0] = base_ref[...]


def _merge(oTs, gate, X, mod, wb_bf, wo_bf, norm2_g, rw_t, rb_col, n_rows, B, S, C, row_batch):
    nt = n_rows // ROW_TILE
    per = S // ROW_TILE
    nlat = B * per

    def bidx(t):
        return jnp.where(t < nlat, t // per, t - nlat)

    def pidx(t):
        return jnp.where(t < nlat, t % per, per)

    o_spec = pl.BlockSpec((1, 256, ROW_TILE), lambda t: (bidx(t), 0, pidx(t)))
    x_specs, x_args = _stream_specs(X, nlat)
    r = np.arange(ROW_TILE)
    strict_upper = jnp.asarray((r[:, None] < r[None, :]).astype(np.float32), dtype=BF16)
    group_tiles = nt // MOE_GROUPS
    assert nt % MOE_GROUPS == 0
    return pl.pallas_call(
        functools.partial(_merge_kernel, group_tiles=group_tiles, n_lat_tiles=nlat),
        grid=(nt,),
        in_specs=[o_spec, o_spec, o_spec, o_spec,
                  pl.BlockSpec((ROW_TILE, GATE_TOT), lambda t: (t, 0))] + x_specs + [
                  pl.BlockSpec((1, 1, N_MOD * D_MODEL), lambda t: (row_batch(t), 0, 0)),
                  pl.BlockSpec((4, 256, D_MODEL), lambda t: (0, 0, 0)),
                  pl.BlockSpec((D_MODEL, D_MODEL), lambda t: (0, 0)),
                  pl.BlockSpec((1, D_MODEL), lambda t: (0, 0)),
                  pl.BlockSpec((N_EXPERTS, D_MODEL), lambda t: (0, 0)),
                  pl.BlockSpec((N_EXPERTS, 1), lambda t: (0, 0)),
                  pl.BlockSpec((ROW_TILE, ROW_TILE), lambda t: (0, 0))],
        out_specs=[pl.BlockSpec((ROW_TILE, D_MODEL), lambda t: (t, 0)),
                   pl.BlockSpec((ROW_TILE, D_MODEL), lambda t: (t, 0)),
                   pl.BlockSpec((ROUTE_ROWS, ROW_TILE), lambda t: (0, t)),
                   pl.BlockSpec((1, N_EXPERTS, LANES), lambda t: (t // group_tiles, 0, 0))],
        out_shape=[jax.ShapeDtypeStruct((n_rows, D_MODEL), F32),
                   jax.ShapeDtypeStruct((n_rows, D_MODEL), F32),
                   jax.ShapeDtypeStruct((ROUTE_ROWS, n_rows), F32),
                   jax.ShapeDtypeStruct((MOE_GROUPS, N_EXPERTS, LANES), F32)],
        scratch_shapes=[pltpu.VMEM((N_EXPERTS, LANES), F32)],
        compiler_params=_cparams(("arbitrary",)),
        name="merge",
    )(*oTs, gate, *x_args, mod, wb_bf, wo_bf, norm2_g.reshape(1, D_MODEL), rw_t, rb_col, strict_upper)


def _expert_kernel(tok_ref, off_ref, first_ref, nblk_ref, nu_ref, h_hbm, wgu_ref, bgu_ref, wd_ref, bd_ref,
                   y_hbm, x_0, x_1, x_2, y_0, y_1, y_2, gsems, ysems, wgu_s, wd_s):
    e = pl.program_id(0)
    n_used = nu_ref[0]
    x_bufs = (x_0, x_1, x_2)
    y_bufs = (y_0, y_1, y_2)

    def start_gather(block, x_dst, sem):
        base = off_ref[block]
        for r in range(EXPERT_ROWS):
            pltpu.make_async_copy(h_hbm.at[pl.ds(tok_ref[base + r], 1), :], x_dst.at[pl.ds(r, 1), :],
                                  sem).start()

    def wait_gather(x_dst, sem):
        pltpu.make_async_copy(h_hbm.at[pl.ds(0, EXPERT_ROWS), :], x_dst, sem).wait()

    def y_copy(g, cur):
        rows = pl.ds(pl.multiple_of(g * EXPERT_ROWS, EXPERT_ROWS), EXPERT_ROWS)
        return pltpu.make_async_copy(y_bufs[cur], y_hbm.at[rows, :], ysems.at[cur])

    def block(cur, g):
        ahead = (cur + 2) % 3

        @pl.when(g >= 3)
        def _():
            y_copy(g - 3, cur).wait()

        wait_gather(x_bufs[cur], gsems.at[cur])
        start_gather(jnp.minimum(g + 2, n_used - 1), x_bufs[ahead], gsems.at[ahead])
        x = x_bufs[cur][...].astype(BF16)
        gu = _dot(x, wgu_s[...]) + bgu_ref[0]
        gate = jnp.minimum(gu[:, :D_FF], SWIGLU_LIMIT)
        up = jnp.clip(gu[:, D_FF:], -SWIGLU_LIMIT, SWIGLU_LIMIT)
        a = gate * jax.nn.sigmoid(SWIGLU_ALPHA * gate) * (up + 1)
        y_bufs[cur][...] = (_dot(a.astype(BF16), wd_s[...]) + bd_ref[0]).astype(BF16)
        y_copy(g, cur).start()

    @pl.when(e == 0)
    def _():
        start_gather(0, x_bufs[0], gsems.at[0])
        start_gather(jnp.minimum(1, n_used - 1), x_bufs[1], gsems.at[1])

    n_blocks = nblk_ref[e]

    @pl.when(n_blocks > 0)
    def _():
        wgu_s[...] = wgu_ref[0].astype(BF16)
        wd_s[...] = wd_ref[0].astype(BF16)

    def body(j, carry):
        g = first_ref[e] + j
        for cur in range(3):
            pl.when(g % 3 == cur)(functools.partial(block, cur, g))
        return carry

    lax.fori_loop(0, n_blocks, body, 0)

    @pl.when(e == pl.num_programs(0) - 1)
    def _():
        last = (n_used - 1) % 3
        for b in range(3):
            @pl.when(b != last)
            def _():
                wait_gather(x_bufs[b], gsems.at[b])

            @pl.when(n_used > b)
            def _():
                y_copy(0, b).wait()


def _experts(h, tok_sorted, blk_off, e_first, e_nblk, n_used, n_blk, layer, w_gate_up, b_gate_up, w_down,
             b_down):
    E = w_gate_up.shape[0]

    def w_map(e, *_):
        return (e + layer * N_EXPERTS, 0, 0)

    grid_spec = pltpu.PrefetchScalarGridSpec(
        num_scalar_prefetch=5,
        grid=(N_EXPERTS,),
        in_specs=[
            pl.BlockSpec(memory_space=pl.ANY),
            pl.BlockSpec((1, D_MODEL, 2 * D_FF), w_map),
            pl.BlockSpec((1, 1, 2 * D_FF), w_map),
            pl.BlockSpec((1, D_FF, D_MODEL), w_map),
            pl.BlockSpec((1, 1, D_MODEL), w_map),
        ],
        out_specs=pl.BlockSpec(memory_space=pl.ANY),
        scratch_shapes=([pltpu.VMEM((EXPERT_ROWS, D_MODEL), F32)] * 3
                        + [pltpu.VMEM((EXPERT_ROWS, D_MODEL), BF16)] * 3
                        + [pltpu.SemaphoreType.DMA((3,)), pltpu.SemaphoreType.DMA((3,)),
                           pltpu.VMEM((D_MODEL, 2 * D_FF), BF16), pltpu.VMEM((D_FF, D_MODEL), BF16)]),
    )
    return pl.pallas_call(
        _expert_kernel,
        grid_spec=grid_spec,
        out_shape=jax.ShapeDtypeStruct((n_blk * EXPERT_ROWS, D_MODEL), BF16),
        compiler_params=_cparams(("arbitrary",)),
        name="experts",
    )(tok_sorted, blk_off, e_first, e_nblk, n_used, h, w_gate_up, b_gate_up.reshape(E, 1, 2 * D_FF), w_down,
      b_down.reshape(E, 1, D_MODEL))


def _combine_kernel(yg_ref, route_ref, x_ref, mod_ref, *rest):
    o_ref = rest[-1]
    route = route_ref[...]
    y = None
    for k in range(TOP_K):
        term = route[:, ROUTE_W + k:ROUTE_W + k + 1] * yg_ref[k].astype(F32)
        y = term if y is None else y + term
    g2 = mod_ref[0][:, 5 * D_MODEL:6 * D_MODEL]
    o_ref[...] = x_ref[...] + g2 * y


def _combine(yg, route, X, mod, row_batch, tile0, out_prev):
    n_rows = X.shape[0]
    in_specs = [pl.BlockSpec((TOP_K, ROW_TILE, D_MODEL), lambda t: (0, t, 0)),
                pl.BlockSpec((ROW_TILE, ROUTE_ROWS), lambda t: (t + tile0, 0)),
                pl.BlockSpec((ROW_TILE, D_MODEL), lambda t: (t + tile0, 0)),
                pl.BlockSpec((1, 1, N_MOD * D_MODEL), lambda t: (row_batch(t + tile0), 0, 0))]
    args = [yg, route, X, mod]
    aliases = {}
    if out_prev is not None:
        in_specs.append(pl.BlockSpec(memory_space=pl.ANY))
        aliases = {len(args): 0}
        args.append(out_prev)
    return pl.pallas_call(
        _combine_kernel,
        grid=(yg.shape[1] // ROW_TILE,),
        in_specs=in_specs,
        out_specs=pl.BlockSpec((ROW_TILE, D_MODEL), lambda t: (t + tile0, 0)),
        out_shape=jax.ShapeDtypeStruct((n_rows, D_MODEL), F32),
        input_output_aliases=aliases,
        compiler_params=_cparams(("arbitrary",)),
        name="moe_combine",
    )(*args)


def _moe(h, route, counts_f, X, mod, row_batch, layer, w_gate_up, b_gate_up, w_down, b_down):
    n_groups = counts_f.shape[0]
    Tg = h.shape[0] // n_groups
    n = Tg * TOP_K
    n_blk = n // EXPERT_ROWS + N_EXPERTS
    experts = jnp.arange(N_EXPERTS, dtype=jnp.int32)
    out = None
    for g in range(n_groups):
        route_g = lax.slice_in_dim(route, g * Tg, (g + 1) * Tg, axis=0)
        top_e = route_g[:, ROUTE_E:ROUTE_E + TOP_K].astype(jnp.int32)
        rank = route_g[:, ROUTE_RANK:ROUTE_RANK + TOP_K].astype(jnp.int32)
        counts = counts_f[g, :, 0].astype(jnp.int32)
        padded = (counts + EXPERT_ROWS - 1) // EXPERT_ROWS * EXPERT_ROWS
        pad_end = jnp.cumsum(padded)
        pad_start = pad_end - padded
        start_of = jnp.sum(jnp.where(top_e[:, :, None] == experts, pad_start, 0), axis=-1)
        dest = start_of + rank
        tok = jnp.broadcast_to(jnp.arange(g * Tg, (g + 1) * Tg, dtype=jnp.int32)[:, None], (Tg, TOP_K))
        _, tok_sorted = lax.sort_key_val(dest.reshape(-1), tok.reshape(-1))
        tok_sorted = jnp.concatenate([tok_sorted, jnp.zeros((EXPERT_ROWS,), jnp.int32)])
        n_used = (pad_end[-1] // EXPERT_ROWS).astype(jnp.int32)
        blk_row = jnp.minimum(jnp.arange(n_blk, dtype=jnp.int32), n_used - 1) * EXPERT_ROWS
        blk_e = jnp.sum((pad_end[None, :] <= blk_row[:, None]).astype(jnp.int32), axis=1)
        blk_e = jnp.minimum(blk_e, N_EXPERTS - 1)
        pad_before = pad_start - (jnp.cumsum(counts) - counts)
        blk_off = blk_row - jnp.sum(jnp.where(blk_e[:, None] == experts, pad_before, 0), axis=-1)
        y = _experts(h, tok_sorted, blk_off, pad_start // EXPERT_ROWS, padded // EXPERT_ROWS,
                     n_used.reshape(1), n_blk, layer, w_gate_up, b_gate_up, w_down, b_down)
        yg = y.at[dest.T.reshape(-1)].get(mode="promise_in_bounds").reshape(TOP_K, Tg, D_MODEL)
        out = _combine(yg, route, X, mod, row_batch, g * (Tg // ROW_TILE), out)
    return out


def _rope_tables(S, dim):
    t = jnp.arange(S, dtype=jnp.int32)
    row = (t // GRID_W).astype(F32)
    col = (t % GRID_W).astype(F32)
    n_freq = dim // 4
    inv = ROPE_THETA ** (-jnp.arange(n_freq, dtype=F32) / n_freq)
    ang = jnp.concatenate([row[:, None] * inv, col[:, None] * inv], axis=-1)
    cos, sin = jnp.cos(ang), jnp.sin(ang)
    reps = LANES // dim
    cos_t = jnp.tile(jnp.concatenate([cos, cos], axis=-1), (1, reps))
    sin_t = jnp.tile(jnp.concatenate([-sin, sin], axis=-1), (1, reps))
    cos_t = jnp.concatenate([cos_t, jnp.ones((ROW_TILE, LANES), F32)], axis=0)
    sin_t = jnp.concatenate([sin_t, jnp.zeros((ROW_TILE, LANES), F32)], axis=0)
    return cos_t, sin_t


def _block_ones(seg):
    i = np.arange(LANES)
    return jnp.asarray((i[:, None] // seg == i[None, :] // seg).astype(np.float32), dtype=BF16)


def _lane_tile(v):
    return jnp.tile(v.astype(F32), LANES // v.shape[0])


def kernel(x, c, ctx, c_ctx, norm1_g, norm2_g, w_ada, b_ada, w_in, b_gate, a_qn, a_kn, b_qn, b_kn, lam_q1, lam_k1, lam_q2, lam_k2, subln_g, c_qn, c_kn, sink, d_qn, d_kn, rpb, w_branch, w_out, router_w, router_b, w_gate_up, b_gate_up, w_down, b_down):
    B, S, D = x.shape
    C = ctx.shape[1]
    L = w_in.shape[0]
    assert D == D_MODEL and C == ROW_TILE and S % ROW_TILE == 0 and B + 1 <= 8
    n_lat = B * S
    per = S // ROW_TILE
    nlat_tiles = B * per

    def row_batch(t):
        return jnp.where(t < nlat_tiles, t // per, B)

    cvec = jnp.zeros((8, D), F32).at[:B].set(c).at[B].set(c_ctx)
    mod_all = _ada(cvec, w_ada, b_ada)

    tables = _rope_tables(S, HEAD_DIM) + _rope_tables(S, B_DK)
    ones64, ones32 = _block_ones(HEAD_DIM), _block_ones(B_DK)

    E = w_gate_up.shape[1]
    w_gu_all = w_gate_up.reshape(L * E, D, 2 * D_FF)
    w_dn_all = w_down.reshape(L * E, D_FF, D)

    X = (x.reshape(n_lat, D), ctx.reshape(B * C, D))
    for l in range(L):
        last = l == L - 1
        lam_init = 0.8 - 0.6 * math.exp(-0.3 * l)
        mod = mod_all[l].reshape(8, 1, N_MOD * D)
        gains = jnp.stack([_lane_tile(g[l]) for g in (a_qn, a_kn, b_qn, b_kn, c_qn, c_kn, d_qn, d_kn)])
        gate, qTs, ks, vTs = _in_proj(X, norm1_g[l], mod, w_in[l].astype(BF16), b_gate[l], row_batch,
                                      tables, ones64, ones32, gains, B, S, C)

        lamvec = jnp.zeros((8, LANES), F32)
        for r, v in enumerate((lam_q1, lam_k1, lam_q2, lam_k2)):
            lamvec = lamvec.at[r, :B_DK].set(v[l])
        subg = subln_g[l].reshape(HEAD_DIM, 1)
        sink_l = sink[l].astype(F32)
        nbr_table = _nbr_bias_table(rpb[l])

        o_a = _flash(qTs[0], ks[0], vTs[0], mode="gqa", S=S, C=C, ctx_only=False)
        o_b = _flash(qTs[1], ks[1], vTs[1], mode="diff", S=S, C=C, ctx_only=False,
                     lamvec=lamvec, subg=subg, lam_init=lam_init)
        o_c = _window(qTs[2], ks[2], vTs[2], sink_l, S, C)
        o_d = _nbr(qTs[3], ks[3], vTs[3], nbr_table, S, C)
        if not last:
            o_a = _flash(qTs[0], ks[0], vTs[0], mode="gqa", S=S, C=C, ctx_only=True, out_prev=o_a)
            o_b = _flash(qTs[1], ks[1], vTs[1], mode="diff", S=S, C=C, ctx_only=True,
                         lamvec=lamvec, subg=subg, lam_init=lam_init, out_prev=o_b)
            o_c = _flash(qTs[2], ks[2], vTs[2], mode="gqa", S=S, C=C, ctx_only=True, sink=sink_l, out_prev=o_c)
            o_d = _flash(qTs[3], ks[3], vTs[3], mode="mha", S=S, C=C, ctx_only=True, out_prev=o_d)

        n_rows = n_lat if last else _stream_rows(X)
        Xm, h2, route_t, counts = _merge((o_a, o_b, o_c, o_d), gate, X, mod, w_branch[l].astype(BF16),
                                         w_out[l].astype(BF16), norm2_g[l], router_w[l].T.astype(BF16),
                                         router_b[l].astype(F32).reshape(N_EXPERTS, 1), n_rows, B, S, C,
                                         row_batch)
        X = _moe(h2, route_t.T, counts, Xm, mod, row_batch, l, w_gu_all, b_gate_up.reshape(L * E, -1),
                 w_dn_all, b_down.reshape(L * E, -1))
    return X[:n_lat].reshape(B, S, D)
```

```python
import functools
import math

import numpy as np
import jax
import jax.numpy as jnp
from jax import lax
from jax.experimental import pallas as pl
from jax.experimental.pallas import tpu as pltpu

F32 = jnp.float32
BF16 = jnp.bfloat16

D_MODEL = 1024
GRID_W = 64
HEAD_DIM = 64
B_DK = 32
WINDOW = 128
NA_ROWS = 8
NA_COLS = 16
ROPE_THETA = 10000.0
N_EXPERTS = 32
TOP_K = 4
D_FF = D_MODEL
SWIGLU_LIMIT = 7.0
SWIGLU_ALPHA = 1.702
N_MOD = 6
EPS = 1e-6
NEG_INF = -1e30
LOG2E = 1.4426950408889634

Q_TOT = 1024
KV_TOT = 1536
QKV_TOT = Q_TOT + KV_TOT
GATE_TOT = 4 * D_MODEL

ROW_TILE = 256
LANES = 128
EXPERT_ROWS = 256
NBR_Q_ROWS = 4
NBR_WIN_ROWS = NA_ROWS + NBR_Q_ROWS
NBR_TABLE_ROWS = NBR_WIN_ROWS + 8
FLASH_KEYS = 1152
FLASH_QUERIES = 512
SUM_ROWS = 16
VMEM_LIMIT = 52 * 1024 * 1024


def _cparams(sem):
    return pltpu.CompilerParams(dimension_semantics=sem, vmem_limit_bytes=VMEM_LIMIT)


def _dot(a, b):
    return jnp.dot(a, b, preferred_element_type=F32)


def _ada_kernel(c_ref, w_ref, b_ref, o_ref):
    c = c_ref[...]
    s = c * jax.nn.sigmoid(c)
    o_ref[0] = jnp.dot(s, w_ref[0], precision=lax.Precision.HIGHEST,
                       preferred_element_type=F32) + b_ref[0]


def _ada(cvec, w_ada, b_ada):
    L = w_ada.shape[0]
    n_out = w_ada.shape[2]
    tn = 1536
    return pl.pallas_call(
        _ada_kernel,
        grid=(L, n_out // tn),
        in_specs=[
            pl.BlockSpec((8, D_MODEL), lambda l, j: (0, 0)),
            pl.BlockSpec((1, D_MODEL, tn), lambda l, j: (l, 0, j)),
            pl.BlockSpec((1, 1, tn), lambda l, j: (l, 0, j)),
        ],
        out_specs=pl.BlockSpec((1, 8, tn), lambda l, j: (l, 0, j)),
        out_shape=jax.ShapeDtypeStruct((L, 8, n_out), F32),
        compiler_params=_cparams(("arbitrary", "arbitrary")),
        name="ada_mod",
    )(cvec, w_ada, b_ada.reshape(L, 1, n_out))


def _seg_rms(x, ones, seg, g):
    sq = x * x
    hi = sq.astype(BF16)
    lo = (sq - hi.astype(F32)).astype(BF16)
    ss = _dot(hi, ones) + _dot(lo, ones)
    return x * lax.rsqrt(ss * (1.0 / seg) + EPS) * g


def _rot_half(y, half):
    lane = lax.broadcasted_iota(jnp.int32, y.shape, 1)
    fwd = pltpu.roll(y, LANES - half, axis=1)
    bwd = pltpu.roll(y, half, axis=1)
    return jnp.where((lane % (2 * half)) < half, fwd, bwd)


def _stream_specs(X, n_lat_tiles):
    lat, ctx_rows, ctx_tile0 = (X[0], X[1], 0) if isinstance(X, tuple) else (X, X, n_lat_tiles)
    specs = [pl.BlockSpec((ROW_TILE, D_MODEL), lambda t: (jnp.minimum(t, n_lat_tiles - 1), 0)),
             pl.BlockSpec((ROW_TILE, D_MODEL), lambda t: (ctx_tile0 + jnp.maximum(t - n_lat_tiles, 0), 0))]
    return specs, [lat, ctx_rows]


def _stream_rows(X):
    return X[0].shape[0] + X[1].shape[0] if isinstance(X, tuple) else X.shape[0]


def _in_kernel(xa_ref, xb_ref, g_ref, mod_ref, w_ref, bg_ref, c64_ref, s64_ref, c32_ref, s32_ref, ones64_ref,
               ones32_ref, gains_ref, gate_ref, qa_ref, qb_ref, qc_ref, qd_ref, ka_ref, kb_ref, kc_ref, kd_ref,
               va_ref, vb_ref, vc_ref, vd_ref, *, tn, n_lat_tiles):
    x = jnp.where(pl.program_id(0) < n_lat_tiles, xa_ref[...], xb_ref[...])
    ms = jnp.mean(x * x, axis=-1, keepdims=True)
    y = x * lax.rsqrt(ms + EPS) * g_ref[...]
    mod = mod_ref[0]
    sh = mod[:, 0:D_MODEL]
    sc = mod[:, D_MODEL:2 * D_MODEL]
    h = (y * (1 + sc) + sh).astype(BF16)
    qkv = [_dot(h, w_ref[:, j * tn:(j + 1) * tn]) for j in range(QKV_TOT // tn)]

    gains = gains_ref[...]
    rope = {64: (c64_ref, s64_ref), 32: (c32_ref, s32_ref)}
    ones = {64: ones64_ref, 32: ones32_ref}

    def chunk(col):
        return qkv[col // tn][:, col % tn:col % tn + LANES]

    def normed(col, seg, gain_row, use_rope, scale):
        y = _seg_rms(chunk(col), ones[seg][...], seg, gains[gain_row:gain_row + 1, :])
        if use_rope:
            y = y * rope[seg][0][...] + _rot_half(y, seg // 2) * rope[seg][1][...]
        return y * scale if scale != 1.0 else y

    units = []
    q_refs = (qa_ref, qb_ref, qc_ref, qd_ref)
    q_seg = (64, 32, 64, 64)
    q_rope = (True, True, True, False)
    for m in range(4):
        scale = float(q_seg[m]) ** -0.5 * LOG2E
        for c in range(2):
            def q_unit(m=m, c=c, scale=scale):
                y = normed(m * 256 + c * LANES, q_seg[m], 2 * m, q_rope[m], scale)
                q_refs[m][0, c * LANES:(c + 1) * LANES, :] = y.T.astype(BF16)
            units.append(q_unit)
    k_refs = (ka_ref, kb_ref, kc_ref, kd_ref)
    v_refs = (va_ref, vb_ref, vc_ref, vd_ref)
    widths = (128, 256, 128, 256)
    kcol = Q_TOT
    vcol = Q_TOT + sum(widths)
    for m in range(4):
        for c in range(widths[m] // LANES):
            def k_unit(m=m, c=c, kcol=kcol):
                y = normed(kcol, q_seg[m], 2 * m + 1, q_rope[m], 1.0)
                k_refs[m][0, :, c * LANES:(c + 1) * LANES] = y.astype(BF16)

            def v_unit(m=m, c=c, vcol=vcol):
                v_refs[m][0, c * LANES:(c + 1) * LANES, :] = chunk(vcol).T.astype(BF16)
            units += [k_unit, v_unit]
            kcol += LANES
            vcol += LANES

    n_gate = GATE_TOT // tn
    per_gate = -(-len(units) // n_gate)
    for j in range(n_gate):
        g = _dot(h, w_ref[:, QKV_TOT + j * tn:QKV_TOT + (j + 1) * tn]) + bg_ref[:, j * tn:(j + 1) * tn]
        gate_ref[:, j * tn:(j + 1) * tn] = jax.nn.sigmoid(g).astype(BF16)
        for unit in units[j * per_gate:(j + 1) * per_gate]:
            unit()


def _in_proj(X, norm_g, mod, w_in_bf, b_gate, row_batch, tables, ones64, ones32, gains, B, S, C):
    R = _stream_rows(X)
    nt = R // ROW_TILE
    per = S // ROW_TILE
    nlat = B * per
    T = S + C
    x_specs, x_args = _stream_specs(X, nlat)

    def bidx(t):
        return jnp.where(t < nlat, t // per, t - nlat)

    def pidx(t):
        return jnp.where(t < nlat, t % per, per)

    tab_spec = pl.BlockSpec((ROW_TILE, LANES), lambda t: (pidx(t), 0))
    const_spec = pl.BlockSpec((LANES, LANES), lambda t: (0, 0))
    widths = (128, 256, 128, 256)
    q_specs = [pl.BlockSpec((1, 256, ROW_TILE), lambda t: (bidx(t), 0, pidx(t))) for _ in range(4)]
    k_specs = [pl.BlockSpec((1, ROW_TILE, w), lambda t: (bidx(t), pidx(t), 0)) for w in widths]
    v_specs = [pl.BlockSpec((1, w, ROW_TILE), lambda t: (bidx(t), 0, pidx(t))) for w in widths]
    q_shapes = [jax.ShapeDtypeStruct((B, 256, T), BF16) for _ in range(4)]
    k_shapes = [jax.ShapeDtypeStruct((B, T, w), BF16) for w in widths]
    v_shapes = [jax.ShapeDtypeStruct((B, w, T), BF16) for w in widths]
    outs = pl.pallas_call(
        functools.partial(_in_kernel, tn=512, n_lat_tiles=nlat),
        grid=(nt,),
        in_specs=x_specs + [
            pl.BlockSpec((1, D_MODEL), lambda t: (0, 0)),
            pl.BlockSpec((1, 1, N_MOD * D_MODEL), lambda t: (row_batch(t), 0, 0)),
            pl.BlockSpec((D_MODEL, QKV_TOT + GATE_TOT), lambda t: (0, 0), pipeline_mode=pl.Buffered(1)),
            pl.BlockSpec((1, GATE_TOT), lambda t: (0, 0)),
            tab_spec, tab_spec, tab_spec, tab_spec, const_spec, const_spec,
            pl.BlockSpec((8, LANES), lambda t: (0, 0)),
        ],
        out_specs=[pl.BlockSpec((ROW_TILE, GATE_TOT), lambda t: (t, 0))] + q_specs + k_specs + v_specs,
        out_shape=[jax.ShapeDtypeStruct((R, GATE_TOT), BF16)] + q_shapes + k_shapes + v_shapes,
        compiler_params=_cparams(("arbitrary",)),
        name="in_proj",
    )(*x_args, norm_g.reshape(1, D_MODEL), mod, w_in_bf, b_gate.reshape(1, GATE_TOT), *tables, ones64, ones32,
      gains)
    return outs[0], outs[1:5], outs[5:9], outs[9:13]


def _pad_queries(q, pieces, kw, unit):
    tq = q.shape[1]
    r = lax.broadcasted_iota(jnp.int32, (kw, tq), 0)
    blocks = []
    for row0, size, extra in pieces:
        tiled = jnp.concatenate([q[row0:row0 + size, :]] * (kw // size), axis=0)
        off = unit * HEAD_DIM + extra
        blocks.append(jnp.where((r >= off) & (r < off + size), tiled, 0.0))
    out = blocks[0] if len(blocks) == 1 else jnp.concatenate(blocks, axis=1)
    return out.astype(BF16)


_PIECES = {
    "gqa": ((0, 64, 0), (64, 64, 0)),
    "diff": ((0, 32, 0), (32, 32, 32)),
    "mha": ((0, 64, 0),),
}


def _flash_kernel(*refs, n_tiles, n_in, tq, **static):
    if n_tiles == 1:
        _flash_tile(*refs, tq=tq, **static)
        return

    def tile(i, carry):
        cols = pl.ds(pl.multiple_of(i * tq, LANES), tq)
        views = list(refs)
        views[0] = refs[0].at[:, :, cols]
        views[n_in] = refs[n_in].at[:, :, cols]
        _flash_tile(*views, tq=tq, **static)
        return carry

    lax.fori_loop(0, n_tiles, tile, 0)


def _flash_tile(*refs, mode, kw, tq, max_keys, chunks, use_sink, lam_init, aliased):
    refs = list(refs)
    qT_ref, k_ref, vT_ref = refs[0:3]
    pos = 3
    sink_ref = None
    if use_sink:
        sink_ref = refs[pos]
        pos += 1
    if mode == "diff":
        lamvec_ref, subg_ref = refs[pos:pos + 2]
        pos += 2
    if aliased:
        pos += 1
    o_ref = refs[pos]
    s_bufs = refs[pos + 1:pos + 4]

    unit = pl.program_id(1)
    pieces = _PIECES[mode]
    ng = len(pieces)
    n = ng * tq
    qpad = _pad_queries(qT_ref[0].astype(F32), pieces, kw, unit)

    if use_sink:
        m0 = jnp.concatenate([jnp.full((1, tq), sink_ref[ng * unit + g] * LOG2E, F32) for g in range(ng)],
                             axis=1)
        l0 = jnp.ones((1, n), F32)
    else:
        m0 = jnp.full((1, n), NEG_INF, F32)
        l0 = jnp.zeros((1, n), F32)
    acc0 = jnp.zeros((HEAD_DIM, n), F32)
    ones_rows = jnp.ones((SUM_ROWS, max_keys), BF16)

    def scores(start, size, s_ref):
        s = _dot(k_ref[0, pl.ds(start, size), :], qpad)
        s_ref[0:size, :] = s
        return jnp.max(s, axis=0, keepdims=True)

    def absorb(start, size, s_ref, mc, state):
        m, l, acc = state
        m_new = jnp.maximum(m, mc)
        alpha = jnp.exp2(m - m_new)
        p = jnp.exp2(s_ref[0:size, :] - m_new)
        v_aug = jnp.concatenate([vT_ref[0, :, pl.ds(start, size)], ones_rows[:, 0:size]], axis=0)
        r = _dot(v_aug, p.astype(BF16))
        return m_new, alpha * l + r[HEAD_DIM:HEAD_DIM + 1], alpha * acc + r[0:HEAD_DIM]

    state = (m0, l0, acc0)
    first, main, n_main, tail = chunks

    def main_start(i):
        return pl.multiple_of(main[0] + i * main[1], LANES)

    mc = scores(first[0], first[1], s_bufs[0])
    if n_main == 0:
        state = absorb(first[0], first[1], s_bufs[0], mc, state)
    else:
        mc_next = scores(main_start(0), main[1], s_bufs[1])
        state = absorb(first[0], first[1], s_bufs[0], mc, state)
        mc = mc_next

        def triple(i, carry):
            mc, state = carry[0], carry[1:]
            k = 3 * i
            mc2 = scores(main_start(k + 1), main[1], s_bufs[2])
            state = absorb(main_start(k), main[1], s_bufs[1], mc, state)
            mc0 = scores(main_start(k + 2), main[1], s_bufs[0])
            state = absorb(main_start(k + 1), main[1], s_bufs[2], mc2, state)
            mc1 = scores(main_start(k + 3), main[1], s_bufs[1])
            state = absorb(main_start(k + 2), main[1], s_bufs[0], mc0, state)
            return (mc1,) + tuple(state)

        n_loop = (n_main - 1) // 3
        if n_loop > 0:
            carry = lax.fori_loop(0, n_loop, triple, (mc,) + tuple(state))
            mc, state = carry[0], carry[1:]
        last = main_start(n_main - 1)
        if tail[1] > 0:
            mc_next = scores(tail[0], tail[1], s_bufs[2])
            state = absorb(last, main[1], s_bufs[1], mc, state)
            state = absorb(tail[0], tail[1], s_bufs[2], mc_next, state)
        else:
            state = absorb(last, main[1], s_bufs[1], mc, state)
    m, l, acc = state
    o = acc / l

    if mode == "gqa":
        for g in range(ng):
            o_ref[0, g * HEAD_DIM:(g + 1) * HEAD_DIM, :] = o[:, g * tq:(g + 1) * tq].astype(o_ref.dtype)
    elif mode == "mha":
        o_ref[0] = o.astype(o_ref.dtype)
    else:
        lv = lamvec_ref[...]
        lam = (jnp.exp(jnp.sum(lv[0:1] * lv[1:2], axis=1, keepdims=True))
               - jnp.exp(jnp.sum(lv[2:3] * lv[3:4], axis=1, keepdims=True)) + lam_init)
        d = o[:, 0:tq] - lam * o[:, tq:2 * tq]
        ms = jnp.mean(d * d, axis=0, keepdims=True)
        o_ref[0] = (d * lax.rsqrt(ms + EPS) * subg_ref[...] * (1.0 - lam_init)).astype(o_ref.dtype)


def _flash(qT, k, vT, *, mode, S, C, ctx_only, sink=None, lamvec=None, subg=None, lam_init=0.0, out_prev=None):
    B, _, T = qT.shape
    kw = k.shape[2]
    q_rows = 128 if mode == "gqa" else 64
    units = 256 // q_rows
    if ctx_only:
        tq, n_tiles, q_blk0 = C, 1, S // C
        chunks = ((S, C), (0, 0), 0, (0, 0))
    else:
        tq = FLASH_QUERIES if S % FLASH_QUERIES == 0 else 256
        n_tiles, q_blk0 = S // tq, 0
        tk = FLASH_KEYS if S >= 4 * FLASH_KEYS else 256
        n_main = (S // tk - 1) // 3 * 3 + 1
        chunks = ((S, C), (0, tk), n_main, (n_main * tk, S - n_main * tk))
    max_keys = max(c[1] for c in (chunks[0], chunks[1], chunks[3]))
    in_specs = [
        pl.BlockSpec((1, q_rows, tq * n_tiles), lambda b, u: (b, u, q_blk0)),
        pl.BlockSpec((1, T, kw), lambda b, u: (b, 0, 0)),
        pl.BlockSpec((1, HEAD_DIM, T), lambda b, u: (b, u, 0)),
    ]
    args = [qT, k, vT]
    if sink is not None:
        in_specs.append(pl.BlockSpec(memory_space=pltpu.SMEM))
        args.append(sink)
    if mode == "diff":
        in_specs += [pl.BlockSpec((8, LANES), lambda b, u: (0, 0)),
                     pl.BlockSpec((HEAD_DIM, 1), lambda b, u: (0, 0))]
        args += [lamvec, subg]
    aliases = {}
    if out_prev is not None:
        in_specs.append(pl.BlockSpec(memory_space=pl.ANY))
        aliases = {len(args): 0}
        args.append(out_prev)
    kern = functools.partial(_flash_kernel, n_tiles=n_tiles, n_in=len(args), mode=mode, kw=kw, tq=tq,
                             max_keys=max_keys, chunks=chunks, use_sink=sink is not None, lam_init=lam_init,
                             aliased=out_prev is not None)
    return pl.pallas_call(
        kern,
        grid=(B, units),
        in_specs=in_specs,
        out_specs=pl.BlockSpec((1, q_rows, tq * n_tiles), lambda b, u: (b, u, q_blk0)),
        out_shape=jax.ShapeDtypeStruct((B, 256, T), BF16),
        input_output_aliases=aliases,
        scratch_shapes=[pltpu.VMEM((max_keys, len(_PIECES[mode]) * tq), F32)] * 3,
        compiler_params=_cparams(("arbitrary", "arbitrary")),
        name="flash_" + mode + ("_ctx" if ctx_only else ""),
    )(*args)


WINDOW_BLOCKS = 2


def _window_kernel(qT_ref, k_ref, vT_ref, sink_ref, o_ref, *, S, C):
    step = pl.program_id(1)
    qb = WINDOW
    span = 3 * qb
    n = 2 * qb
    k_ctx = k_ref[0, S:S + C, :]
    for bi in range(WINDOW_BLOCKS):
        i = step * WINDOW_BLOCKS + bi
        cols = slice(bi * qb, (bi + 1) * qb)
        start = pl.multiple_of(jnp.clip((i - 1) * qb, 0, S - span), LANES)
        k_loc = k_ref[0, pl.ds(start, span), :]
        kpos = start + lax.broadcasted_iota(jnp.int32, (span, n), 0)
        qpos = i * qb + lax.broadcasted_iota(jnp.int32, (span, n), 1) % qb
        in_window = jnp.abs(kpos - qpos) <= WINDOW
        for unit in range(2):
            q = qT_ref[0, unit * 128:(unit + 1) * 128, cols].astype(F32)
            qpad = _pad_queries(q, _PIECES["gqa"], LANES, unit)
            s_loc = jnp.where(in_window, _dot(k_loc, qpad), NEG_INF)
            s_ctx = _dot(k_ctx, qpad)
            sink = jnp.concatenate(
                [jnp.full((1, qb), sink_ref[2 * unit + g] * LOG2E, F32) for g in range(2)], axis=1)
            m = jnp.maximum(jnp.maximum(jnp.max(s_loc, axis=0, keepdims=True),
                                        jnp.max(s_ctx, axis=0, keepdims=True)), sink)
            e_loc = jnp.exp2(s_loc - m)
            e_ctx = jnp.exp2(s_ctx - m)
            den = (jnp.sum(e_loc, axis=0, keepdims=True) + jnp.sum(e_ctx, axis=0, keepdims=True)
                   + jnp.exp2(sink - m))
            v_rows = slice(unit * HEAD_DIM, (unit + 1) * HEAD_DIM)
            o = (_dot(vT_ref[0, v_rows, pl.ds(start, span)], e_loc.astype(BF16))
                 + _dot(vT_ref[0, v_rows, S:S + C], e_ctx.astype(BF16))) / den
            for g in range(2):
                head = 2 * unit + g
                o_ref[0, head * HEAD_DIM:(head + 1) * HEAD_DIM, cols] = (
                    o[:, g * qb:(g + 1) * qb].astype(o_ref.dtype))


def _window(qT, k, vT, sink, S, C):
    B, _, T = qT.shape
    tq = WINDOW * WINDOW_BLOCKS
    return pl.pallas_call(
        functools.partial(_window_kernel, S=S, C=C),
        grid=(B, S // tq),
        in_specs=[
            pl.BlockSpec((1, 256, tq), lambda b, i: (b, 0, i)),
            pl.BlockSpec((1, T, LANES), lambda b, i: (b, 0, 0)),
            pl.BlockSpec((1, 2 * HEAD_DIM, T), lambda b, i: (b, 0, 0)),
            pl.BlockSpec(memory_space=pltpu.SMEM),
        ],
        out_specs=pl.BlockSpec((1, 256, tq), lambda b, i: (b, 0, i)),
        out_shape=jax.ShapeDtypeStruct((B, 256, T), BF16),
        compiler_params=_cparams(("arbitrary", "arbitrary")),
        name="window_attn",
    )(qT, k, vT, sink)


def _nbr_kernel(qT_ref, k_ref, vT_ref, tab_ref, o_ref, *, S, C):
    j = pl.program_id(1)
    rows = S // GRID_W
    n = NBR_Q_ROWS * GRID_W
    span = NBR_WIN_ROWS * GRID_W
    w0 = jnp.clip(NBR_Q_ROWS * j - NA_ROWS // 2, 0, rows - NBR_WIN_ROWS)
    shift = w0 - NBR_Q_ROWS * j + NA_ROWS // 2 + 4
    start = pl.multiple_of(w0 * GRID_W, LANES)
    k_loc = k_ref[0, pl.ds(start, span), :]
    k_ctx = k_ref[0, S:S + C, :]
    kr = w0 + lax.broadcasted_iota(jnp.int32, (span, n), 0) // GRID_W
    qr = NBR_Q_ROWS * j + lax.broadcasted_iota(jnp.int32, (span, n), 1) // GRID_W
    r0 = jnp.clip(qr - NA_ROWS // 2, 0, rows - NA_ROWS)
    in_rows = (kr >= r0) & (kr < r0 + NA_ROWS)
    for head in range(4):
        h_rows = slice(head * HEAD_DIM, (head + 1) * HEAD_DIM)
        qpad = _pad_queries(qT_ref[0, h_rows, :].astype(F32), _PIECES["mha"], 256, head)
        bias = tab_ref[head, pl.ds(pl.multiple_of(shift * GRID_W, GRID_W), span), :]
        s_loc = jnp.where(in_rows, _dot(k_loc, qpad) + bias, NEG_INF)
        s_ctx = _dot(k_ctx, qpad)
        m = jnp.maximum(jnp.max(s_loc, axis=0, keepdims=True), jnp.max(s_ctx, axis=0, keepdims=True))
        e_loc = jnp.exp2(s_loc - m)
        e_ctx = jnp.exp2(s_ctx - m)
        den = jnp.sum(e_loc, axis=0, keepdims=True) + jnp.sum(e_ctx, axis=0, keepdims=True)
        o = (_dot(vT_ref[0, h_rows, pl.ds(start, span)], e_loc.astype(BF16))
             + _dot(vT_ref[0, h_rows, S:S + C], e_ctx.astype(BF16))) / den
        o_ref[0, h_rows, :] = o.astype(o_ref.dtype)


def _nbr_bias_table(rpb):
    u = np.arange(NBR_TABLE_ROWS)[:, None, None, None]
    kc = np.arange(GRID_W)[None, :, None, None]
    e = np.arange(NBR_Q_ROWS)[None, None, :, None]
    qc = np.arange(GRID_W)[None, None, None, :]
    dr = u - e - 1
    row_ok = (dr >= 0) & (dr < 2 * NA_ROWS - 1)
    dc = np.clip(kc - qc, -(NA_COLS - 1), NA_COLS - 1) + (NA_COLS - 1)
    c0 = np.clip(qc - NA_COLS // 2, 0, GRID_W - NA_COLS)
    col_ok = (kc >= c0) & (kc < c0 + NA_COLS)
    shape = (NBR_TABLE_ROWS, GRID_W, NBR_Q_ROWS, GRID_W)
    n_dr, n_dc = 2 * NA_ROWS - 1, 2 * NA_COLS - 1
    sel_r = ((dr[:, 0, :, 0, None] == np.arange(n_dr)) & row_ok[:, 0, :, 0, None]).astype(np.float32)
    sel_c = (dc[0, :, 0, :, None] == np.arange(n_dc)).astype(np.float32)
    vals = jnp.einsum("uer,hrc,kqc->hukeq", sel_r, rpb.astype(F32) * LOG2E, sel_c,
                      precision=lax.Precision.HIGHEST)
    vals = jnp.where(np.broadcast_to(col_ok, shape), vals, NEG_INF)
    return vals.reshape(rpb.shape[0], NBR_TABLE_ROWS * GRID_W, NBR_Q_ROWS * GRID_W)


def _nbr(qT, k, vT, table, S, C):
    B, _, T = qT.shape
    tq = NBR_Q_ROWS * GRID_W
    return pl.pallas_call(
        functools.partial(_nbr_kernel, S=S, C=C),
        grid=(B, S // tq),
        in_specs=[
            pl.BlockSpec((1, 256, tq), lambda b, j: (b, 0, j)),
            pl.BlockSpec((1, T, 256), lambda b, j: (b, 0, 0)),
            pl.BlockSpec((1, 256, T), lambda b, j: (b, 0, 0)),
            pl.BlockSpec((4, NBR_TABLE_ROWS * GRID_W, tq), lambda b, j: (0, 0, 0)),
        ],
        out_specs=pl.BlockSpec((1, 256, tq), lambda b, j: (b, 0, j)),
        out_shape=jax.ShapeDtypeStruct((B, 256, T), BF16),
        compiler_params=_cparams(("arbitrary", "arbitrary")),
        name="nbr_attn",
    )(qT, k, vT, table)


ROUTE_W, ROUTE_E, ROUTE_RANK = 0, 4, 8
ROUTE_ROWS = 16
MOE_GROUPS = 1


def _merge_kernel(oa_ref, ob_ref, oc_ref, od_ref, gate_ref, xa_ref, xb_ref, mod_ref, wb_ref, wo_ref, n2_ref,
                  rw_ref, rb_ref, tri_ref, xo_ref, h_ref, route_ref, cnt_ref, base_ref, *, group_tiles,
                  n_lat_tiles):
    @pl.when(pl.program_id(0) % group_tiles == 0)
    def _():
        base_ref[...] = jnp.zeros_like(base_ref)

    acc = None
    for nbr, o_ref in enumerate((oa_ref, ob_ref, oc_ref, od_ref)):
        proj = lax.dot_general(o_ref[0], wb_ref[nbr], (((0,), (0,)), ((), ())),
                               preferred_element_type=F32)
        term = gate_ref[:, nbr * D_MODEL:(nbr + 1) * D_MODEL].astype(F32) * proj
        acc = term if acc is None else acc + term
    mix = _dot(acc.astype(BF16), wo_ref[...])
    mod = mod_ref[0]
    g1 = mod[:, 2 * D_MODEL:3 * D_MODEL]
    sh2 = mod[:, 3 * D_MODEL:4 * D_MODEL]
    sc2 = mod[:, 4 * D_MODEL:5 * D_MODEL]
    xn = jnp.where(pl.program_id(0) < n_lat_tiles, xa_ref[...], xb_ref[...]) + g1 * mix
    xo_ref[...] = xn
    ms = jnp.mean(xn * xn, axis=-1, keepdims=True)
    h = (xn * lax.rsqrt(ms + EPS) * n2_ref[...]) * (1 + sc2) + sh2
    h_ref[...] = h
    logits = lax.dot_general(rw_ref[...], h.astype(BF16), (((1,), (1,)), ((), ())),
                             preferred_element_type=F32) + rb_ref[...]
    row_f = lax.broadcasted_iota(jnp.int32, logits.shape, 0).astype(F32)
    work = logits
    picks = []
    for _ in range(TOP_K):
        top = jnp.max(work, axis=0, keepdims=True)
        idx = jnp.min(jnp.where(work == top, row_f, float(N_EXPERTS)), axis=0, keepdims=True)
        hit = row_f == idx
        picks.append((top, idx, hit))
        work = jnp.where(hit, -jnp.inf, work)
    ex = [jnp.exp(top - picks[0][0]) for top, _, _ in picks]
    den = ex[0] + ex[1] + ex[2] + ex[3]
    chosen = jnp.zeros(logits.shape, F32)
    for _, _, hit in picks:
        chosen = chosen + hit.astype(F32)
    base = base_ref[...][:, 0:1]
    before = _dot(chosen.astype(BF16), tri_ref[...]) + base
    fields = ([ex[k] / den for k in range(TOP_K)] + [idx for _, idx, _ in picks]
              + [jnp.sum(jnp.where(hit, before, 0.0), axis=0, keepdims=True) for _, _, hit in picks])
    fields.append(jnp.zeros((ROUTE_ROWS - len(fields), logits.shape[1]), F32))
    route_ref[...] = jnp.concatenate(fields, axis=0)
    base_ref[...] = base_ref[...] + jnp.sum(chosen, axis=1, keepdims=True)
    cnt_ref[0] = base_ref[...]


def _merge(oTs, gate, X, mod, wb_bf, wo_bf, norm2_g, rw_t, rb_col, n_rows, B, S, C, row_batch):
    nt = n_rows // ROW_TILE
    per = S // ROW_TILE
    nlat = B * per

    def bidx(t):
        return jnp.where(t < nlat, t // per, t - nlat)

    def pidx(t):
        return jnp.where(t < nlat, t % per, per)

    o_spec = pl.BlockSpec((1, 256, ROW_TILE), lambda t: (bidx(t), 0, pidx(t)))
    x_specs, x_args = _stream_specs(X, nlat)
    r = np.arange(ROW_TILE)
    strict_upper = jnp.asarray((r[:, None] < r[None, :]).astype(np.float32), dtype=BF16)
    group_tiles = nt // MOE_GROUPS
    assert nt % MOE_GROUPS == 0
    return pl.pallas_call(
        functools.partial(_merge_kernel, group_tiles=group_tiles, n_lat_tiles=nlat),
        grid=(nt,),
        in_specs=[o_spec, o_spec, o_spec, o_spec,
                  pl.BlockSpec((ROW_TILE, GATE_TOT), lambda t: (t, 0))] + x_specs + [
                  pl.BlockSpec((1, 1, N_MOD * D_MODEL), lambda t: (row_batch(t), 0, 0)),
                  pl.BlockSpec((4, 256, D_MODEL), lambda t: (0, 0, 0)),
                  pl.BlockSpec((D_MODEL, D_MODEL), lambda t: (0, 0)),
                  pl.BlockSpec((1, D_MODEL), lambda t: (0, 0)),
                  pl.BlockSpec((N_EXPERTS, D_MODEL), lambda t: (0, 0)),
                  pl.BlockSpec((N_EXPERTS, 1), lambda t: (0, 0)),
                  pl.BlockSpec((ROW_TILE, ROW_TILE), lambda t: (0, 0))],
        out_specs=[pl.BlockSpec((ROW_TILE, D_MODEL), lambda t: (t, 0)),
                   pl.BlockSpec((ROW_TILE, D_MODEL), lambda t: (t, 0)),
                   pl.BlockSpec((ROUTE_ROWS, ROW_TILE), lambda t: (0, t)),
                   pl.BlockSpec((1, N_EXPERTS, LANES), lambda t: (t // group_tiles, 0, 0))],
        out_shape=[jax.ShapeDtypeStruct((n_rows, D_MODEL), F32),
                   jax.ShapeDtypeStruct((n_rows, D_MODEL), F32),
                   jax.ShapeDtypeStruct((ROUTE_ROWS, n_rows), F32),
                   jax.ShapeDtypeStruct((MOE_GROUPS, N_EXPERTS, LANES), F32)],
        scratch_shapes=[pltpu.VMEM((N_EXPERTS, LANES), F32)],
        compiler_params=_cparams(("arbitrary",)),
        name="merge",
    )(*oTs, gate, *x_args, mod, wb_bf, wo_bf, norm2_g.reshape(1, D_MODEL), rw_t, rb_col, strict_upper)


def _expert_kernel(tok_ref, off_ref, first_ref, nblk_ref, nu_ref, h_hbm, wgu_ref, bgu_ref, wd_ref, bd_ref,
                   y_hbm, x_0, x_1, x_2, y_0, y_1, y_2, gsems, ysems, wgu_s, wd_s):
    e = pl.program_id(0)
    n_used = nu_ref[0]
    x_bufs = (x_0, x_1, x_2)
    y_bufs = (y_0, y_1, y_2)

    def start_gather(block, x_dst, sem):
        base = off_ref[block]
        for r in range(EXPERT_ROWS):
            pltpu.make_async_copy(h_hbm.at[pl.ds(tok_ref[base + r], 1), :], x_dst.at[pl.ds(r, 1), :],
                                  sem).start()

    def wait_gather(x_dst, sem):
        pltpu.make_async_copy(h_hbm.at[pl.ds(0, EXPERT_ROWS), :], x_dst, sem).wait()

    def y_copy(g, cur):
        rows = pl.ds(pl.multiple_of(g * EXPERT_ROWS, EXPERT_ROWS), EXPERT_ROWS)
        return pltpu.make_async_copy(y_bufs[cur], y_hbm.at[rows, :], ysems.at[cur])

    def block(cur, g):
        ahead = (cur + 2) % 3

        @pl.when(g >= 3)
        def _():
            y_copy(g - 3, cur).wait()

        wait_gather(x_bufs[cur], gsems.at[cur])
        start_gather(jnp.minimum(g + 2, n_used - 1), x_bufs[ahead], gsems.at[ahead])
        x = x_bufs[cur][...].astype(BF16)
        gu = _dot(x, wgu_s[...]) + bgu_ref[0]
        gate = jnp.minimum(gu[:, :D_FF], SWIGLU_LIMIT)
        up = jnp.clip(gu[:, D_FF:], -SWIGLU_LIMIT, SWIGLU_LIMIT)
        a = gate * jax.nn.sigmoid(SWIGLU_ALPHA * gate) * (up + 1)
        y_bufs[cur][...] = (_dot(a.astype(BF16), wd_s[...]) + bd_ref[0]).astype(BF16)
        y_copy(g, cur).start()

    @pl.when(e == 0)
    def _():
        start_gather(0, x_bufs[0], gsems.at[0])
        start_gather(jnp.minimum(1, n_used - 1), x_bufs[1], gsems.at[1])

    n_blocks = nblk_ref[e]

    @pl.when(n_blocks > 0)
    def _():
        wgu_s[...] = wgu_ref[0].astype(BF16)
        wd_s[...] = wd_ref[0].astype(BF16)

    def body(j, carry):
        g = first_ref[e] + j
        for cur in range(3):
            pl.when(g % 3 == cur)(functools.partial(block, cur, g))
        return carry

    lax.fori_loop(0, n_blocks, body, 0)

    @pl.when(e == pl.num_programs(0) - 1)
    def _():
        last = (n_used - 1) % 3
        for b in range(3):
            @pl.when(b != last)
            def _():
                wait_gather(x_bufs[b], gsems.at[b])

            @pl.when(n_used > b)
            def _():
                y_copy(0, b).wait()


def _experts(h, tok_sorted, blk_off, e_first, e_nblk, n_used, n_blk, layer, w_gate_up, b_gate_up, w_down,
             b_down):
    E = w_gate_up.shape[0]

    def w_map(e, *_):
        return (e + layer * N_EXPERTS, 0, 0)

    grid_spec = pltpu.PrefetchScalarGridSpec(
        num_scalar_prefetch=5,
        grid=(N_EXPERTS,),
        in_specs=[
            pl.BlockSpec(memory_space=pl.ANY),
            pl.BlockSpec((1, D_MODEL, 2 * D_FF), w_map),
            pl.BlockSpec((1, 1, 2 * D_FF), w_map),
            pl.BlockSpec((1, D_FF, D_MODEL), w_map),
            pl.BlockSpec((1, 1, D_MODEL), w_map),
        ],
        out_specs=pl.BlockSpec(memory_space=pl.ANY),
        scratch_shapes=([pltpu.VMEM((EXPERT_ROWS, D_MODEL), F32)] * 3
                        + [pltpu.VMEM((EXPERT_ROWS, D_MODEL), BF16)] * 3
                        + [pltpu.SemaphoreType.DMA((3,)), pltpu.SemaphoreType.DMA((3,)),
                           pltpu.VMEM((D_MODEL, 2 * D_FF), BF16), pltpu.VMEM((D_FF, D_MODEL), BF16)]),
    )
    return pl.pallas_call(
        _expert_kernel,
        grid_spec=grid_spec,
        out_shape=jax.ShapeDtypeStruct((n_blk * EXPERT_ROWS, D_MODEL), BF16),
        compiler_params=_cparams(("arbitrary",)),
        name="experts",
    )(tok_sorted, blk_off, e_first, e_nblk, n_used, h, w_gate_up, b_gate_up.reshape(E, 1, 2 * D_FF), w_down,
      b_down.reshape(E, 1, D_MODEL))


def _combine_kernel(yg_ref, route_ref, x_ref, mod_ref, *rest):
    o_ref = rest[-1]
    route = route_ref[...]
    y = None
    for k in range(TOP_K):
        term = route[:, ROUTE_W + k:ROUTE_W + k + 1] * yg_ref[k].astype(F32)
        y = term if y is None else y + term
    g2 = mod_ref[0][:, 5 * D_MODEL:6 * D_MODEL]
    o_ref[...] = x_ref[...] + g2 * y


def _combine(yg, route, X, mod, row_batch, tile0, out_prev):
    n_rows = X.shape[0]
    in_specs = [pl.BlockSpec((TOP_K, ROW_TILE, D_MODEL), lambda t: (0, t, 0)),
                pl.BlockSpec((ROW_TILE, ROUTE_ROWS), lambda t: (t + tile0, 0)),
                pl.BlockSpec((ROW_TILE, D_MODEL), lambda t: (t + tile0, 0)),
                pl.BlockSpec((1, 1, N_MOD * D_MODEL), lambda t: (row_batch(t + tile0), 0, 0))]
    args = [yg, route, X, mod]
    aliases = {}
    if out_prev is not None:
        in_specs.append(pl.BlockSpec(memory_space=pl.ANY))
        aliases = {len(args): 0}
        args.append(out_prev)
    return pl.pallas_call(
        _combine_kernel,
        grid=(yg.shape[1] // ROW_TILE,),
        in_specs=in_specs,
        out_specs=pl.BlockSpec((ROW_TILE, D_MODEL), lambda t: (t + tile0, 0)),
        out_shape=jax.ShapeDtypeStruct((n_rows, D_MODEL), F32),
        input_output_aliases=aliases,
        compiler_params=_cparams(("arbitrary",)),
        name="moe_combine",
    )(*args)


def _moe(h, route, counts_f, X, mod, row_batch, layer, w_gate_up, b_gate_up, w_down, b_down):
    n_groups = counts_f.shape[0]
    Tg = h.shape[0] // n_groups
    n = Tg * TOP_K
    n_blk = n // EXPERT_ROWS + N_EXPERTS
    experts = jnp.arange(N_EXPERTS, dtype=jnp.int32)
    out = None
    for g in range(n_groups):
        route_g = lax.slice_in_dim(route, g * Tg, (g + 1) * Tg, axis=0)
        top_e = route_g[:, ROUTE_E:ROUTE_E + TOP_K].astype(jnp.int32)
        rank = route_g[:, ROUTE_RANK:ROUTE_RANK + TOP_K].astype(jnp.int32)
        counts = counts_f[g, :, 0].astype(jnp.int32)
        padded = (counts + EXPERT_ROWS - 1) // EXPERT_ROWS * EXPERT_ROWS
        pad_end = jnp.cumsum(padded)
        pad_start = pad_end - padded
        start_of = jnp.sum(jnp.where(top_e[:, :, None] == experts, pad_start, 0), axis=-1)
        dest = start_of + rank
        tok = jnp.broadcast_to(jnp.arange(g * Tg, (g + 1) * Tg, dtype=jnp.int32)[:, None], (Tg, TOP_K))
        _, tok_sorted = lax.sort_key_val(dest.reshape(-1), tok.reshape(-1))
        tok_sorted = jnp.concatenate([tok_sorted, jnp.zeros((EXPERT_ROWS,), jnp.int32)])
        n_used = (pad_end[-1] // EXPERT_ROWS).astype(jnp.int32)
        blk_row = jnp.minimum(jnp.arange(n_blk, dtype=jnp.int32), n_used - 1) * EXPERT_ROWS
        blk_e = jnp.sum((pad_end[None, :] <= blk_row[:, None]).astype(jnp.int32), axis=1)
        blk_e = jnp.minimum(blk_e, N_EXPERTS - 1)
        pad_before = pad_start - (jnp.cumsum(counts) - counts)
        blk_off = blk_row - jnp.sum(jnp.where(blk_e[:, None] == experts, pad_before, 0), axis=-1)
        y = _experts(h, tok_sorted, blk_off, pad_start // EXPERT_ROWS, padded // EXPERT_ROWS,
                     n_used.reshape(1), n_blk, layer, w_gate_up, b_gate_up, w_down, b_down)
        yg = y.at[dest.T.reshape(-1)].get(mode="promise_in_bounds").reshape(TOP_K, Tg, D_MODEL)
        out = _combine(yg, route, X, mod, row_batch, g * (Tg // ROW_TILE), out)
    return out


def _rope_tables(S, dim):
    t = jnp.arange(S, dtype=jnp.int32)
    row = (t // GRID_W).astype(F32)
    col = (t % GRID_W).astype(F32)
    n_freq = dim // 4
    inv = ROPE_THETA ** (-jnp.arange(n_freq, dtype=F32) / n_freq)
    ang = jnp.concatenate([row[:, None] * inv, col[:, None] * inv], axis=-1)
    cos, sin = jnp.cos(ang), jnp.sin(ang)
    reps = LANES // dim
    cos_t = jnp.tile(jnp.concatenate([cos, cos], axis=-1), (1, reps))
    sin_t = jnp.tile(jnp.concatenate([-sin, sin], axis=-1), (1, reps))
    cos_t = jnp.concatenate([cos_t, jnp.ones((ROW_TILE, LANES), F32)], axis=0)
    sin_t = jnp.concatenate([sin_t, jnp.zeros((ROW_TILE, LANES), F32)], axis=0)
    return cos_t, sin_t


def _block_ones(seg):
    i = np.arange(LANES)
    return jnp.asarray((i[:, None] // seg == i[None, :] // seg).astype(np.float32), dtype=BF16)


def _lane_tile(v):
    return jnp.tile(v.astype(F32), LANES // v.shape[0])


def kernel(x, c, ctx, c_ctx, norm1_g, norm2_g, w_ada, b_ada, w_in, b_gate, a_qn, a_kn, b_qn, b_kn, lam_q1, lam_k1, lam_q2, lam_k2, subln_g, c_qn, c_kn, sink, d_qn, d_kn, rpb, w_branch, w_out, router_w, router_b, w_gate_up, b_gate_up, w_down, b_down):
    B, S, D = x.shape
    C = ctx.shape[1]
    L = w_in.shape[0]
    assert D == D_MODEL and C == ROW_TILE and S % ROW_TILE == 0 and B + 1 <= 8
    n_lat = B * S
    per = S // ROW_TILE
    nlat_tiles = B * per

    def row_batch(t):
        return jnp.where(t < nlat_tiles, t // per, B)

    cvec = jnp.zeros((8, D), F32).at[:B].set(c).at[B].set(c_ctx)
    mod_all = _ada(cvec, w_ada, b_ada)

    tables = _rope_tables(S, HEAD_DIM) + _rope_tables(S, B_DK)
    ones64, ones32 = _block_ones(HEAD_DIM), _block_ones(B_DK)

    E = w_gate_up.shape[1]
    w_gu_all = w_gate_up.reshape(L * E, D, 2 * D_FF)
    w_dn_all = w_down.reshape(L * E, D_FF, D)

    X = (x.reshape(n_lat, D), ctx.reshape(B * C, D))
    for l in range(L):
        last = l == L - 1
        lam_init = 0.8 - 0.6 * math.exp(-0.3 * l)
        mod = mod_all[l].reshape(8, 1, N_MOD * D)
        gains = jnp.stack([_lane_tile(g[l]) for g in (a_qn, a_kn, b_qn, b_kn, c_qn, c_kn, d_qn, d_kn)])
        gate, qTs, ks, vTs = _in_proj(X, norm1_g[l], mod, w_in[l].astype(BF16), b_gate[l], row_batch,
                                      tables, ones64, ones32, gains, B, S, C)

        lamvec = jnp.zeros((8, LANES), F32)
        for r, v in enumerate((lam_q1, lam_k1, lam_q2, lam_k2)):
            lamvec = lamvec.at[r, :B_DK].set(v[l])
        subg = subln_g[l].reshape(HEAD_DIM, 1)
        sink_l = sink[l].astype(F32)
        nbr_table = _nbr_bias_table(rpb[l])

        o_a = _flash(qTs[0], ks[0], vTs[0], mode="gqa", S=S, C=C, ctx_only=False)
        o_b = _flash(qTs[1], ks[1], vTs[1], mode="diff", S=S, C=C, ctx_only=False,
                     lamvec=lamvec, subg=subg, lam_init=lam_init)
        o_c = _window(qTs[2], ks[2], vTs[2], sink_l, S, C)
        o_d = _nbr(qTs[3], ks[3], vTs[3], nbr_table, S, C)
        if not last:
            o_a = _flash(qTs[0], ks[0], vTs[0], mode="gqa", S=S, C=C, ctx_only=True, out_prev=o_a)
            o_b = _flash(qTs[1], ks[1], vTs[1], mode="diff", S=S, C=C, ctx_only=True,
                         lamvec=lamvec, subg=subg, lam_init=lam_init, out_prev=o_b)
            o_c = _flash(qTs[2], ks[2], vTs[2], mode="gqa", S=S, C=C, ctx_only=True, sink=sink_l, out_prev=o_c)
            o_d = _flash(qTs[3], ks[3], vTs[3], mode="mha", S=S, C=C, ctx_only=True, out_prev=o_d)

        n_rows = n_lat if last else _stream_rows(X)
        Xm, h2, route_t, counts = _merge((o_a, o_b, o_c, o_d), gate, X, mod, w_branch[l].astype(BF16),
                                         w_out[l].astype(BF16), norm2_g[l], router_w[l].T.astype(BF16),
                                         router_b[l].astype(F32).reshape(N_EXPERTS, 1), n_rows, B, S, C,
                                         row_batch)
        X = _moe(h2, route_t.T, counts, Xm, mod, row_batch, l, w_gu_all, b_gate_up.reshape(L * E, -1),
                 w_dn_all, b_down.reshape(L * E, -1))
    return X[:n_lat].reshape(B, S, D)
```

```python
import functools
import math

import numpy as np
import jax
import jax.numpy as jnp
from jax import lax
from jax.experimental import pallas as pl
from jax.experimental.pallas import tpu as pltpu

F32 = jnp.float32
BF16 = jnp.bfloat16

D_MODEL = 1024
GRID_W = 64
HEAD_DIM = 64
B_DK = 32
WINDOW = 128
NA_ROWS = 8
NA_COLS = 16
ROPE_THETA = 10000.0
N_EXPERTS = 32
TOP_K = 4
D_FF = D_MODEL
SWIGLU_LIMIT = 7.0
SWIGLU_ALPHA = 1.702
N_MOD = 6
EPS = 1e-6
NEG_INF = -1e30
LOG2E = 1.4426950408889634

Q_TOT = 1024
KV_TOT = 1536
QKV_TOT = Q_TOT + KV_TOT
GATE_TOT = 4 * D_MODEL

ROW_TILE = 256
LANES = 128
EXPERT_ROWS = 256
NBR_Q_ROWS = 4
NBR_WIN_ROWS = NA_ROWS + NBR_Q_ROWS
NBR_TABLE_ROWS = NBR_WIN_ROWS + 8
FLASH_KEYS = 1152
FLASH_QUERIES = 512
SUM_ROWS = 16
VMEM_LIMIT = 52 * 1024 * 1024


def _cparams(sem):
    return pltpu.CompilerParams(dimension_semantics=sem, vmem_limit_bytes=VMEM_LIMIT)


def _dot(a, b):
    return jnp.dot(a, b, preferred_element_type=F32)


def _ada_kernel(c_ref, w_ref, b_ref, o_ref):
    c = c_ref[...]
    s = c * jax.nn.sigmoid(c)
    o_ref[0] = jnp.dot(s, w_ref[0], precision=lax.Precision.HIGHEST,
                       preferred_element_type=F32) + b_ref[0]


def _ada(cvec, w_ada, b_ada):
    L = w_ada.shape[0]
    n_out = w_ada.shape[2]
    tn = 1536
    return pl.pallas_call(
        _ada_kernel,
        grid=(L, n_out // tn),
        in_specs=[
            pl.BlockSpec((8, D_MODEL), lambda l, j: (0, 0)),
            pl.BlockSpec((1, D_MODEL, tn), lambda l, j: (l, 0, j)),
            pl.BlockSpec((1, 1, tn), lambda l, j: (l, 0, j)),
        ],
        out_specs=pl.BlockSpec((1, 8, tn), lambda l, j: (l, 0, j)),
        out_shape=jax.ShapeDtypeStruct((L, 8, n_out), F32),
        compiler_params=_cparams(("arbitrary", "arbitrary")),
        name="ada_mod",
    )(cvec, w_ada, b_ada.reshape(L, 1, n_out))


def _seg_rms(x, ones, seg, g):
    sq = x * x
    hi = sq.astype(BF16)
    lo = (sq - hi.astype(F32)).astype(BF16)
    ss = _dot(hi, ones) + _dot(lo, ones)
    return x * lax.rsqrt(ss * (1.0 / seg) + EPS) * g


def _rot_half(y, half):
    lane = lax.broadcasted_iota(jnp.int32, y.shape, 1)
    fwd = pltpu.roll(y, LANES - half, axis=1)
    bwd = pltpu.roll(y, half, axis=1)
    return jnp.where((lane % (2 * half)) < half, fwd, bwd)


def _stream_specs(X, n_lat_tiles):
    lat, ctx_rows, ctx_tile0 = (X[0], X[1], 0) if isinstance(X, tuple) else (X, X, n_lat_tiles)
    specs = [pl.BlockSpec((ROW_TILE, D_MODEL), lambda t: (jnp.minimum(t, n_lat_tiles - 1), 0)),
             pl.BlockSpec((ROW_TILE, D_MODEL), lambda t: (ctx_tile0 + jnp.maximum(t - n_lat_tiles, 0), 0))]
    return specs, [lat, ctx_rows]


def _stream_rows(X):
    return X[0].shape[0] + X[1].shape[0] if isinstance(X, tuple) else X.shape[0]


def _in_kernel(xa_ref, xb_ref, g_ref, mod_ref, w_ref, bg_ref, c64_ref, s64_ref, c32_ref, s32_ref, ones64_ref,
               ones32_ref, gains_ref, gate_ref, qa_ref, qb_ref, qc_ref, qd_ref, ka_ref, kb_ref, kc_ref, kd_ref,
               va_ref, vb_ref, vc_ref, vd_ref, *, tn, n_lat_tiles):
    x = jnp.where(pl.program_id(0) < n_lat_tiles, xa_ref[...], xb_ref[...])
    ms = jnp.mean(x * x, axis=-1, keepdims=True)
    y = x * lax.rsqrt(ms + EPS) * g_ref[...]
    mod = mod_ref[0]
    sh = mod[:, 0:D_MODEL]
    sc = mod[:, D_MODEL:2 * D_MODEL]
    h = (y * (1 + sc) + sh).astype(BF16)
    qkv = [_dot(h, w_ref[:, j * tn:(j + 1) * tn]) for j in range(QKV_TOT // tn)]

    gains = gains_ref[...]
    rope = {64: (c64_ref, s64_ref), 32: (c32_ref, s32_ref)}
    ones = {64: ones64_ref, 32: ones32_ref}

    def chunk(col):
        return qkv[col // tn][:, col % tn:col % tn + LANES]

    def normed(col, seg, gain_row, use_rope, scale):
        y = _seg_rms(chunk(col), ones[seg][...], seg, gains[gain_row:gain_row + 1, :])
        if use_rope:
            y = y * rope[seg][0][...] + _rot_half(y, seg // 2) * rope[seg][1][...]
        return y * scale if scale != 1.0 else y

    units = []
    q_refs = (qa_ref, qb_ref, qc_ref, qd_ref)
    q_seg = (64, 32, 64, 64)
    q_rope = (True, True, True, False)
    for m in range(4):
        scale = float(q_seg[m]) ** -0.5 * LOG2E
        for c in range(2):
            def q_unit(m=m, c=c, scale=scale):
                y = normed(m * 256 + c * LANES, q_seg[m], 2 * m, q_rope[m], scale)
                q_refs[m][0, c * LANES:(c + 1) * LANES, :] = y.T.astype(BF16)
            units.append(q_unit)
    k_refs = (ka_ref, kb_ref, kc_ref, kd_ref)
    v_refs = (va_ref, vb_ref, vc_ref, vd_ref)
    widths = (128, 256, 128, 256)
    kcol = Q_TOT
    vcol = Q_TOT + sum(widths)
    for m in range(4):
        for c in range(widths[m] // LANES):
            def k_unit(m=m, c=c, kcol=kcol):
                y = normed(kcol, q_seg[m], 2 * m + 1, q_rope[m], 1.0)
                k_refs[m][0, :, c * LANES:(c + 1) * LANES] = y.astype(BF16)

            def v_unit(m=m, c=c, vcol=vcol):
                v_refs[m][0, c * LANES:(c + 1) * LANES, :] = chunk(vcol).T.astype(BF16)
            units += [k_unit, v_unit]
            kcol += LANES
            vcol += LANES

    n_gate = GATE_TOT // tn
    per_gate = -(-len(units) // n_gate)
    for j in range(n_gate):
        g = _dot(h, w_ref[:, QKV_TOT + j * tn:QKV_TOT + (j + 1) * tn]) + bg_ref[:, j * tn:(j + 1) * tn]
        gate_ref[:, j * tn:(j + 1) * tn] = jax.nn.sigmoid(g).astype(BF16)
        for unit in units[j * per_gate:(j + 1) * per_gate]:
            unit()


def _in_proj(X, norm_g, mod, w_in_bf, b_gate, row_batch, tables, ones64, ones32, gains, B, S, C):
    R = _stream_rows(X)
    nt = R // ROW_TILE
    per = S // ROW_TILE
    nlat = B * per
    T = S + C
    x_specs, x_args = _stream_specs(X, nlat)

    def bidx(t):
        return jnp.where(t < nlat, t // per, t - nlat)

    def pidx(t):
        return jnp.where(t < nlat, t % per, per)

    tab_spec = pl.BlockSpec((ROW_TILE, LANES), lambda t: (pidx(t), 0))
    const_spec = pl.BlockSpec((LANES, LANES), lambda t: (0, 0))
    widths = (128, 256, 128, 256)
    q_specs = [pl.BlockSpec((1, 256, ROW_TILE), lambda t: (bidx(t), 0, pidx(t))) for _ in range(4)]
    k_specs = [pl.BlockSpec((1, ROW_TILE, w), lambda t: (bidx(t), pidx(t), 0)) for w in widths]
    v_specs = [pl.BlockSpec((1, w, ROW_TILE), lambda t: (bidx(t), 0, pidx(t))) for w in widths]
    q_shapes = [jax.ShapeDtypeStruct((B, 256, T), BF16) for _ in range(4)]
    k_shapes = [jax.ShapeDtypeStruct((B, T, w), BF16) for w in widths]
    v_shapes = [jax.ShapeDtypeStruct((B, w, T), BF16) for w in widths]
    outs = pl.pallas_call(
        functools.partial(_in_kernel, tn=512, n_lat_tiles=nlat),
        grid=(nt,),
        in_specs=x_specs + [
            pl.BlockSpec((1, D_MODEL), lambda t: (0, 0)),
            pl.BlockSpec((1, 1, N_MOD * D_MODEL), lambda t: (row_batch(t), 0, 0)),
            pl.BlockSpec((D_MODEL, QKV_TOT + GATE_TOT), lambda t: (0, 0), pipeline_mode=pl.Buffered(1)),
            pl.BlockSpec((1, GATE_TOT), lambda t: (0, 0)),
            tab_spec, tab_spec, tab_spec, tab_spec, const_spec, const_spec,
            pl.BlockSpec((8, LANES), lambda t: (0, 0)),
        ],
        out_specs=[pl.BlockSpec((ROW_TILE, GATE_TOT), lambda t: (t, 0))] + q_specs + k_specs + v_specs,
        out_shape=[jax.ShapeDtypeStruct((R, GATE_TOT), BF16)] + q_shapes + k_shapes + v_shapes,
        compiler_params=_cparams(("arbitrary",)),
        name="in_proj",
    )(*x_args, norm_g.reshape(1, D_MODEL), mod, w_in_bf, b_gate.reshape(1, GATE_TOT), *tables, ones64, ones32,
      gains)
    return outs[0], outs[1:5], outs[5:9], outs[9:13]


def _pad_queries(q, pieces, kw, unit):
    tq = q.shape[1]
    r = lax.broadcasted_iota(jnp.int32, (kw, tq), 0)
    blocks = []
    for row0, size, extra in pieces:
        tiled = jnp.concatenate([q[row0:row0 + size, :]] * (kw // size), axis=0)
        off = unit * HEAD_DIM + extra
        blocks.append(jnp.where((r >= off) & (r < off + size), tiled, 0.0))
    out = blocks[0] if len(blocks) == 1 else jnp.concatenate(blocks, axis=1)
    return out.astype(BF16)


_PIECES = {
    "gqa": ((0, 64, 0), (64, 64, 0)),
    "diff": ((0, 32, 0), (32, 32, 32)),
    "mha": ((0, 64, 0),),
}


def _flash_kernel(*refs, n_tiles, n_in, tq, **static):
    if n_tiles == 1:
        _flash_tile(*refs, tq=tq, **static)
        return

    def tile(i, carry):
        cols = pl.ds(pl.multiple_of(i * tq, LANES), tq)
        views = list(refs)
        views[0] = refs[0].at[:, :, cols]
        views[n_in] = refs[n_in].at[:, :, cols]
        _flash_tile(*views, tq=tq, **static)
        return carry

    lax.fori_loop(0, n_tiles, tile, 0)


def _flash_tile(*refs, mode, kw, tq, max_keys, chunks, use_sink, lam_init, aliased):
    refs = list(refs)
    qT_ref, k_ref, vT_ref = refs[0:3]
    pos = 3
    sink_ref = None
    if use_sink:
        sink_ref = refs[pos]
        pos += 1
    if mode == "diff":
        lamvec_ref, subg_ref = refs[pos:pos + 2]
        pos += 2
    if aliased:
        pos += 1
    o_ref = refs[pos]
    s_bufs = refs[pos + 1:pos + 4]

    unit = pl.program_id(1)
    pieces = _PIECES[mode]
    ng = len(pieces)
    n = ng * tq
    qpad = _pad_queries(qT_ref[0].astype(F32), pieces, kw, unit)

    if use_sink:
        m0 = jnp.concatenate([jnp.full((1, tq), sink_ref[ng * unit + g] * LOG2E, F32) for g in range(ng)],
                             axis=1)
        l0 = jnp.ones((1, n), F32)
    else:
        m0 = jnp.full((1, n), NEG_INF, F32)
        l0 = jnp.zeros((1, n), F32)
    acc0 = jnp.zeros((HEAD_DIM, n), F32)
    ones_rows = jnp.ones((SUM_ROWS, max_keys), BF16)

    def scores(start, size, s_ref):
        s = _dot(k_ref[0, pl.ds(start, size), :], qpad)
        s_ref[0:size, :] = s
        return jnp.max(s, axis=0, keepdims=True)

    def absorb(start, size, s_ref, mc, state):
        m, l, acc = state
        m_new = jnp.maximum(m, mc)
        alpha = jnp.exp2(m - m_new)
        p = jnp.exp2(s_ref[0:size, :] - m_new)
        v_aug = jnp.concatenate([vT_ref[0, :, pl.ds(start, size)], ones_rows[:, 0:size]], axis=0)
        r = _dot(v_aug, p.astype(BF16))
        return m_new, alpha * l + r[HEAD_DIM:HEAD_DIM + 1], alpha * acc + r[0:HEAD_DIM]

    state = (m0, l0, acc0)
    first, main, n_main, tail = chunks

    def main_start(i):
        return pl.multiple_of(main[0] + i * main[1], LANES)

    mc = scores(first[0], first[1], s_bufs[0])
    if n_main == 0:
        state = absorb(first[0], first[1], s_bufs[0], mc, state)
    else:
        mc_next = scores(main_start(0), main[1], s_bufs[1])
        state = absorb(first[0], first[1], s_bufs[0], mc, state)
        mc = mc_next

        def triple(i, carry):
            mc, state = carry[0], carry[1:]
            k = 3 * i
            mc2 = scores(main_start(k + 1), main[1], s_bufs[2])
            state = absorb(main_start(k), main[1], s_bufs[1], mc, state)
            mc0 = scores(main_start(k + 2), main[1], s_bufs[0])
            state = absorb(main_start(k + 1), main[1], s_bufs[2], mc2, state)
            mc1 = scores(main_start(k + 3), main[1], s_bufs[1])
            state = absorb(main_start(k + 2), main[1], s_bufs[0], mc0, state)
            return (mc1,) + tuple(state)

        n_loop = (n_main - 1) // 3
        if n_loop > 0:
            carry = lax.fori_loop(0, n_loop, triple, (mc,) + tuple(state))
            mc, state = carry[0], carry[1:]
        last = main_start(n_main - 1)
        if tail[1] > 0:
            mc_next = scores(tail[0], tail[1], s_bufs[2])
            state = absorb(last, main[1], s_bufs[1], mc, state)
            state = absorb(tail[0], tail[1], s_bufs[2], mc_next, state)
        else:
            state = absorb(last, main[1], s_bufs[1], mc, state)
    m, l, acc = state
    o = acc / l

    if mode == "gqa":
        for g in range(ng):
            o_ref[0, g * HEAD_DIM:(g + 1) * HEAD_DIM, :] = o[:, g * tq:(g + 1) * tq].astype(o_ref.dtype)
    elif mode == "mha":
        o_ref[0] = o.astype(o_ref.dtype)
    else:
        lv = lamvec_ref[...]
        lam = (jnp.exp(jnp.sum(lv[0:1] * lv[1:2], axis=1, keepdims=True))
               - jnp.exp(jnp.sum(lv[2:3] * lv[3:4], axis=1, keepdims=True)) + lam_init)
        d = o[:, 0:tq] - lam * o[:, tq:2 * tq]
        ms = jnp.mean(d * d, axis=0, keepdims=True)
        o_ref[0] = (d * lax.rsqrt(ms + EPS) * subg_ref[...] * (1.0 - lam_init)).astype(o_ref.dtype)


def _flash(qT, k, vT, *, mode, S, C, ctx_only, sink=None, lamvec=None, subg=None, lam_init=0.0, out_prev=None):
    B, _, T = qT.shape
    kw = k.shape[2]
    q_rows = 128 if mode == "gqa" else 64
    units = 256 // q_rows
    if ctx_only:
        tq, n_tiles, q_blk0 = C, 1, S // C
        chunks = ((S, C), (0, 0), 0, (0, 0))
    else:
        tq = FLASH_QUERIES if S % FLASH_QUERIES == 0 else 256
        n_tiles, q_blk0 = S // tq, 0
        tk = FLASH_KEYS if S >= 4 * FLASH_KEYS else 256
        n_main = (S // tk - 1) // 3 * 3 + 1
        chunks = ((S, C), (0, tk), n_main, (n_main * tk, S - n_main * tk))
    max_keys = max(c[1] for c in (chunks[0], chunks[1], chunks[3]))
    in_specs = [
        pl.BlockSpec((1, q_rows, tq * n_tiles), lambda b, u: (b, u, q_blk0)),
        pl.BlockSpec((1, T, kw), lambda b, u: (b, 0, 0)),
        pl.BlockSpec((1, HEAD_DIM, T), lambda b, u: (b, u, 0)),
    ]
    args = [qT, k, vT]
    if sink is not None:
        in_specs.append(pl.BlockSpec(memory_space=pltpu.SMEM))
        args.append(sink)
    if mode == "diff":
        in_specs += [pl.BlockSpec((8, LANES), lambda b, u: (0, 0)),
                     pl.BlockSpec((HEAD_DIM, 1), lambda b, u: (0, 0))]
        args += [lamvec, subg]
    aliases = {}
    if out_prev is not None:
        in_specs.append(pl.BlockSpec(memory_space=pl.ANY))
        aliases = {len(args): 0}
        args.append(out_prev)
    kern = functools.partial(_flash_kernel, n_tiles=n_tiles, n_in=len(args), mode=mode, kw=kw, tq=tq,
                             max_keys=max_keys, chunks=chunks, use_sink=sink is not None, lam_init=lam_init,
                             aliased=out_prev is not None)
    return pl.pallas_call(
        kern,
        grid=(B, units),
        in_specs=in_specs,
        out_specs=pl.BlockSpec((1, q_rows, tq * n_tiles), lambda b, u: (b, u, q_blk0)),
        out_shape=jax.ShapeDtypeStruct((B, 256, T), BF16),
        input_output_aliases=aliases,
        scratch_shapes=[pltpu.VMEM((max_keys, len(_PIECES[mode]) * tq), F32)] * 3,
        compiler_params=_cparams(("arbitrary", "arbitrary")),
        name="flash_" + mode + ("_ctx" if ctx_only else ""),
    )(*args)


WINDOW_BLOCKS = 2


def _window_kernel(qT_ref, k_ref, vT_ref, sink_ref, o_ref, *, S, C):
    step = pl.program_id(1)
    qb = WINDOW
    span = 3 * qb
    n = 2 * qb
    k_ctx = k_ref[0, S:S + C, :]
    for bi in range(WINDOW_BLOCKS):
        i = step * WINDOW_BLOCKS + bi
        cols = slice(bi * qb, (bi + 1) * qb)
        start = pl.multiple_of(jnp.clip((i - 1) * qb, 0, S - span), LANES)
        k_loc = k_ref[0, pl.ds(start, span), :]
        kpos = start + lax.broadcasted_iota(jnp.int32, (span, n), 0)
        qpos = i * qb + lax.broadcasted_iota(jnp.int32, (span, n), 1) % qb
        in_window = jnp.abs(kpos - qpos) <= WINDOW
        for unit in range(2):
            q = qT_ref[0, unit * 128:(unit + 1) * 128, cols].astype(F32)
            qpad = _pad_queries(q, _PIECES["gqa"], LANES, unit)
            s_loc = jnp.where(in_window, _dot(k_loc, qpad), NEG_INF)
            s_ctx = _dot(k_ctx, qpad)
            sink = jnp.concatenate(
                [jnp.full((1, qb), sink_ref[2 * unit + g] * LOG2E, F32) for g in range(2)], axis=1)
            m = jnp.maximum(jnp.maximum(jnp.max(s_loc, axis=0, keepdims=True),
                                        jnp.max(s_ctx, axis=0, keepdims=True)), sink)
            e_loc = jnp.exp2(s_loc - m)
            e_ctx = jnp.exp2(s_ctx - m)
            den = (jnp.sum(e_loc, axis=0, keepdims=True) + jnp.sum(e_ctx, axis=0, keepdims=True)
                   + jnp.exp2(sink - m))
            v_rows = slice(unit * HEAD_DIM, (unit + 1) * HEAD_DIM)
            o = (_dot(vT_ref[0, v_rows, pl.ds(start, span)], e_loc.astype(BF16))
                 + _dot(vT_ref[0, v_rows, S:S + C], e_ctx.astype(BF16))) / den
            for g in range(2):
                head = 2 * unit + g
                o_ref[0, head * HEAD_DIM:(head + 1) * HEAD_DIM, cols] = (
                    o[:, g * qb:(g + 1) * qb].astype(o_ref.dtype))


def _window(qT, k, vT, sink, S, C):
    B, _, T = qT.shape
    tq = WINDOW * WINDOW_BLOCKS
    return pl.pallas_call(
        functools.partial(_window_kernel, S=S, C=C),
        grid=(B, S // tq),
        in_specs=[
            pl.BlockSpec((1, 256, tq), lambda b, i: (b, 0, i)),
            pl.BlockSpec((1, T, LANES), lambda b, i: (b, 0, 0)),
            pl.BlockSpec((1, 2 * HEAD_DIM, T), lambda b, i: (b, 0, 0)),
            pl.BlockSpec(memory_space=pltpu.SMEM),
        ],
        out_specs=pl.BlockSpec((1, 256, tq), lambda b, i: (b, 0, i)),
        out_shape=jax.ShapeDtypeStruct((B, 256, T), BF16),
        compiler_params=_cparams(("arbitrary", "arbitrary")),
        name="window_attn",
    )(qT, k, vT, sink)


def _nbr_kernel(qT_ref, k_ref, vT_ref, tab_ref, o_ref, *, S, C):
    j = pl.program_id(1)
    rows = S // GRID_W
    n = NBR_Q_ROWS * GRID_W
    span = NBR_WIN_ROWS * GRID_W
    w0 = jnp.clip(NBR_Q_ROWS * j - NA_ROWS // 2, 0, rows - NBR_WIN_ROWS)
    shift = w0 - NBR_Q_ROWS * j + NA_ROWS // 2 + 4
    start = pl.multiple_of(w0 * GRID_W, LANES)
    k_loc = k_ref[0, pl.ds(start, span), :]
    k_ctx = k_ref[0, S:S + C, :]
    kr = w0 + lax.broadcasted_iota(jnp.int32, (span, n), 0) // GRID_W
    qr = NBR_Q_ROWS * j + lax.broadcasted_iota(jnp.int32, (span, n), 1) // GRID_W
    r0 = jnp.clip(qr - NA_ROWS // 2, 0, rows - NA_ROWS)
    in_rows = (kr >= r0) & (kr < r0 + NA_ROWS)
    for head in range(4):
        h_rows = slice(head * HEAD_DIM, (head + 1) * HEAD_DIM)
        qpad = _pad_queries(qT_ref[0, h_rows, :].astype(F32), _PIECES["mha"], 256, head)
        bias = tab_ref[head, pl.ds(pl.multiple_of(shift * GRID_W, GRID_W), span), :]
        s_loc = jnp.where(in_rows, _dot(k_loc, qpad) + bias, NEG_INF)
        s_ctx = _dot(k_ctx, qpad)
        m = jnp.maximum(jnp.max(s_loc, axis=0, keepdims=True), jnp.max(s_ctx, axis=0, keepdims=True))
        e_loc = jnp.exp2(s_loc - m)
        e_ctx = jnp.exp2(s_ctx - m)
        den = jnp.sum(e_loc, axis=0, keepdims=True) + jnp.sum(e_ctx, axis=0, keepdims=True)
        o = (_dot(vT_ref[0, h_rows, pl.ds(start, span)], e_loc.astype(BF16))
             + _dot(vT_ref[0, h_rows, S:S + C], e_ctx.astype(BF16))) / den
        o_ref[0, h_rows, :] = o.astype(o_ref.dtype)


def _nbr_bias_table(rpb):
    u = np.arange(NBR_TABLE_ROWS)[:, None, None, None]
    kc = np.arange(GRID_W)[None, :, None, None]
    e = np.arange(NBR_Q_ROWS)[None, None, :, None]
    qc = np.arange(GRID_W)[None, None, None, :]
    dr = u - e - 1
    row_ok = (dr >= 0) & (dr < 2 * NA_ROWS - 1)
    dc = np.clip(kc - qc, -(NA_COLS - 1), NA_COLS - 1) + (NA_COLS - 1)
    c0 = np.clip(qc - NA_COLS // 2, 0, GRID_W - NA_COLS)
    col_ok = (kc >= c0) & (kc < c0 + NA_COLS)
    shape = (NBR_TABLE_ROWS, GRID_W, NBR_Q_ROWS, GRID_W)
    n_dr, n_dc = 2 * NA_ROWS - 1, 2 * NA_COLS - 1
    sel_r = ((dr[:, 0, :, 0, None] == np.arange(n_dr)) & row_ok[:, 0, :, 0, None]).astype(np.float32)
    sel_c = (dc[0, :, 0, :, None] == np.arange(n_dc)).astype(np.float32)
    vals = jnp.einsum("uer,hrc,kqc->hukeq", sel_r, rpb.astype(F32) * LOG2E, sel_c,
                      precision=lax.Precision.HIGHEST)
    vals = jnp.where(np.broadcast_to(col_ok, shape), vals, NEG_INF)
    return vals.reshape(rpb.shape[0], NBR_TABLE_ROWS * GRID_W, NBR_Q_ROWS * GRID_W)


def _nbr(qT, k, vT, table, S, C):
    B, _, T = qT.shape
    tq = NBR_Q_ROWS * GRID_W
    return pl.pallas_call(
        functools.partial(_nbr_kernel, S=S, C=C),
        grid=(B, S // tq),
        in_specs=[
            pl.BlockSpec((1, 256, tq), lambda b, j: (b, 0, j)),
            pl.BlockSpec((1, T, 256), lambda b, j: (b, 0, 0)),
            pl.BlockSpec((1, 256, T), lambda b, j: (b, 0, 0)),
            pl.BlockSpec((4, NBR_TABLE_ROWS * GRID_W, tq), lambda b, j: (0, 0, 0)),
        ],
        out_specs=pl.BlockSpec((1, 256, tq), lambda b, j: (b, 0, j)),
        out_shape=jax.ShapeDtypeStruct((B, 256, T), BF16),
        compiler_params=_cparams(("arbitrary", "arbitrary")),
        name="nbr_attn",
    )(qT, k, vT, table)


ROUTE_W, ROUTE_E, ROUTE_RANK = 0, 4, 8
ROUTE_ROWS = 16
MOE_GROUPS = 1


def _merge_kernel(oa_ref, ob_ref, oc_ref, od_ref, gate_ref, xa_ref, xb_ref, mod_ref, wb_ref, wo_ref, n2_ref,
                  rw_ref, rb_ref, tri_ref, xo_ref, h_ref, route_ref, cnt_ref, base_ref, *, group_tiles,
                  n_lat_tiles):
    @pl.when(pl.program_id(0) % group_tiles == 0)
    def _():
        base_ref[...] = jnp.zeros_like(base_ref)

    acc = None
    for nbr, o_ref in enumerate((oa_ref, ob_ref, oc_ref, od_ref)):
        proj = lax.dot_general(o_ref[0], wb_ref[nbr], (((0,), (0,)), ((), ())),
                               preferred_element_type=F32)
        term = gate_ref[:, nbr * D_MODEL:(nbr + 1) * D_MODEL].astype(F32) * proj
        acc = term if acc is None else acc + term
    mix = _dot(acc.astype(BF16), wo_ref[...])
    mod = mod_ref[0]
    g1 = mod[:, 2 * D_MODEL:3 * D_MODEL]
    sh2 = mod[:, 3 * D_MODEL:4 * D_MODEL]
    sc2 = mod[:, 4 * D_MODEL:5 * D_MODEL]
    xn = jnp.where(pl.program_id(0) < n_lat_tiles, xa_ref[...], xb_ref[...]) + g1 * mix
    xo_ref[...] = xn
    ms = jnp.mean(xn * xn, axis=-1, keepdims=True)
    h = (xn * lax.rsqrt(ms + EPS) * n2_ref[...]) * (1 + sc2) + sh2
    h_ref[...] = h
    logits = lax.dot_general(rw_ref[...], h.astype(BF16), (((1,), (1,)), ((), ())),
                             preferred_element_type=F32) + rb_ref[...]
    row_f = lax.broadcasted_iota(jnp.int32, logits.shape, 0).astype(F32)
    work = logits
    picks = []
    for _ in range(TOP_K):
        top = jnp.max(work, axis=0, keepdims=True)
        idx = jnp.min(jnp.where(work == top, row_f, float(N_EXPERTS)), axis=0, keepdims=True)
        hit = row_f == idx
        picks.append((top, idx, hit))
        work = jnp.where(hit, -jnp.inf, work)
    ex = [jnp.exp(top - picks[0][0]) for top, _, _ in picks]
    den = ex[0] + ex[1] + ex[2] + ex[3]
    chosen = jnp.zeros(logits.shape, F32)
    for _, _, hit in picks:
        chosen = chosen + hit.astype(F32)
    base = base_ref[...][:, 0:1]
    before = _dot(chosen.astype(BF16), tri_ref[...]) + base
    fields = ([ex[k] / den for k in range(TOP_K)] + [idx for _, idx, _ in picks]
              + [jnp.sum(jnp.where(hit, before, 0.0), axis=0, keepdims=True) for _, _, hit in picks])
    fields.append(jnp.zeros((ROUTE_ROWS - len(fields), logits.shape[1]), F32))
    route_ref[...] = jnp.concatenate(fields, axis=0)
    base_ref[...] = base_ref[...] + jnp.sum(chosen, axis=1, keepdims=True)
    cnt_ref[0] = base_ref[...]


def _merge(oTs, gate, X, mod, wb_bf, wo_bf, norm2_g, rw_t, rb_col, n_rows, B, S, C, row_batch):
    nt = n_rows // ROW_TILE
    per = S // ROW_TILE
    nlat = B * per

    def bidx(t):
        return jnp.where(t < nlat, t // per, t - nlat)

    def pidx(t):
        return jnp.where(t < nlat, t % per, per)

    o_spec = pl.BlockSpec((1, 256, ROW_TILE), lambda t: (bidx(t), 0, pidx(t)))
    x_specs, x_args = _stream_specs(X, nlat)
    r = np.arange(ROW_TILE)
    strict_upper = jnp.asarray((r[:, None] < r[None, :]).astype(np.float32), dtype=BF16)
    group_tiles = nt // MOE_GROUPS
    assert nt % MOE_GROUPS == 0
    return pl.pallas_call(
        functools.partial(_merge_kernel, group_tiles=group_tiles, n_lat_tiles=nlat),
        grid=(nt,),
        in_specs=[o_spec, o_spec, o_spec, o_spec,
                  pl.BlockSpec((ROW_TILE, GATE_TOT), lambda t: (t, 0))] + x_specs + [
                  pl.BlockSpec((1, 1, N_MOD * D_MODEL), lambda t: (row_batch(t), 0, 0)),
                  pl.BlockSpec((4, 256, D_MODEL), lambda t: (0, 0, 0)),
                  pl.BlockSpec((D_MODEL, D_MODEL), lambda t: (0, 0)),
                  pl.BlockSpec((1, D_MODEL), lambda t: (0, 0)),
                  pl.BlockSpec((N_EXPERTS, D_MODEL), lambda t: (0, 0)),
                  pl.BlockSpec((N_EXPERTS, 1), lambda t: (0, 0)),
                  pl.BlockSpec((ROW_TILE, ROW_TILE), lambda t: (0, 0))],
        out_specs=[pl.BlockSpec((ROW_TILE, D_MODEL), lambda t: (t, 0)),
                   pl.BlockSpec((ROW_TILE, D_MODEL), lambda t: (t, 0)),
                   pl.BlockSpec((ROUTE_ROWS, ROW_TILE), lambda t: (0, t)),
                   pl.BlockSpec((1, N_EXPERTS, LANES), lambda t: (t // group_tiles, 0, 0))],
        out_shape=[jax.ShapeDtypeStruct((n_rows, D_MODEL), F32),
                   jax.ShapeDtypeStruct((n_rows, D_MODEL), F32),
                   jax.ShapeDtypeStruct((ROUTE_ROWS, n_rows), F32),
                   jax.ShapeDtypeStruct((MOE_GROUPS, N_EXPERTS, LANES), F32)],
        scratch_shapes=[pltpu.VMEM((N_EXPERTS, LANES), F32)],
        compiler_params=_cparams(("arbitrary",)),
        name="merge",
    )(*oTs, gate, *x_args, mod, wb_bf, wo_bf, norm2_g.reshape(1, D_MODEL), rw_t, rb_col, strict_upper)


def _expert_kernel(tok_ref, off_ref, first_ref, nblk_ref, nu_ref, h_hbm, wgu_ref, bgu_ref, wd_ref, bd_ref,
                   y_hbm, x_0, x_1, x_2, y_0, y_1, y_2, gsems, ysems, wgu_s, wd_s):
    e = pl.program_id(0)
    n_used = nu_ref[0]
    x_bufs = (x_0, x_1, x_2)
    y_bufs = (y_0, y_1, y_2)

    def start_gather(block, x_dst, sem):
        base = off_ref[block]
        for r in range(EXPERT_ROWS):
            pltpu.make_async_copy(h_hbm.at[pl.ds(tok_ref[base + r], 1), :], x_dst.at[pl.ds(r, 1), :],
                                  sem).start(priority=r % 2)

    def wait_gather(x_dst, sem):
        pltpu.make_async_copy(h_hbm.at[pl.ds(0, EXPERT_ROWS), :], x_dst, sem).wait()

    def y_copy(g, cur):
        rows = pl.ds(pl.multiple_of(g * EXPERT_ROWS, EXPERT_ROWS), EXPERT_ROWS)
        return pltpu.make_async_copy(y_bufs[cur], y_hbm.at[rows, :], ysems.at[cur])

    def block(cur, g):
        ahead = (cur + 2) % 3

        @pl.when(g >= 3)
        def _():
            y_copy(g - 3, cur).wait()

        wait_gather(x_bufs[cur], gsems.at[cur])
        start_gather(jnp.minimum(g + 2, n_used - 1), x_bufs[ahead], gsems.at[ahead])
        x = x_bufs[cur][...].astype(BF16)
        gu = _dot(x, wgu_s[...]) + bgu_ref[0]
        gate = jnp.minimum(gu[:, :D_FF], SWIGLU_LIMIT)
        up = jnp.clip(gu[:, D_FF:], -SWIGLU_LIMIT, SWIGLU_LIMIT)
        a = gate * jax.nn.sigmoid(SWIGLU_ALPHA * gate) * (up + 1)
        y_bufs[cur][...] = (_dot(a.astype(BF16), wd_s[...]) + bd_ref[0]).astype(BF16)
        y_copy(g, cur).start()

    @pl.when(e == 0)
    def _():
        start_gather(0, x_bufs[0], gsems.at[0])
        start_gather(jnp.minimum(1, n_used - 1), x_bufs[1], gsems.at[1])

    n_blocks = nblk_ref[e]

    @pl.when(n_blocks > 0)
    def _():
        wgu_s[...] = wgu_ref[0].astype(BF16)
        wd_s[...] = wd_ref[0].astype(BF16)

    def body(j, carry):
        g = first_ref[e] + j
        for cur in range(3):
            pl.when(g % 3 == cur)(functools.partial(block, cur, g))
        return carry

    lax.fori_loop(0, n_blocks, body, 0)

    @pl.when(e == pl.num_programs(0) - 1)
    def _():
        last = (n_used - 1) % 3
        for b in range(3):
            @pl.when(b != last)
            def _():
                wait_gather(x_bufs[b], gsems.at[b])

            @pl.when(n_used > b)
            def _():
                y_copy(0, b).wait()


def _experts(h, tok_sorted, blk_off, e_first, e_nblk, n_used, n_blk, layer, w_gate_up, b_gate_up, w_down,
             b_down):
    E = w_gate_up.shape[0]

    def w_map(e, *_):
        return (e + layer * N_EXPERTS, 0, 0)

    grid_spec = pltpu.PrefetchScalarGridSpec(
        num_scalar_prefetch=5,
        grid=(N_EXPERTS,),
        in_specs=[
            pl.BlockSpec(memory_space=pl.ANY),
            pl.BlockSpec((1, D_MODEL, 2 * D_FF), w_map),
            pl.BlockSpec((1, 1, 2 * D_FF), w_map),
            pl.BlockSpec((1, D_FF, D_MODEL), w_map),
            pl.BlockSpec((1, 1, D_MODEL), w_map),
        ],
        out_specs=pl.BlockSpec(memory_space=pl.ANY),
        scratch_shapes=([pltpu.VMEM((EXPERT_ROWS, D_MODEL), F32)] * 3
                        + [pltpu.VMEM((EXPERT_ROWS, D_MODEL), BF16)] * 3
                        + [pltpu.SemaphoreType.DMA((3,)), pltpu.SemaphoreType.DMA((3,)),
                           pltpu.VMEM((D_MODEL, 2 * D_FF), BF16), pltpu.VMEM((D_FF, D_MODEL), BF16)]),
    )
    return pl.pallas_call(
        _expert_kernel,
        grid_spec=grid_spec,
        out_shape=jax.ShapeDtypeStruct((n_blk * EXPERT_ROWS, D_MODEL), BF16),
        compiler_params=_cparams(("arbitrary",)),
        name="experts",
    )(tok_sorted, blk_off, e_first, e_nblk, n_used, h, w_gate_up, b_gate_up.reshape(E, 1, 2 * D_FF), w_down,
      b_down.reshape(E, 1, D_MODEL))


def _combine_kernel(yg_ref, route_ref, x_ref, mod_ref, *rest):
    o_ref = rest[-1]
    route = route_ref[...]
    y = None
    for k in range(TOP_K):
        term = route[:, ROUTE_W + k:ROUTE_W + k + 1] * yg_ref[k].astype(F32)
        y = term if y is None else y + term
    g2 = mod_ref[0][:, 5 * D_MODEL:6 * D_MODEL]
    o_ref[...] = x_ref[...] + g2 * y


def _combine(yg, route, X, mod, row_batch, tile0, out_prev):
    n_rows = X.shape[0]
    in_specs = [pl.BlockSpec((TOP_K, ROW_TILE, D_MODEL), lambda t: (0, t, 0)),
                pl.BlockSpec((ROW_TILE, ROUTE_ROWS), lambda t: (t + tile0, 0)),
                pl.BlockSpec((ROW_TILE, D_MODEL), lambda t: (t + tile0, 0)),
                pl.BlockSpec((1, 1, N_MOD * D_MODEL), lambda t: (row_batch(t + tile0), 0, 0))]
    args = [yg, route, X, mod]
    aliases = {}
    if out_prev is not None:
        in_specs.append(pl.BlockSpec(memory_space=pl.ANY))
        aliases = {len(args): 0}
        args.append(out_prev)
    return pl.pallas_call(
        _combine_kernel,
        grid=(yg.shape[1] // ROW_TILE,),
        in_specs=in_specs,
        out_specs=pl.BlockSpec((ROW_TILE, D_MODEL), lambda t: (t + tile0, 0)),
        out_shape=jax.ShapeDtypeStruct((n_rows, D_MODEL), F32),
        input_output_aliases=aliases,
        compiler_params=_cparams(("arbitrary",)),
        name="moe_combine",
    )(*args)


def _moe(h, route, counts_f, X, mod, row_batch, layer, w_gate_up, b_gate_up, w_down, b_down):
    n_groups = counts_f.shape[0]
    Tg = h.shape[0] // n_groups
    n = Tg * TOP_K
    n_blk = n // EXPERT_ROWS + N_EXPERTS
    experts = jnp.arange(N_EXPERTS, dtype=jnp.int32)
    out = None
    for g in range(n_groups):
        route_g = lax.slice_in_dim(route, g * Tg, (g + 1) * Tg, axis=0)
        top_e = route_g[:, ROUTE_E:ROUTE_E + TOP_K].astype(jnp.int32)
        rank = route_g[:, ROUTE_RANK:ROUTE_RANK + TOP_K].astype(jnp.int32)
        counts = counts_f[g, :, 0].astype(jnp.int32)
        padded = (counts + EXPERT_ROWS - 1) // EXPERT_ROWS * EXPERT_ROWS
        pad_end = jnp.cumsum(padded)
        pad_start = pad_end - padded
        start_of = jnp.sum(jnp.where(top_e[:, :, None] == experts, pad_start, 0), axis=-1)
        dest = start_of + rank
        tok = jnp.broadcast_to(jnp.arange(g * Tg, (g + 1) * Tg, dtype=jnp.int32)[:, None], (Tg, TOP_K))
        _, tok_sorted = lax.sort_key_val(dest.reshape(-1), tok.reshape(-1))
        tok_sorted = jnp.concatenate([tok_sorted, jnp.zeros((EXPERT_ROWS,), jnp.int32)])
        n_used = (pad_end[-1] // EXPERT_ROWS).astype(jnp.int32)
        blk_row = jnp.minimum(jnp.arange(n_blk, dtype=jnp.int32), n_used - 1) * EXPERT_ROWS
        blk_e = jnp.sum((pad_end[None, :] <= blk_row[:, None]).astype(jnp.int32), axis=1)
        blk_e = jnp.minimum(blk_e, N_EXPERTS - 1)
        pad_before = pad_start - (jnp.cumsum(counts) - counts)
        blk_off = blk_row - jnp.sum(jnp.where(blk_e[:, None] == experts, pad_before, 0), axis=-1)
        y = _experts(h, tok_sorted, blk_off, pad_start // EXPERT_ROWS, padded // EXPERT_ROWS,
                     n_used.reshape(1), n_blk, layer, w_gate_up, b_gate_up, w_down, b_down)
        yg = y.at[dest.T.reshape(-1)].get(mode="promise_in_bounds").reshape(TOP_K, Tg, D_MODEL)
        out = _combine(yg, route, X, mod, row_batch, g * (Tg // ROW_TILE), out)
    return out


def _rope_tables(S, dim):
    t = jnp.arange(S, dtype=jnp.int32)
    row = (t // GRID_W).astype(F32)
    col = (t % GRID_W).astype(F32)
    n_freq = dim // 4
    inv = ROPE_THETA ** (-jnp.arange(n_freq, dtype=F32) / n_freq)
    ang = jnp.concatenate([row[:, None] * inv, col[:, None] * inv], axis=-1)
    cos, sin = jnp.cos(ang), jnp.sin(ang)
    reps = LANES // dim
    cos_t = jnp.tile(jnp.concatenate([cos, cos], axis=-1), (1, reps))
    sin_t = jnp.tile(jnp.concatenate([-sin, sin], axis=-1), (1, reps))
    cos_t = jnp.concatenate([cos_t, jnp.ones((ROW_TILE, LANES), F32)], axis=0)
    sin_t = jnp.concatenate([sin_t, jnp.zeros((ROW_TILE, LANES), F32)], axis=0)
    return cos_t, sin_t


def _block_ones(seg):
    i = np.arange(LANES)
    return jnp.asarray((i[:, None] // seg == i[None, :] // seg).astype(np.float32), dtype=BF16)


def _lane_tile(v):
    return jnp.tile(v.astype(F32), LANES // v.shape[0])


def kernel(x, c, ctx, c_ctx, norm1_g, norm2_g, w_ada, b_ada, w_in, b_gate, a_qn, a_kn, b_qn, b_kn, lam_q1, lam_k1, lam_q2, lam_k2, subln_g, c_qn, c_kn, sink, d_qn, d_kn, rpb, w_branch, w_out, router_w, router_b, w_gate_up, b_gate_up, w_down, b_down):
    B, S, D = x.shape
    C = ctx.shape[1]
    L = w_in.shape[0]
    assert D == D_MODEL and C == ROW_TILE and S % ROW_TILE == 0 and B + 1 <= 8
    n_lat = B * S
    per = S // ROW_TILE
    nlat_tiles = B * per

    def row_batch(t):
        return jnp.where(t < nlat_tiles, t // per, B)

    cvec = jnp.zeros((8, D), F32).at[:B].set(c).at[B].set(c_ctx)
    mod_all = _ada(cvec, w_ada, b_ada)

    tables = _rope_tables(S, HEAD_DIM) + _rope_tables(S, B_DK)
    ones64, ones32 = _block_ones(HEAD_DIM), _block_ones(B_DK)

    E = w_gate_up.shape[1]
    w_gu_all = w_gate_up.reshape(L * E, D, 2 * D_FF)
    w_dn_all = w_down.reshape(L * E, D_FF, D)

    X = (x.reshape(n_lat, D), ctx.reshape(B * C, D))
    for l in range(L):
        last = l == L - 1
        lam_init = 0.8 - 0.6 * math.exp(-0.3 * l)
        mod = mod_all[l].reshape(8, 1, N_MOD * D)
        gains = jnp.stack([_lane_tile(g[l]) for g in (a_qn, a_kn, b_qn, b_kn, c_qn, c_kn, d_qn, d_kn)])
        gate, qTs, ks, vTs = _in_proj(X, norm1_g[l], mod, w_in[l].astype(BF16), b_gate[l], row_batch,
                                      tables, ones64, ones32, gains, B, S, C)

        lamvec = jnp.zeros((8, LANES), F32)
        for r, v in enumerate((lam_q1, lam_k1, lam_q2, lam_k2)):
            lamvec = lamvec.at[r, :B_DK].set(v[l])
        subg = subln_g[l].reshape(HEAD_DIM, 1)
        sink_l = sink[l].astype(F32)
        nbr_table = _nbr_bias_table(rpb[l])

        o_a = _flash(qTs[0], ks[0], vTs[0], mode="gqa", S=S, C=C, ctx_only=False)
        o_b = _flash(qTs[1], ks[1], vTs[1], mode="diff", S=S, C=C, ctx_only=False,
                     lamvec=lamvec, subg=subg, lam_init=lam_init)
        o_c = _window(qTs[2], ks[2], vTs[2], sink_l, S, C)
        o_d = _nbr(qTs[3], ks[3], vTs[3], nbr_table, S, C)
        if not last:
            o_a = _flash(qTs[0], ks[0], vTs[0], mode="gqa", S=S, C=C, ctx_only=True, out_prev=o_a)
            o_b = _flash(qTs[1], ks[1], vTs[1], mode="diff", S=S, C=C, ctx_only=True,
                         lamvec=lamvec, subg=subg, lam_init=lam_init, out_prev=o_b)
            o_c = _flash(qTs[2], ks[2], vTs[2], mode="gqa", S=S, C=C, ctx_only=True, sink=sink_l, out_prev=o_c)
            o_d = _flash(qTs[3], ks[3], vTs[3], mode="mha", S=S, C=C, ctx_only=True, out_prev=o_d)

        n_rows = n_lat if last else _stream_rows(X)
        Xm, h2, route_t, counts = _merge((o_a, o_b, o_c, o_d), gate, X, mod, w_branch[l].astype(BF16),
                                         w_out[l].astype(BF16), norm2_g[l], router_w[l].T.astype(BF16),
                                         router_b[l].astype(F32).reshape(N_EXPERTS, 1), n_rows, B, S, C,
                                         row_batch)
        X = _moe(h2, route_t.T, counts, Xm, mod, row_batch, l, w_gu_all, b_gate_up.reshape(L * E, -1),
                 w_dn_all, b_down.reshape(L * E, -1))
    return X[:n_lat].reshape(B, S, D)
```
